```python
import math
import jax
import jax.numpy as jnp
from jax import lax
import numpy as np

D_MODEL = 4096
BATCH = 32
SEQ = 256
DEPTH = 2
DEC_BATCH = 8
DEC_SEQ = 2048
PAST_LEN = 256

GRID_W = 64
EPS = 1e-6
HEAD_DIM = 128
ATT_HEADS = 16
ATT_KV_HEADS = 4
ATT_GROUPS = ATT_HEADS // ATT_KV_HEADS
ATT_WIDTH = ATT_HEADS * HEAD_DIM
WINDOW = 128
Q_BLOCK = 128
ROPE_THETA = 10000.0
ROPE_FREQS = HEAD_DIM // 4
GLA_HEADS = 8
GLA_DK = 128
GLA_DV = 128
GLA_WIDTH = GLA_HEADS * GLA_DV
GLA_LOWRANK = 16
GLA_TAU = 16.0
GLA_CHUNK = 64
LRU_WIDTH = 1024
LRU_BLOCKS = 8
LRU_BLOCK = LRU_WIDTH // LRU_BLOCKS
LRU_CONV = 4
CONV_LEFT = 2
CONV_RIGHT = LRU_CONV - 1 - CONV_LEFT
LRU_C = 8.0

MIX_WIDTH = ATT_WIDTH + GLA_WIDTH + LRU_WIDTH
D_FF = -(-8 * D_MODEL // (3 * 256)) * 256
IN_SIZES = (ATT_WIDTH, ATT_KV_HEADS * HEAD_DIM, ATT_KV_HEADS * HEAD_DIM,
            GLA_HEADS * GLA_DK, GLA_HEADS * GLA_DK, GLA_WIDTH, GLA_WIDTH, 2 * GLA_LOWRANK,
            LRU_WIDTH, LRU_WIDTH)
IN_WIDTH = sum(IN_SIZES)
IN_SPLITS = tuple(sum(IN_SIZES[:i + 1]) for i in range(len(IN_SIZES) - 1))

kernel_name = "hybrid_dit_prefix_context_step"

F32 = jnp.float32


def _rmsnorm(x, g):
    xf = x.astype(F32)
    y = xf * lax.rsqrt(jnp.mean(xf * xf, axis=-1, keepdims=True) + EPS)
    return (y * g.astype(F32)).astype(x.dtype)


def _axial_rope_tables(rows):
    row = jnp.repeat(jnp.arange(rows, dtype=F32), GRID_W)
    col = jnp.tile(jnp.arange(GRID_W, dtype=F32), rows)
    inv = ROPE_THETA ** (-jnp.arange(ROPE_FREQS, dtype=F32) / ROPE_FREQS)
    ang_r = row[:, None] * inv[None, :]
    ang_c = col[:, None] * inv[None, :]
    return (jnp.cos(ang_r), jnp.sin(ang_r), jnp.cos(ang_c), jnp.sin(ang_c))


def _rotate(v, cos, sin):
    c = cos[None, :, None, :]
    s = sin[None, :, None, :]
    v1, v2 = v[..., :ROPE_FREQS], v[..., ROPE_FREQS:]
    return jnp.concatenate([v1 * c - v2 * s, v1 * s + v2 * c], axis=-1)


def _apply_axial_rope(x, tables):
    cr, sr, cc, sc = tables
    xf = x.astype(F32)
    half = HEAD_DIM // 2
    y = jnp.concatenate([_rotate(xf[..., :half], cr, sr), _rotate(xf[..., half:], cc, sc)], axis=-1)
    return y.astype(x.dtype)


def _attend(q, k, v, mask, sink):
    s = jnp.einsum('bqkgd,bskd->bkgqs', q, k, preferred_element_type=F32) * (HEAD_DIM ** -0.5)
    if mask is not None:
        s = jnp.where(mask, s, -jnp.inf)
    sk = sink.astype(F32)[None, :, :, None, None]
    m = jnp.maximum(jnp.max(s, axis=-1, keepdims=True), sk)
    p = jnp.exp(s - m)
    denom = jnp.sum(p, axis=-1, keepdims=True) + jnp.exp(sk - m)
    return jnp.einsum('bkgqs,bskd->bqkgd', (p / denom).astype(v.dtype), v)


def _context_attention(q, k, v, sink):
    b, s = q.shape[:2]
    nb = s // Q_BLOCK
    qb = q.reshape(b, nb, Q_BLOCK, ATT_KV_HEADS, ATT_GROUPS, HEAD_DIM).transpose(1, 0, 2, 3, 4, 5)
    o = lax.map(lambda qi: _attend(qi, k, v, None, sink), qb)
    return o.transpose(1, 0, 2, 3, 4, 5).reshape(b, s, ATT_WIDTH)


def _latent_attention(q, k, v, k_ctx, v_ctx, sink):
    b, n = q.shape[:2]
    nb = n // Q_BLOCK
    span = Q_BLOCK + 2 * WINDOW
    ctx_len = k_ctx.shape[1]
    qb = q.reshape(b, nb, Q_BLOCK, ATT_KV_HEADS, ATT_GROUPS, HEAD_DIM).transpose(1, 0, 2, 3, 4, 5)
    pad = ((0, 0), (WINDOW, WINDOW), (0, 0), (0, 0))
    kp = jnp.pad(k, pad)
    vp = jnp.pad(v, pad)
    ctx_mask = jnp.ones((Q_BLOCK, ctx_len), dtype=bool)

    def one_block(args):
        i, qi = args
        start = i * Q_BLOCK
        kw = lax.dynamic_slice_in_dim(kp, start, span, axis=1)
        vw = lax.dynamic_slice_in_dim(vp, start, span, axis=1)
        qpos = start + jnp.arange(Q_BLOCK)
        kpos = start - WINDOW + jnp.arange(span)
        win = ((jnp.abs(qpos[:, None] - kpos[None, :]) <= WINDOW)
               & (kpos >= 0)[None, :] & (kpos < n)[None, :])
        mask = jnp.concatenate([win, ctx_mask], axis=1)
        return _attend(qi, jnp.concatenate([kw, k_ctx], axis=1),
                       jnp.concatenate([vw, v_ctx], axis=1), mask, sink)

    o = lax.map(one_block, (jnp.arange(nb), qb))
    return o.transpose(1, 0, 2, 3, 4, 5).reshape(b, n, ATT_WIDTH)


def _gla_scan(q, k, v, log_a, s0):
    b, n = q.shape[:2]
    nc = n // GLA_CHUNK

    def chunks(t):
        return t.reshape(b, nc, GLA_CHUNK, GLA_HEADS, t.shape[-1]).transpose(1, 0, 3, 2, 4)

    causal = jnp.tril(jnp.ones((GLA_CHUNK, GLA_CHUNK), dtype=bool))

    def step(state, inp):
        qc, kc, vc, ac = inp
        cum = jnp.cumsum(ac, axis=2)
        total = cum[:, :, -1:, :]
        q_in = qc * jnp.exp(cum)
        k_in = kc * jnp.exp(-cum)
        att = jnp.where(causal, jnp.einsum('bhtd,bhsd->bhts', q_in, k_in), 0.0)
        o = (jnp.einsum('bhts,bhsv->bhtv', att, vc)
             + jnp.einsum('bhtd,bhdv->bhtv', q_in, state))
        k_dec = kc * jnp.exp(total - cum)
        new_state = (jnp.exp(total)[:, :, 0, :, None] * state
                     + jnp.einsum('bhsd,bhsv->bhdv', k_dec, vc))
        return new_state, o

    final, o = lax.scan(step, s0, (chunks(q), chunks(k), chunks(v), chunks(log_a)))
    o = o.transpose(1, 0, 3, 2, 4).reshape(b, n, GLA_HEADS, GLA_DV)
    return o, final


def _gla_mixer(q, k, v, g, dec_lr, w_dec, b_dec, norm_g, init):
    b, n, _ = q.shape
    q = (q.astype(F32) * (GLA_DK ** -0.5)).reshape(b, n, GLA_HEADS, GLA_DK)
    k = k.astype(F32).reshape(b, n, GLA_HEADS, GLA_DK)
    v = v.astype(F32).reshape(b, n, GLA_HEADS, GLA_DV)
    lr = dec_lr.astype(F32).reshape(b, n, 2, GLA_LOWRANK)
    z = jnp.einsum('bndr,drc->bndc', lr, w_dec.astype(F32)) + b_dec.astype(F32)
    log_a = (jax.nn.log_sigmoid(z) / GLA_TAU).reshape(b, n, 2, GLA_HEADS, GLA_DK)
    o_f, s_f = _gla_scan(q, k, v, log_a[:, :, 0], init[0])
    flip = lambda t: jnp.flip(t, axis=1)
    o_b, s_b = _gla_scan(flip(q), flip(k), flip(v), flip(log_a[:, :, 1]), init[1])
    o = o_f + flip(o_b)
    o = (o * lax.rsqrt(jnp.mean(o * o, axis=-1, keepdims=True) + EPS)
         * norm_g.astype(F32).reshape(GLA_HEADS, GLA_DV))
    o = o.reshape(b, n, GLA_WIDTH) * jax.nn.silu(g.astype(F32))
    return o, jnp.stack([s_f, s_b], axis=1)


def _depthwise_conv(x, w, bias):
    y = lax.conv_general_dilated(x, w[:, None, :], window_strides=(1,),
                                 padding=[(CONV_LEFT, CONV_RIGHT)],
                                 dimension_numbers=('NWC', 'WIO', 'NWC'),
                                 feature_group_count=x.shape[-1])
    return y + bias


def _linear_combine(left, right):
    a_l, u_l = left
    a_r, u_r = right
    return a_l * a_r, a_r * u_l + u_r


def _rglru_scan(x, w_r, b_r, w_i, b_i, lam, h0):
    b, n, _ = x.shape
    xb = x.reshape(b, n, LRU_BLOCKS, LRU_BLOCK)
    r = jax.nn.sigmoid(jnp.einsum('bnki,kij->bnkj', xb, w_r.astype(F32)).reshape(b, n, LRU_WIDTH)
                       + b_r.astype(F32))
    i = jax.nn.sigmoid(jnp.einsum('bnki,kij->bnkj', xb, w_i.astype(F32)).reshape(b, n, LRU_WIDTH)
                       + b_i.astype(F32))
    log_a = -LRU_C * r * jax.nn.softplus(-lam.astype(F32))
    a = jnp.exp(log_a)
    u = jnp.sqrt(-jnp.expm1(2.0 * log_a)) * (i * x)
    u = u.at[:, 0].add(a[:, 0] * h0)
    _, h = lax.associative_scan(_linear_combine, (a, u), axis=1)
    return h, h[:, -1]


def _rglru_mixer(x, y, conv_w, conv_b, w_r, b_r, w_i, b_i, lam, init):
    xc = _depthwise_conv(x.astype(F32), conv_w.astype(F32), conv_b.astype(F32))
    h_f, s_f = _rglru_scan(xc, w_r[0], b_r[0], w_i[0], b_i[0], lam[0], init[0])
    h_b, s_b = _rglru_scan(jnp.flip(xc, axis=1), w_r[1], b_r[1], w_i[1], b_i[1], lam[1], init[1])
    h = h_f + jnp.flip(h_b, axis=1)
    out = h * jax.nn.gelu(y.astype(F32))
    return out, jnp.stack([s_f, s_b], axis=1)


def _token_mixers(h, mix_p, cached, rope):
    (w_in, sink, gla_w_dec, gla_b_dec, gla_norm, conv_w, conv_b,
     w_r, b_r, w_i, b_i, lam) = mix_p
    b, n, _ = h.shape
    proj = h @ w_in
    aq, ak, av, gq, gk, gv, gg, gdec, lx, ly = jnp.split(proj, IN_SPLITS, axis=-1)
    aq = aq.reshape(b, n, ATT_HEADS, HEAD_DIM)
    ak = ak.reshape(b, n, ATT_KV_HEADS, HEAD_DIM)
    av = av.reshape(b, n, ATT_KV_HEADS, HEAD_DIM)
    sink_g = sink.reshape(ATT_KV_HEADS, ATT_GROUPS)
    if cached is None:
        attn = _context_attention(aq, ak, av, sink_g)
        gla_init = jnp.zeros((2, b, GLA_HEADS, GLA_DK, GLA_DV), F32)
        lru_init = jnp.zeros((2, b, LRU_WIDTH), F32)
    else:
        k_ctx, v_ctx, gla_state, lru_state = cached
        attn = _latent_attention(_apply_axial_rope(aq, rope), _apply_axial_rope(ak, rope), av,
                                 k_ctx, v_ctx, sink_g)
        gla_init = jnp.moveaxis(gla_state.astype(F32), 1, 0)
        lru_init = jnp.moveaxis(lru_state.astype(F32), 1, 0)
    gla_out, gla_fin = _gla_mixer(gq, gk, gv, gg, gdec, gla_w_dec, gla_b_dec, gla_norm, gla_init)
    lru_out, lru_fin = _rglru_mixer(lx, ly, conv_w, conv_b, w_r, b_r, w_i, b_i, lam, lru_init)
    mix = jnp.concatenate([attn, gla_out.astype(h.dtype), lru_out.astype(h.dtype)], axis=-1)
    return mix, (ak, av, gla_fin, lru_fin)


def _layer(x, mod, n1, n2, mix_p, w_out, w_gu, w_down, cached, rope):
    sh1, sc1, g1, sh2, sc2, g2 = jnp.split(mod, 6, axis=-1)
    h = _rmsnorm(x, n1) * (1.0 + sc1) + sh1
    mix, ctx = _token_mixers(h, mix_p, cached, rope)
    x = x + g1 * (mix @ w_out)
    h = _rmsnorm(x, n2) * (1.0 + sc2) + sh2
    a, u = jnp.split(h @ w_gu, 2, axis=-1)
    x = x + g2 * ((jax.nn.silu(a) * u) @ w_down)
    return x, ctx


def setup_inputs(seed: int = 0) -> dict:
    key = jax.random.key(seed)
    ks = jax.random.split(key, 32)

    def nrm(k, shape, s):
        return jax.random.normal(k, shape, F32) * s

    u = jax.random.uniform(ks[22], (DEPTH, 2, LRU_WIDTH), F32, 0.9, 0.999)
    a = u ** (1.0 / LRU_C)
    lru_lambda = jnp.log(a) - jnp.log1p(-a)
    return {
        "x_prompt": nrm(ks[0], (BATCH, SEQ, D_MODEL), 1.0),
        "x_sample": nrm(ks[1], (DEC_BATCH, DEC_SEQ, D_MODEL), 1.0),
        "cache_attn_k": nrm(ks[2], (DEC_BATCH, DEPTH, PAST_LEN, ATT_KV_HEADS, HEAD_DIM), 1.0),
        "cache_attn_v": nrm(ks[3], (DEC_BATCH, DEPTH, PAST_LEN, ATT_KV_HEADS, HEAD_DIM), 1.0),
        "state_gla": nrm(ks[4], (DEC_BATCH, DEPTH, 2, GLA_HEADS, GLA_DK, GLA_DV), 0.5),
        "state_lru": nrm(ks[5], (DEC_BATCH, DEPTH, 2, LRU_WIDTH), 0.5),
        "c": nrm(ks[6], (DEC_BATCH, D_MODEL), 1.0),
        "c_ctx": nrm(ks[7], (D_MODEL,), 1.0),
        "w_ada": nrm(ks[8], (DEPTH, D_MODEL, 6 * D_MODEL), 0.5 * D_MODEL ** -0.5),
        "b_ada": nrm(ks[9], (DEPTH, 6 * D_MODEL), 0.02),
        "norm1_g": 1.0 + nrm(ks[10], (DEPTH, D_MODEL), 0.02),
        "w_in": nrm(ks[11], (DEPTH, D_MODEL, IN_WIDTH), D_MODEL ** -0.5),
        "attn_sink": nrm(ks[12], (DEPTH, ATT_HEADS), 0.5),
        "gla_w_decay": nrm(ks[13], (DEPTH, 2, GLA_LOWRANK, GLA_HEADS * GLA_DK), GLA_LOWRANK ** -0.5),
        "gla_b_decay": nrm(ks[14], (DEPTH, 2, GLA_HEADS * GLA_DK), 0.1),
        "gla_norm_g": 1.0 + nrm(ks[15], (DEPTH, GLA_WIDTH), 0.02),
        "lru_conv_w": nrm(ks[16], (DEPTH, LRU_CONV, LRU_WIDTH), LRU_CONV ** -0.5),
        "lru_conv_b": nrm(ks[17], (DEPTH, LRU_WIDTH), 0.02),
        "lru_w_rgate": nrm(ks[18], (DEPTH, 2, LRU_BLOCKS, LRU_BLOCK, LRU_BLOCK), LRU_BLOCK ** -0.5),
        "lru_b_rgate": nrm(ks[19], (DEPTH, 2, LRU_WIDTH), 0.02),
        "lru_w_igate": nrm(ks[20], (DEPTH, 2, LRU_BLOCKS, LRU_BLOCK, LRU_BLOCK), LRU_BLOCK ** -0.5),
        "lru_b_igate": nrm(ks[21], (DEPTH, 2, LRU_WIDTH), 0.02),
        "lru_lambda": lru_lambda,
        "w_out": nrm(ks[23], (DEPTH, MIX_WIDTH, D_MODEL), MIX_WIDTH ** -0.5),
        "norm2_g": 1.0 + nrm(ks[24], (DEPTH, D_MODEL), 0.02),
        "w_gu": nrm(ks[25], (DEPTH, D_MODEL, 2 * D_FF), D_MODEL ** -0.5),
        "w_down": nrm(ks[26], (DEPTH, D_FF, D_MODEL), D_FF ** -0.5),
        "final_norm_g": 1.0 + nrm(ks[27], (D_MODEL,), 0.02),
    }


def reference(x_prompt, x_sample, cache_attn_k, cache_attn_v, state_gla, state_lru, c, c_ctx,
              w_ada, b_ada, norm1_g, w_in, attn_sink, gla_w_decay, gla_b_decay, gla_norm_g,
              lru_conv_w, lru_conv_b, lru_w_rgate, lru_b_rgate, lru_w_igate, lru_b_igate,
              lru_lambda, w_out, norm2_g, w_gu, w_down, final_norm_g):
    rows = x_sample.shape[1] // GRID_W
    rope = _axial_rope_tables(rows)
    silu_c = jax.nn.silu(c)
    silu_ctx = jax.nn.silu(c_ctx)
    xc = x_prompt
    xl = x_sample
    new_k, new_v, new_gla, new_lru = [], [], [], []
    for l in range(DEPTH):
        mix_p = (w_in[l], attn_sink[l], gla_w_decay[l], gla_b_decay[l], gla_norm_g[l],
                 lru_conv_w[l], lru_conv_b[l], lru_w_rgate[l], lru_b_rgate[l],
                 lru_w_igate[l], lru_b_igate[l], lru_lambda[l])
        mod_ctx = (silu_ctx @ w_ada[l] + b_ada[l])[None, None, :]
        mod_lat = (silu_c @ w_ada[l] + b_ada[l])[:, None, :]
        xc, (k_l, v_l, g_l, r_l) = _layer(xc, mod_ctx, norm1_g[l], norm2_g[l], mix_p,
                                          w_out[l], w_gu[l], w_down[l], None, rope)
        new_k.append(k_l)
        new_v.append(v_l)
        new_gla.append(g_l)
        new_lru.append(r_l)
        cached = (cache_attn_k[:, l], cache_attn_v[:, l], state_gla[:, l], state_lru[:, l])
        xl, _ = _layer(xl, mod_lat, norm1_g[l], norm2_g[l], mix_p,
                       w_out[l], w_gu[l], w_down[l], cached, rope)
    y_prompt = _rmsnorm(xc, final_norm_g)
    y_sample = _rmsnorm(xl, final_norm_g)
    new_attn_k = jnp.stack(new_k, axis=1)
    new_attn_v = jnp.stack(new_v, axis=1)
    new_state_gla = jnp.stack(new_gla, axis=1)
    new_state_lru = jnp.stack(new_lru, axis=1)
    return (y_prompt, y_sample, new_attn_k, new_attn_v, new_state_gla, new_state_lru)
```

```python
import functools

import jax
import jax.numpy as jnp
from jax import lax
from jax.experimental import pallas as pl
from jax.experimental.pallas import tpu as pltpu

F32 = jnp.float32
BF16 = jnp.bfloat16

EPS = 1e-6
HEAD_DIM = 128
ATT_HEADS = 16
ATT_KV_HEADS = 4
ATT_GROUPS = ATT_HEADS // ATT_KV_HEADS
WINDOW = 128
Q_BLOCK = 128
GRID_W = 64
ROPE_THETA = 10000.0
ROPE_FREQS = HEAD_DIM // 4
GLA_HEADS = 8
GLA_DK = 128
GLA_LOWRANK = 16
GLA_TAU = 16.0
GLA_CHUNK = 64
LRU_BLOCK = 128
LRU_CONV = 4
CONV_LEFT = 2
LRU_C = 8.0

LANE = 128
SUBLANE = 8
VMEM_BYTES_V7X = 64 * 2**20
MOD_ROWS = 16

NT_DIMS = (((1,), (1,)), ((), ()))
TN_DIMS = (((0,), (0,)), ((), ()))


def _params(sem, est_bytes):
    limit = int(min(VMEM_BYTES_V7X - 4 * 2**20, max(est_bytes + 8 * 2**20, 32 * 2**20)))
    return pltpu.CompilerParams(dimension_semantics=sem, vmem_limit_bytes=limit)


def _softplus(x):
    return jnp.maximum(x, 0.0) + jnp.log1p(jnp.exp(-jnp.abs(x)))


def _log_sigmoid(x):
    return -_softplus(-x)


def _ada_kernel(c_ref, w_ref, b_ref, o_ref):
    cv = c_ref[...]
    s = (cv * jax.nn.sigmoid(cv)).astype(BF16)
    o_ref[...] = jnp.dot(s, w_ref[...].astype(BF16), preferred_element_type=F32) + b_ref[...]


def _ada(cvecs, w_ada, b_ada):
    nl, d, n6 = w_ada.shape
    tn = 512
    return pl.pallas_call(
        _ada_kernel,
        grid=(nl, n6 // tn),
        in_specs=[pl.BlockSpec((MOD_ROWS, d), lambda l, j: (0, 0)),
                  pl.BlockSpec((None, d, tn), lambda l, j: (l, 0, j)),
                  pl.BlockSpec((None, 1, tn), lambda l, j: (l, 0, j))],
        out_specs=pl.BlockSpec((None, MOD_ROWS, tn), lambda l, j: (l, 0, j)),
        out_shape=jax.ShapeDtypeStruct((nl, MOD_ROWS, n6), F32),
        compiler_params=_params(("arbitrary", "arbitrary"), 2 * d * tn * 4 + d * tn * 2),
    )(cvecs, w_ada, b_ada.reshape(nl, 1, n6))


def _norm_mod_kernel(x_ref, g_ref, sc_ref, sh_ref, o_ref):
    x = x_ref[...]
    y = x * lax.rsqrt(jnp.mean(x * x, axis=-1, keepdims=True) + EPS) * g_ref[...]
    o_ref[...] = (y * (1.0 + sc_ref[...]) + sh_ref[...]).astype(o_ref.dtype)


def _norm_kernel(x_ref, g_ref, o_ref):
    x = x_ref[...]
    y = x * lax.rsqrt(jnp.mean(x * x, axis=-1, keepdims=True) + EPS) * g_ref[...]
    o_ref[...] = y.astype(o_ref.dtype)


def _norm_mod(x, g, modr, layer, row_of, sc_idx, sh_idx, tr=256):
    m, d = x.shape
    mspec = lambda k: pl.BlockSpec((None, None, None, 1, d), lambda i: (layer, row_of(i * tr), k, 0, 0))
    return pl.pallas_call(
        _norm_mod_kernel,
        grid=(m // tr,),
        in_specs=[pl.BlockSpec((tr, d), lambda i: (i, 0)),
                  pl.BlockSpec((1, d), lambda i: (0, 0)),
                  mspec(sc_idx), mspec(sh_idx)],
        out_specs=pl.BlockSpec((tr, d), lambda i: (i, 0)),
        out_shape=jax.ShapeDtypeStruct((m, d), BF16),
        compiler_params=_params(("arbitrary",), 2 * tr * d * 6 + 2 * tr * d * 4),
    )(x, g.reshape(1, d), modr, modr)


def _final_norm(x, g, tr=256):
    m, d = x.shape
    return pl.pallas_call(
        _norm_kernel,
        grid=(m // tr,),
        in_specs=[pl.BlockSpec((tr, d), lambda i: (i, 0)),
                  pl.BlockSpec((1, d), lambda i: (0, 0))],
        out_specs=pl.BlockSpec((tr, d), lambda i: (i, 0)),
        out_shape=jax.ShapeDtypeStruct((m, d), F32),
        compiler_params=_params(("arbitrary",), 2 * tr * d * 8 + 2 * tr * d * 4),
    )(x, g.reshape(1, d))


def _mm_kernel(x_ref, w_ref, o_ref):
    o_ref[...] = jnp.dot(x_ref[...], w_ref[...], preferred_element_type=F32)


def _in_proj(h, w, tm=1024, tn=512):
    m, k = h.shape
    n = w.shape[1]
    return pl.pallas_call(
        _mm_kernel,
        grid=(m // tm, n // tn),
        in_specs=[pl.BlockSpec((tm, k), lambda i, j: (i, 0)),
                  pl.BlockSpec((k, tn), lambda i, j: (0, j))],
        out_specs=pl.BlockSpec((tm, tn), lambda i, j: (i, j)),
        out_shape=jax.ShapeDtypeStruct((m, n), F32),
        compiler_params=_params(("arbitrary", "arbitrary"), 2 * (tm * k * 2 + k * tn * 2 + tm * tn * 4)),
    )(h, w)


def _out_proj_kernel(a_ref, b_ref, c_ref, w_ref, x_ref, g_ref, o_ref):
    ka, kb = a_ref.shape[1], b_ref.shape[1]
    acc = jnp.dot(a_ref[...], w_ref[0:ka, :], preferred_element_type=F32)
    acc += jnp.dot(b_ref[...], w_ref[ka:ka + kb, :], preferred_element_type=F32)
    acc += jnp.dot(c_ref[...], w_ref[ka + kb:, :], preferred_element_type=F32)
    o_ref[...] = x_ref[...] + g_ref[...] * acc


def _out_proj(attn, gla, lru, w, x, modr, layer, row_of, gate_idx, tm=1024, tn=512):
    m, d = x.shape
    k = w.shape[0]
    xs = lambda a: pl.BlockSpec((tm, a.shape[1]), lambda i, j: (i, 0))
    return pl.pallas_call(
        _out_proj_kernel,
        grid=(m // tm, d // tn),
        in_specs=[xs(attn), xs(gla), xs(lru),
                  pl.BlockSpec((k, tn), lambda i, j: (0, j)),
                  pl.BlockSpec((tm, tn), lambda i, j: (i, j)),
                  pl.BlockSpec((None, None, None, 1, tn),
                               lambda i, j: (layer, row_of(i * tm), gate_idx, 0, j))],
        out_specs=pl.BlockSpec((tm, tn), lambda i, j: (i, j)),
        out_shape=jax.ShapeDtypeStruct((m, d), F32),
        compiler_params=_params(("arbitrary", "arbitrary"),
                                2 * (tm * k * 2 + k * tn * 2 + 2 * tm * tn * 4) + tm * tn * 4),
    )(attn, gla, lru, w, x, modr)


def _gate_up_kernel(h_ref, wg_ref, wu_ref, o_ref):
    h = h_ref[...]
    a = jnp.dot(h, wg_ref[...], preferred_element_type=F32)
    u = jnp.dot(h, wu_ref[...], preferred_element_type=F32)
    o_ref[...] = (a * jax.nn.sigmoid(a) * u).astype(o_ref.dtype)


def _gate_up(h, wg, wu, tm=1024, tn=512):
    m, k = h.shape
    n = wg.shape[1]
    return pl.pallas_call(
        _gate_up_kernel,
        grid=(m // tm, n // tn),
        in_specs=[pl.BlockSpec((tm, k), lambda i, j: (i, 0)),
                  pl.BlockSpec((k, tn), lambda i, j: (0, j)),
                  pl.BlockSpec((k, tn), lambda i, j: (0, j))],
        out_specs=pl.BlockSpec((tm, tn), lambda i, j: (i, j)),
        out_shape=jax.ShapeDtypeStruct((m, n), BF16),
        compiler_params=_params(("arbitrary", "arbitrary"),
                                2 * (tm * k * 2 + 2 * k * tn * 2 + tm * tn * 2) + 3 * tm * tn * 4),
    )(h, wg, wu)


def _down_kernel(a_ref, w_ref, x_ref, g_ref, o_ref):
    acc = jnp.dot(a_ref[...], w_ref[...], preferred_element_type=F32)
    o_ref[...] = x_ref[...] + g_ref[...] * acc


def _down_proj(act, w, x, modr, layer, row_of, gate_idx, tm=512, tn=256):
    m, d = x.shape
    k = w.shape[0]
    return pl.pallas_call(
        _down_kernel,
        grid=(m // tm, d // tn),
        in_specs=[pl.BlockSpec((tm, k), lambda i, j: (i, 0)),
                  pl.BlockSpec((k, tn), lambda i, j: (0, j)),
                  pl.BlockSpec((tm, tn), lambda i, j: (i, j)),
                  pl.BlockSpec((None, None, None, 1, tn),
                               lambda i, j: (layer, row_of(i * tm), gate_idx, 0, j))],
        out_specs=pl.BlockSpec((tm, tn), lambda i, j: (i, j)),
        out_shape=jax.ShapeDtypeStruct((m, d), F32),
        compiler_params=_params(("arbitrary", "arbitrary"),
                                2 * (tm * k * 2 + k * tn * 2 + 2 * tm * tn * 4) + tm * tn * 4),
    )(act, w, x, modr)


def _sink_column(sink_ref, kh, rows):
    return jnp.concatenate(
        [jnp.full((rows, 1), sink_ref[kh * ATT_GROUPS + g], F32) for g in range(ATT_GROUPS)], axis=0)


def _attn_ctx_kernel(sink_ref, q_ref, k_ref, v_ref, o_ref):
    s_len = q_ref.shape[0]
    scale = HEAD_DIM ** -0.5
    for kh in range(ATT_KV_HEADS):
        kk = k_ref[:, kh * HEAD_DIM:(kh + 1) * HEAD_DIM].astype(BF16)
        vv = v_ref[:, kh * HEAD_DIM:(kh + 1) * HEAD_DIM].astype(BF16)
        h0 = kh * ATT_GROUPS
        q = jnp.concatenate([q_ref[:, (h0 + g) * HEAD_DIM:(h0 + g + 1) * HEAD_DIM]
                             for g in range(ATT_GROUPS)], axis=0).astype(BF16)
        s = lax.dot_general(q, kk, NT_DIMS, preferred_element_type=F32) * scale
        sk = _sink_column(sink_ref, kh, s_len)
        m = jnp.maximum(jnp.max(s, axis=-1, keepdims=True), sk)
        p = jnp.exp(s - m)
        denom = jnp.sum(p, axis=-1, keepdims=True) + jnp.exp(sk - m)
        pn = (p * (1.0 / denom)).astype(BF16)
        o = jnp.dot(pn, vv, preferred_element_type=F32)
        for g in range(ATT_GROUPS):
            o_ref[:, (h0 + g) * HEAD_DIM:(h0 + g + 1) * HEAD_DIM] = (
                o[g * s_len:(g + 1) * s_len].astype(o_ref.dtype))


def _attn_ctx(proj, sink, nb, s_len):
    qw = ATT_HEADS * HEAD_DIM
    kw = ATT_KV_HEADS * HEAD_DIM
    return pl.pallas_call(
        _attn_ctx_kernel,
        grid=(nb,),
        in_specs=[pl.BlockSpec(memory_space=pltpu.SMEM),
                  pl.BlockSpec((s_len, qw), lambda b: (b, 0)),
                  pl.BlockSpec((s_len, kw), lambda b: (b, qw // kw)),
                  pl.BlockSpec((s_len, kw), lambda b: (b, qw // kw + 1))],
        out_specs=pl.BlockSpec((s_len, qw), lambda b: (b, 0)),
        out_shape=jax.ShapeDtypeStruct((nb * s_len, qw), BF16),
        compiler_params=_params(("arbitrary",), 2 * s_len * (qw + 2 * kw) * 4 + 2 * s_len * qw * 2
                                + 8 * ATT_GROUPS * s_len * s_len * 4),
    )(sink, proj, proj, proj)


def _attn_lat_kernel(sink_ref, q_ref, k_ref, v_ref, kc_ref, vc_ref, cos_ref, sa_ref, sb_ref,
                     o_ref, kr_s, vb_s):
    n = q_ref.shape[0]
    kh = pl.program_id(1)
    scale = HEAD_DIM ** -0.5
    span = Q_BLOCK + 2 * WINDOW

    def rope(x, r0, rows):
        c = cos_ref[pl.ds(r0, rows), :]
        a = sa_ref[pl.ds(r0, rows), :]
        b = sb_ref[pl.ds(r0, rows), :]
        return (x * c + pltpu.roll(x, HEAD_DIM - ROPE_FREQS, 1) * a
                + pltpu.roll(x, ROPE_FREQS, 1) * b)

    kr_s[...] = rope(k_ref[...], 0, n).astype(BF16)
    vb_s[...] = v_ref[...].astype(BF16)
    kc = kc_ref[...].astype(BF16)
    vc = vc_ref[...].astype(BF16)
    sk = jnp.concatenate(
        [jnp.full((Q_BLOCK, 1), sink_ref[kh * ATT_GROUPS + g], F32) for g in range(ATT_GROUPS)], axis=0)
    qi = lax.broadcasted_iota(jnp.int32, (Q_BLOCK, span), 0)
    ki = lax.broadcasted_iota(jnp.int32, (Q_BLOCK, span), 1)

    def block(i, carry):
        qs = pl.multiple_of(i * Q_BLOCK, Q_BLOCK)
        ks = pl.multiple_of(jnp.clip(qs - WINDOW, 0, n - span), Q_BLOCK)
        q = q_ref[pl.ds(qs, Q_BLOCK), :]
        qr = jnp.concatenate([rope(q[:, g * HEAD_DIM:(g + 1) * HEAD_DIM], qs, Q_BLOCK)
                              for g in range(ATT_GROUPS)], axis=0).astype(BF16)
        kw = kr_s[pl.ds(ks, span), :]
        vw = vb_s[pl.ds(ks, span), :]
        bias = jnp.where(jnp.abs((qs + qi) - (ks + ki)) <= WINDOW, 0.0, -jnp.inf).astype(F32)
        s1 = (lax.dot_general(qr, kw, NT_DIMS, preferred_element_type=F32) * scale
              + jnp.concatenate([bias] * ATT_GROUPS, axis=0))
        s2 = lax.dot_general(qr, kc, NT_DIMS, preferred_element_type=F32) * scale
        m = jnp.maximum(jnp.maximum(jnp.max(s1, axis=-1, keepdims=True),
                                    jnp.max(s2, axis=-1, keepdims=True)), sk)
        p1 = jnp.exp(s1 - m)
        p2 = jnp.exp(s2 - m)
        denom = (jnp.sum(p1, axis=-1, keepdims=True) + jnp.sum(p2, axis=-1, keepdims=True)
                 + jnp.exp(sk - m))
        r = 1.0 / denom
        o = (jnp.dot((p1 * r).astype(BF16), vw, preferred_element_type=F32)
             + jnp.dot((p2 * r).astype(BF16), vc, preferred_element_type=F32))
        for g in range(ATT_GROUPS):
            o_ref[pl.ds(qs, Q_BLOCK), g * HEAD_DIM:(g + 1) * HEAD_DIM] = (
                o[g * Q_BLOCK:(g + 1) * Q_BLOCK].astype(o_ref.dtype))
        return carry

    lax.fori_loop(0, n // Q_BLOCK, block, 0)


def _attn_lat(proj, sink, cache_k, cache_v, layer, tables, nb, n):
    qw = ATT_HEADS * HEAD_DIM
    gw = ATT_GROUPS * HEAD_DIM
    past = cache_k.shape[2]
    kcol = qw // HEAD_DIM
    vcol = kcol + ATT_KV_HEADS
    cspec = pl.BlockSpec((None, None, past, HEAD_DIM), lambda b, h: (b, layer, 0, h))
    tspec = pl.BlockSpec((n, HEAD_DIM), lambda b, h: (0, 0))
    return pl.pallas_call(
        _attn_lat_kernel,
        grid=(nb, ATT_KV_HEADS),
        in_specs=[pl.BlockSpec(memory_space=pltpu.SMEM),
                  pl.BlockSpec((n, gw), lambda b, h: (b, h)),
                  pl.BlockSpec((n, HEAD_DIM), lambda b, h: (b, kcol + h)),
                  pl.BlockSpec((n, HEAD_DIM), lambda b, h: (b, vcol + h)),
                  cspec, cspec, tspec, tspec, tspec],
        out_specs=pl.BlockSpec((n, gw), lambda b, h: (b, h)),
        out_shape=jax.ShapeDtypeStruct((nb * n, qw), BF16),
        scratch_shapes=[pltpu.VMEM((n, HEAD_DIM), BF16), pltpu.VMEM((n, HEAD_DIM), BF16)],
        compiler_params=_params(("arbitrary", "arbitrary"),
                                2 * n * (gw + 2 * HEAD_DIM) * 4 + 2 * n * gw * 2 + 6 * n * HEAD_DIM * 4
                                + 4 * n * HEAD_DIM),
    )(sink, proj, proj, proj, cache_k, cache_v, *tables)


def _rope_tables(n):
    pos = jnp.arange(n)
    row = (pos // GRID_W).astype(F32)
    col = (pos % GRID_W).astype(F32)
    inv = ROPE_THETA ** (-jnp.arange(ROPE_FREQS, dtype=F32) / ROPE_FREQS)
    ang_r = row[:, None] * inv[None, :]
    ang_c = col[:, None] * inv[None, :]
    cr, sr, cc, sc = jnp.cos(ang_r), jnp.sin(ang_r), jnp.cos(ang_c), jnp.sin(ang_c)
    z = jnp.zeros_like(cr)
    cos = jnp.concatenate([cr, cr, cc, cc], axis=1)
    sa = jnp.concatenate([-sr, z, -sc, z], axis=1)
    sb = jnp.concatenate([z, sr, z, sc], axis=1)
    return cos, sa, sb


def _gla_kernel(*refs, has_init):
    if has_init:
        (q_ref, k_ref, v_ref, g_ref, dec_ref, wf_ref, wb_ref, bd_ref, ng_ref, s0_ref,
         o_ref, of_s, ob_s, st_s) = refs
        sfin_ref = None
    else:
        (q_ref, k_ref, v_ref, g_ref, dec_ref, wf_ref, wb_ref, bd_ref, ng_ref,
         o_ref, sfin_ref, of_s, ob_s, st_s) = refs
    n = q_ref.shape[0]
    c = GLA_CHUNK
    nc = n // c
    qscale = GLA_DK ** -0.5
    ri = lax.broadcasted_iota(jnp.int32, (c, c), 0)
    ci = lax.broadcasted_iota(jnp.int32, (c, c), 1)
    lower = ri >= ci
    upper = ri <= ci
    tril = lower.astype(F32)
    triu = upper.astype(F32)
    wdec = (wf_ref[...], wb_ref[...])

    for d in range(2):
        if has_init:
            st_s[d] = s0_ref[d].T
        else:
            st_s[d] = jnp.zeros((GLA_DK, GLA_DK), F32)

    def one(d, r0, out_s):
        qc = q_ref[pl.ds(r0, c), :] * qscale
        kc = k_ref[pl.ds(r0, c), :]
        vc = v_ref[pl.ds(r0, c), :].astype(BF16)
        z = jnp.dot(dec_ref[pl.ds(r0, c), :].astype(BF16), wdec[d],
                    preferred_element_type=F32) + bd_ref[d]
        la = _log_sigmoid(z) / GLA_TAU
        if d == 0:
            cum = jnp.dot(tril, la, precision=lax.Precision.HIGHEST, preferred_element_type=F32)
            total = cum[c - 1:c, :]
            keep = lower
        else:
            cum = jnp.dot(triu, la, precision=lax.Precision.HIGHEST, preferred_element_type=F32)
            total = cum[0:1, :]
            keep = upper
        q_in = (qc * jnp.exp(cum)).astype(BF16)
        k_in = (kc * jnp.exp(-cum)).astype(BF16)
        att = jnp.where(keep, lax.dot_general(q_in, k_in, NT_DIMS, preferred_element_type=F32), 0.0)
        st = st_s[d]
        o = (jnp.dot(att.astype(BF16), vc, preferred_element_type=F32)
             + lax.dot_general(q_in, st.astype(BF16), NT_DIMS, preferred_element_type=F32))
        out_s[pl.ds(r0, c), :] = o
        k_dec = (kc * jnp.exp(total - cum)).astype(BF16)
        st_s[d] = jnp.exp(total) * st + lax.dot_general(vc, k_dec, TN_DIMS, preferred_element_type=F32)

    def step(j, carry):
        one(0, pl.multiple_of(j * c, c), of_s)
        one(1, pl.multiple_of((nc - 1 - j) * c, c), ob_s)
        return carry

    lax.fori_loop(0, nc, step, 0)

    o = of_s[...] + ob_s[...]
    o = o * lax.rsqrt(jnp.mean(o * o, axis=-1, keepdims=True) + EPS) * ng_ref[...]
    g = g_ref[...]
    o_ref[...] = (o * (g * jax.nn.sigmoid(g))).astype(o_ref.dtype)
    if sfin_ref is not None:
        for d in range(2):
            sfin_ref[d] = st_s[d].T


def _gla(proj, wf, wb, b_dec, norm_g, nb, n, init=None, layer=0):
    hh = GLA_HEADS
    col = lambda base: pl.BlockSpec((n, LANE), lambda b, h: (b, base + h))
    in_specs = [col(24), col(32), col(40), col(48),
                pl.BlockSpec((n, LANE), lambda b, h: (b, 72)),
                pl.BlockSpec((None, LANE, LANE), lambda b, h: (h, 0, 0)),
                pl.BlockSpec((None, LANE, LANE), lambda b, h: (h, 0, 0)),
                pl.BlockSpec((2, None, 1, LANE), lambda b, h: (0, h, 0, 0)),
                pl.BlockSpec((None, 1, LANE), lambda b, h: (h, 0, 0))]
    args = [proj, proj, proj, proj, proj, wf, wb, b_dec, norm_g]
    o_shape = jax.ShapeDtypeStruct((nb * n, hh * LANE), BF16)
    o_spec = pl.BlockSpec((n, LANE), lambda b, h: (b, h))
    if init is not None:
        in_specs.append(pl.BlockSpec((None, None, 2, None, GLA_DK, LANE),
                                     lambda b, h: (b, layer, 0, h, 0, 0)))
        args.append(init)
        out_shape, out_specs = o_shape, o_spec
    else:
        out_shape = (o_shape, jax.ShapeDtypeStruct((nb, 2, hh, GLA_DK, LANE), F32))
        out_specs = (o_spec, pl.BlockSpec((None, 2, None, GLA_DK, LANE), lambda b, h: (b, 0, h, 0, 0)))
    return pl.pallas_call(
        functools.partial(_gla_kernel, has_init=init is not None),
        grid=(nb, hh),
        in_specs=in_specs,
        out_specs=out_specs,
        out_shape=out_shape,
        scratch_shapes=[pltpu.VMEM((n, LANE), F32), pltpu.VMEM((n, LANE), F32),
                        pltpu.VMEM((2, LANE, GLA_DK), F32)],
        compiler_params=_params(("arbitrary", "arbitrary"), 2 * 5 * n * LANE * 4 + 8 * n * LANE * 4),
    )(*args)


def _lru_kernel(*refs, has_init):
    if has_init:
        (x_ref, y_ref, cw_ref, cb_ref, wr_ref, br_ref, wi_ref, bi_ref, lam_ref, h0_ref,
         o_ref, xp_s, a_s, u_s) = refs
        hfin_ref = None
    else:
        (x_ref, y_ref, cw_ref, cb_ref, wr_ref, br_ref, wi_ref, bi_ref, lam_ref,
         o_ref, hfin_ref, xp_s, a_s, u_s) = refs
    n = x_ref.shape[0]
    pad = SUBLANE
    xp_s[0:pad, :] = jnp.zeros((pad, LANE), F32)
    xp_s[pad + n:pad + n + pad, :] = jnp.zeros((pad, LANE), F32)
    xp_s[pad:pad + n, :] = x_ref[...]
    xc = xp_s[pl.ds(pad - CONV_LEFT, n), :] * cw_ref[0:1, :]
    for j in range(1, LRU_CONV):
        xc = xc + xp_s[pl.ds(pad - CONV_LEFT + j, n), :] * cw_ref[j:j + 1, :]
    xc = xc + cb_ref[...]
    xcb = xc.astype(BF16)
    for d in range(2):
        r = jax.nn.sigmoid(jnp.dot(xcb, wr_ref[d], preferred_element_type=F32) + br_ref[d])
        i = jax.nn.sigmoid(jnp.dot(xcb, wi_ref[d], preferred_element_type=F32) + bi_ref[d])
        log_a = -LRU_C * r * _softplus(-lam_ref[d])
        a = jnp.exp(log_a)
        a_s[d] = a
        u_s[d] = jnp.sqrt(-jnp.tanh(log_a) * (a * a + 1.0)) * (i * xc)

    if has_init:
        hf0, hb0 = h0_ref[0], h0_ref[1]
    else:
        hf0 = hb0 = jnp.zeros((1, LANE), F32)

    def step(t, carry):
        hf, hb = carry
        tb = n - 1 - t
        hf = a_s[0, pl.ds(t, 1), :] * hf + u_s[0, pl.ds(t, 1), :]
        hb = a_s[1, pl.ds(tb, 1), :] * hb + u_s[1, pl.ds(tb, 1), :]
        u_s[0, pl.ds(t, 1), :] = hf
        u_s[1, pl.ds(tb, 1), :] = hb
        return hf, hb

    hf, hb = lax.fori_loop(0, n, step, (hf0, hb0), unroll=8)
    o_ref[...] = ((u_s[0] + u_s[1]) * jax.nn.gelu(y_ref[...])).astype(o_ref.dtype)
    if hfin_ref is not None:
        hfin_ref[0] = hf
        hfin_ref[1] = hb


def _lru(proj, conv_w, conv_b, w_r, b_r, w_i, b_i, lam, nb, n, init=None, layer=0):
    kb = conv_w.shape[1] // LRU_BLOCK
    vec = lambda rows: pl.BlockSpec((rows, LANE), lambda b, k: (0, k))
    vec2 = pl.BlockSpec((2, 1, LANE), lambda b, k: (0, 0, k))
    wspec = pl.BlockSpec((2, None, LRU_BLOCK, LRU_BLOCK), lambda b, k: (0, k, 0, 0))
    in_specs = [pl.BlockSpec((n, LANE), lambda b, k: (b, 56 + k)),
                pl.BlockSpec((n, LANE), lambda b, k: (b, 64 + k)),
                vec(LRU_CONV), vec(1), wspec, vec2, wspec, vec2, vec2]
    args = [proj, proj, conv_w, conv_b, w_r, b_r, w_i, b_i, lam]
    o_shape = jax.ShapeDtypeStruct((nb * n, kb * LANE), BF16)
    o_spec = pl.BlockSpec((n, LANE), lambda b, k: (b, k))
    if init is not None:
        in_specs.append(pl.BlockSpec((None, None, 2, 1, LANE), lambda b, k: (b, layer, 0, 0, k)))
        args.append(init)
        out_shape, out_specs = o_shape, o_spec
    else:
        out_shape = (o_shape, jax.ShapeDtypeStruct((nb, 2, 1, kb * LANE), F32))
        out_specs = (o_spec, pl.BlockSpec((None, 2, 1, LANE), lambda b, k: (b, 0, 0, k)))
    return pl.pallas_call(
        functools.partial(_lru_kernel, has_init=init is not None),
        grid=(nb, kb),
        in_specs=in_specs,
        out_specs=out_specs,
        out_shape=out_shape,
        scratch_shapes=[pltpu.VMEM((n + 2 * SUBLANE, LANE), F32),
                        pltpu.VMEM((2, n, LANE), F32), pltpu.VMEM((2, n, LANE), F32)],
        compiler_params=_params(("arbitrary", "arbitrary"), 2 * 2 * n * LANE * 4 + 12 * n * LANE * 4),
    )(*args)


def _pad_cols(w, n):
    return jnp.pad(w, ((0, 0), (0, n - w.shape[1])))


def kernel(x_prompt, x_sample, cache_attn_k, cache_attn_v, state_gla, state_lru, c, c_ctx, w_ada, b_ada, norm1_g, w_in, attn_sink, gla_w_decay, gla_b_decay, gla_norm_g, lru_conv_w, lru_conv_b, lru_w_rgate, lru_b_rgate, lru_w_igate, lru_b_igate, lru_lambda, w_out, norm2_g, w_gu, w_down, final_norm_g):
    nb_c, s_len, d = x_prompt.shape
    nb_l, n_lat, _ = x_sample.shape
    depth = w_in.shape[0]
    d_ff = w_down.shape[1]
    lru_w = lru_conv_w.shape[2]
    qw, kw = ATT_HEADS * HEAD_DIM, ATT_KV_HEADS * HEAD_DIM
    gdec0 = qw + 2 * kw + 4 * GLA_HEADS * GLA_DK
    gdec1 = gdec0 + 2 * GLA_LOWRANK
    in_pad = 19 * 512
    ff_pad = -(-d_ff // 512) * 512

    cvecs = jnp.concatenate([c_ctx[None], c, jnp.zeros((MOD_ROWS - 1 - nb_l, d), F32)], axis=0)
    modr = _ada(cvecs, w_ada, b_ada).reshape(depth, MOD_ROWS, 6, 1, d)
    row_ctx = lambda r0: 0
    row_lat = lambda r0: 1 + r0 // n_lat

    tables = _rope_tables(n_lat)
    cache_k = cache_attn_k.reshape(nb_l, depth, cache_attn_k.shape[2], kw)
    cache_v = cache_attn_v.reshape(nb_l, depth, cache_attn_v.shape[2], kw)
    lru_init = state_lru.reshape(nb_l, depth, 2, 1, lru_w)

    xc = x_prompt.reshape(nb_c * s_len, d)
    xl = x_sample.reshape(nb_l * n_lat, d)
    new_k, new_v, new_gla, new_lru = [], [], [], []
    for l in range(depth):
        w_in_l = _pad_cols(jnp.concatenate(
            [w_in[l][:, :gdec0], w_in[l][:, gdec1:], w_in[l][:, gdec0:gdec1]], axis=1), in_pad).astype(BF16)
        w_out_l = w_out[l].astype(BF16)
        w_g = _pad_cols(w_gu[l][:, :d_ff], ff_pad).astype(BF16)
        w_u = _pad_cols(w_gu[l][:, d_ff:], ff_pad).astype(BF16)
        w_dn = jnp.pad(w_down[l], ((0, ff_pad - d_ff), (0, 0))).astype(BF16)
        wdec = gla_w_decay[l].reshape(2, GLA_LOWRANK, GLA_HEADS, GLA_DK).transpose(0, 2, 1, 3)
        zpad = jnp.zeros((GLA_HEADS, GLA_LOWRANK, GLA_DK), F32)
        zrest = jnp.zeros((GLA_HEADS, LANE - 2 * GLA_LOWRANK, GLA_DK), F32)
        wf = jnp.concatenate([wdec[0], zpad, zrest], axis=1).astype(BF16)
        wb = jnp.concatenate([zpad, wdec[1], zrest], axis=1).astype(BF16)
        b_dec = gla_b_decay[l].reshape(2, GLA_HEADS, 1, GLA_DK)
        ng = gla_norm_g[l].reshape(GLA_HEADS, 1, LANE)
        lru_args = (lru_conv_w[l], lru_conv_b[l].reshape(1, lru_w),
                    lru_w_rgate[l].astype(BF16), lru_b_rgate[l].reshape(2, 1, lru_w),
                    lru_w_igate[l].astype(BF16), lru_b_igate[l].reshape(2, 1, lru_w),
                    lru_lambda[l].reshape(2, 1, lru_w))
        sink = attn_sink[l]

        def ffn(x, mix, row_of):
            x = _out_proj(*mix, w_out_l, x, modr, l, row_of, 2)
            h2 = _norm_mod(x, norm2_g[l], modr, l, row_of, 4, 3)
            act = _gate_up(h2, w_g, w_u)
            return _down_proj(act, w_dn, x, modr, l, row_of, 5)

        h = _norm_mod(xc, norm1_g[l], modr, l, row_ctx, 1, 0)
        proj = _in_proj(h, w_in_l)
        attn = _attn_ctx(proj, sink, nb_c, s_len)
        gla, gla_fin = _gla(proj, wf, wb, b_dec, ng, nb_c, s_len)
        lru, lru_fin = _lru(proj, *lru_args, nb_c, s_len)
        new_k.append(proj[:, qw:qw + kw].reshape(nb_c, s_len, ATT_KV_HEADS, HEAD_DIM))
        new_v.append(proj[:, qw + kw:qw + 2 * kw].reshape(nb_c, s_len, ATT_KV_HEADS, HEAD_DIM))
        new_gla.append(gla_fin)
        new_lru.append(lru_fin.reshape(nb_c, 2, lru_w))
        xc = ffn(xc, (attn, gla, lru), row_ctx)

        h = _norm_mod(xl, norm1_g[l], modr, l, row_lat, 1, 0)
        proj = _in_proj(h, w_in_l)
        attn = _attn_lat(proj, sink, cache_k, cache_v, l, tables, nb_l, n_lat)
        gla = _gla(proj, wf, wb, b_dec, ng, nb_l, n_lat, init=state_gla, layer=l)
        lru = _lru(proj, *lru_args, nb_l, n_lat, init=lru_init, layer=l)
        xl = ffn(xl, (attn, gla, lru), row_lat)

    y_prompt = _final_norm(xc, final_norm_g).reshape(nb_c, s_len, d)
    y_sample = _final_norm(xl, final_norm_g).reshape(nb_l, n_lat, d)
    return (y_prompt, y_sample, jnp.stack(new_k, axis=1), jnp.stack(new_v, axis=1),
            jnp.stack(new_gla, axis=1), jnp.stack(new_lru, axis=1))
```

```python
import functools

import jax
import jax.numpy as jnp
from jax import lax
from jax.experimental import pallas as pl
from jax.experimental.pallas import tpu as pltpu

F32 = jnp.float32
BF16 = jnp.bfloat16

EPS = 1e-6
HEAD_DIM = 128
ATT_HEADS = 16
ATT_KV_HEADS = 4
ATT_GROUPS = ATT_HEADS // ATT_KV_HEADS
WINDOW = 128
Q_BLOCK = 128
GRID_W = 64
ROPE_THETA = 10000.0
ROPE_FREQS = HEAD_DIM // 4
GLA_HEADS = 8
GLA_DK = 128
GLA_LOWRANK = 16
GLA_TAU = 16.0
GLA_CHUNK = 64
GLA_BULK_ROWS = 256
LRU_BLOCK = 128
LRU_CONV = 4
CONV_LEFT = 2
LRU_C = 8.0
SCAN_TOP_ROWS = 8
SCAN_SLAB_ROWS = 64

LANE = 128
SUBLANE = 8
VMEM_BYTES_V7X = 64 * 2**20
MOD_ROWS = 16

NT_DIMS = (((1,), (1,)), ((), ()))
TN_DIMS = (((0,), (0,)), ((), ()))


def _params(sem, est_bytes):
    limit = int(min(VMEM_BYTES_V7X - 4 * 2**20, max(est_bytes + 8 * 2**20, 32 * 2**20)))
    return pltpu.CompilerParams(dimension_semantics=sem, vmem_limit_bytes=limit)


def _softplus(x):
    return jnp.maximum(x, 0.0) + jnp.log1p(jnp.exp(-jnp.abs(x)))


def _log_sigmoid(x):
    return -_softplus(-x)


def _sigmoid(x):
    return 0.5 * jnp.tanh(0.5 * x) + 0.5


def _ada_kernel(c_ref, w_ref, b_ref, o_ref):
    cv = c_ref[...]
    s = (cv * jax.nn.sigmoid(cv)).astype(BF16)
    o_ref[...] = jnp.dot(s, w_ref[...].astype(BF16), preferred_element_type=F32) + b_ref[...]


def _ada(cvecs, w_ada, b_ada):
    nl, d, n6 = w_ada.shape
    tn = 512
    return pl.pallas_call(
        _ada_kernel,
        grid=(nl, n6 // tn),
        in_specs=[pl.BlockSpec((MOD_ROWS, d), lambda l, j: (0, 0)),
                  pl.BlockSpec((None, d, tn), lambda l, j: (l, 0, j)),
                  pl.BlockSpec((None, 1, tn), lambda l, j: (l, 0, j))],
        out_specs=pl.BlockSpec((None, MOD_ROWS, tn), lambda l, j: (l, 0, j)),
        out_shape=jax.ShapeDtypeStruct((nl, MOD_ROWS, n6), F32),
        compiler_params=_params(("arbitrary", "arbitrary"), 2 * d * tn * 4 + d * tn * 2),
    )(cvecs, w_ada, b_ada.reshape(nl, 1, n6))


def _norm_mod_kernel(x_ref, g_ref, sc_ref, sh_ref, o_ref):
    x = x_ref[...]
    y = x * lax.rsqrt(jnp.mean(x * x, axis=-1, keepdims=True) + EPS) * g_ref[...]
    o_ref[...] = (y * (1.0 + sc_ref[...]) + sh_ref[...]).astype(o_ref.dtype)


def _norm_kernel(x_ref, g_ref, o_ref):
    x = x_ref[...]
    y = x * lax.rsqrt(jnp.mean(x * x, axis=-1, keepdims=True) + EPS) * g_ref[...]
    o_ref[...] = y.astype(o_ref.dtype)


def _norm_mod(x, g, modr, layer, row_of, sc_idx, sh_idx, tr=256):
    m, d = x.shape
    mspec = lambda k: pl.BlockSpec((None, None, None, 1, d), lambda i: (layer, row_of(i * tr), k, 0, 0))
    return pl.pallas_call(
        _norm_mod_kernel,
        grid=(m // tr,),
        in_specs=[pl.BlockSpec((tr, d), lambda i: (i, 0)),
                  pl.BlockSpec((1, d), lambda i: (0, 0)),
                  mspec(sc_idx), mspec(sh_idx)],
        out_specs=pl.BlockSpec((tr, d), lambda i: (i, 0)),
        out_shape=jax.ShapeDtypeStruct((m, d), BF16),
        compiler_params=_params(("arbitrary",), 2 * tr * d * 6 + 2 * tr * d * 4),
    )(x, g.reshape(1, d), modr, modr)


def _final_norm(x, g, tr=256):
    m, d = x.shape
    return pl.pallas_call(
        _norm_kernel,
        grid=(m // tr,),
        in_specs=[pl.BlockSpec((tr, d), lambda i: (i, 0)),
                  pl.BlockSpec((1, d), lambda i: (0, 0))],
        out_specs=pl.BlockSpec((tr, d), lambda i: (i, 0)),
        out_shape=jax.ShapeDtypeStruct((m, d), F32),
        compiler_params=_params(("arbitrary",), 2 * tr * d * 8 + 2 * tr * d * 4),
    )(x, g.reshape(1, d))


def _mm_kernel(x_ref, w_ref, o_ref):
    o_ref[...] = jnp.dot(x_ref[...], w_ref[...], preferred_element_type=F32)


def _in_proj(h, w, tm=1024, tn=512):
    m, k = h.shape
    n = w.shape[1]
    return pl.pallas_call(
        _mm_kernel,
        grid=(m // tm, n // tn),
        in_specs=[pl.BlockSpec((tm, k), lambda i, j: (i, 0)),
                  pl.BlockSpec((k, tn), lambda i, j: (0, j))],
        out_specs=pl.BlockSpec((tm, tn), lambda i, j: (i, j)),
        out_shape=jax.ShapeDtypeStruct((m, n), F32),
        compiler_params=_params(("arbitrary", "arbitrary"), 2 * (tm * k * 2 + k * tn * 2 + tm * tn * 4)),
    )(h, w)


def _out_proj_kernel(a_ref, b_ref, c_ref, w_ref, x_ref, g_ref, o_ref):
    ka, kb = a_ref.shape[1], b_ref.shape[1]
    acc = jnp.dot(a_ref[...], w_ref[0:ka, :], preferred_element_type=F32)
    acc += jnp.dot(b_ref[...], w_ref[ka:ka + kb, :], preferred_element_type=F32)
    acc += jnp.dot(c_ref[...], w_ref[ka + kb:, :], preferred_element_type=F32)
    o_ref[...] = x_ref[...] + g_ref[...] * acc


def _out_proj(attn, gla, lru, w, x, modr, layer, row_of, gate_idx, tm=1024, tn=512):
    m, d = x.shape
    k = w.shape[0]
    xs = lambda a: pl.BlockSpec((tm, a.shape[1]), lambda i, j: (i, 0))
    return pl.pallas_call(
        _out_proj_kernel,
        grid=(m // tm, d // tn),
        in_specs=[xs(attn), xs(gla), xs(lru),
                  pl.BlockSpec((k, tn), lambda i, j: (0, j)),
                  pl.BlockSpec((tm, tn), lambda i, j: (i, j)),
                  pl.BlockSpec((None, None, None, 1, tn),
                               lambda i, j: (layer, row_of(i * tm), gate_idx, 0, j))],
        out_specs=pl.BlockSpec((tm, tn), lambda i, j: (i, j)),
        out_shape=jax.ShapeDtypeStruct((m, d), F32),
        compiler_params=_params(("arbitrary", "arbitrary"),
                                2 * (tm * k * 2 + k * tn * 2 + 2 * tm * tn * 4) + tm * tn * 4),
    )(attn, gla, lru, w, x, modr)


def _gate_up_kernel(h_ref, wg_ref, wu_ref, o_ref):
    h = h_ref[...]
    a = jnp.dot(h, wg_ref[...], preferred_element_type=F32)
    u = jnp.dot(h, wu_ref[...], preferred_element_type=F32)
    o_ref[...] = (a * _sigmoid(a) * u).astype(o_ref.dtype)


def _gate_up(h, wg, wu, tm=1024, tn=512):
    m, k = h.shape
    n = wg.shape[1]
    return pl.pallas_call(
        _gate_up_kernel,
        grid=(m // tm, n // tn),
        in_specs=[pl.BlockSpec((tm, k), lambda i, j: (i, 0)),
                  pl.BlockSpec((k, tn), lambda i, j: (0, j)),
                  pl.BlockSpec((k, tn), lambda i, j: (0, j))],
        out_specs=pl.BlockSpec((tm, tn), lambda i, j: (i, j)),
        out_shape=jax.ShapeDtypeStruct((m, n), BF16),
        compiler_params=_params(("arbitrary", "arbitrary"),
                                2 * (tm * k * 2 + 2 * k * tn * 2 + tm * tn * 2) + 3 * tm * tn * 4),
    )(h, wg, wu)


def _down_kernel(a_ref, w_ref, x_ref, g_ref, o_ref):
    acc = jnp.dot(a_ref[...], w_ref[...], preferred_element_type=F32)
    o_ref[...] = x_ref[...] + g_ref[...] * acc


def _down_proj(act, w, x, modr, layer, row_of, gate_idx, tm=512, tn=256):
    m, d = x.shape
    k = w.shape[0]
    return pl.pallas_call(
        _down_kernel,
        grid=(m // tm, d // tn),
        in_specs=[pl.BlockSpec((tm, k), lambda i, j: (i, 0)),
                  pl.BlockSpec((k, tn), lambda i, j: (0, j)),
                  pl.BlockSpec((tm, tn), lambda i, j: (i, j)),
                  pl.BlockSpec((None, None, None, 1, tn),
                               lambda i, j: (layer, row_of(i * tm), gate_idx, 0, j))],
        out_specs=pl.BlockSpec((tm, tn), lambda i, j: (i, j)),
        out_shape=jax.ShapeDtypeStruct((m, d), F32),
        compiler_params=_params(("arbitrary", "arbitrary"),
                                2 * (tm * k * 2 + k * tn * 2 + 2 * tm * tn * 4) + tm * tn * 4),
    )(act, w, x, modr)


def _sink_column(sink_ref, kh, rows):
    return jnp.concatenate(
        [jnp.full((rows, 1), sink_ref[kh * ATT_GROUPS + g], F32) for g in range(ATT_GROUPS)], axis=0)


def _attn_ctx_kernel(sink_ref, q_ref, k_ref, v_ref, o_ref):
    s_len = q_ref.shape[0]
    scale = HEAD_DIM ** -0.5
    for kh in range(ATT_KV_HEADS):
        kk = k_ref[:, kh * HEAD_DIM:(kh + 1) * HEAD_DIM].astype(BF16)
        vv = v_ref[:, kh * HEAD_DIM:(kh + 1) * HEAD_DIM].astype(BF16)
        h0 = kh * ATT_GROUPS
        q = jnp.concatenate([q_ref[:, (h0 + g) * HEAD_DIM:(h0 + g + 1) * HEAD_DIM]
                             for g in range(ATT_GROUPS)], axis=0).astype(BF16)
        s = lax.dot_general(q, kk, NT_DIMS, preferred_element_type=F32) * scale
        sk = _sink_column(sink_ref, kh, s_len)
        m = jnp.maximum(jnp.max(s, axis=-1, keepdims=True), sk)
        p = jnp.exp(s - m)
        denom = jnp.sum(p, axis=-1, keepdims=True) + jnp.exp(sk - m)
        pn = (p * (1.0 / denom)).astype(BF16)
        o = jnp.dot(pn, vv, preferred_element_type=F32)
        for g in range(ATT_GROUPS):
            o_ref[:, (h0 + g) * HEAD_DIM:(h0 + g + 1) * HEAD_DIM] = (
                o[g * s_len:(g + 1) * s_len].astype(o_ref.dtype))


def _attn_ctx(proj, sink, nb, s_len):
    qw = ATT_HEADS * HEAD_DIM
    kw = ATT_KV_HEADS * HEAD_DIM
    return pl.pallas_call(
        _attn_ctx_kernel,
        grid=(nb,),
        in_specs=[pl.BlockSpec(memory_space=pltpu.SMEM),
                  pl.BlockSpec((s_len, qw), lambda b: (b, 0)),
                  pl.BlockSpec((s_len, kw), lambda b: (b, qw // kw)),
                  pl.BlockSpec((s_len, kw), lambda b: (b, qw // kw + 1))],
        out_specs=pl.BlockSpec((s_len, qw), lambda b: (b, 0)),
        out_shape=jax.ShapeDtypeStruct((nb * s_len, qw), BF16),
        compiler_params=_params(("arbitrary",), 2 * s_len * (qw + 2 * kw) * 4 + 2 * s_len * qw * 2
                                + 8 * ATT_GROUPS * s_len * s_len * 4),
    )(sink, proj, proj, proj)


def _attn_lat_kernel(sink_ref, q_ref, k_ref, v_ref, kc_ref, vc_ref, cos_ref, sa_ref, sb_ref,
                     o_ref, kt_s, vb_s, qr_s):
    n = q_ref.shape[0]
    nqb = n // Q_BLOCK
    kh = pl.program_id(1)
    scale = HEAD_DIM ** -0.5
    wblocks = (Q_BLOCK + 2 * WINDOW) // Q_BLOCK

    def rope(x, rows):
        return (x * cos_ref[rows, :] + pltpu.roll(x, HEAD_DIM - ROPE_FREQS, 1) * sa_ref[rows, :]
                + pltpu.roll(x, ROPE_FREQS, 1) * sb_ref[rows, :])

    def prep(j, carry):
        rows = pl.ds(pl.multiple_of(j * Q_BLOCK, Q_BLOCK), Q_BLOCK)
        kt_s[j] = rope(k_ref[rows, :], rows).T.astype(BF16)
        for g in range(ATT_GROUPS):
            lanes = slice(g * HEAD_DIM, (g + 1) * HEAD_DIM)
            qr_s[rows, lanes] = rope(q_ref[rows, lanes], rows).astype(BF16)
        return carry

    lax.fori_loop(0, nqb, prep, 0)
    vb_s[...] = v_ref[...].astype(BF16)
    kct = kc_ref[...].T.astype(BF16)
    vc = vc_ref[...].astype(BF16)
    sk = jnp.concatenate(
        [jnp.full((Q_BLOCK, 1), sink_ref[kh * ATT_GROUPS + g], F32) for g in range(ATT_GROUPS)], axis=0)
    qi = lax.broadcasted_iota(jnp.int32, (Q_BLOCK, Q_BLOCK), 0)
    ki = lax.broadcasted_iota(jnp.int32, (Q_BLOCK, Q_BLOCK), 1)
    rel = qi - ki

    def block(i, carry):
        qs = pl.multiple_of(i * Q_BLOCK, Q_BLOCK)
        kb = jnp.clip(i - WINDOW // Q_BLOCK, 0, nqb - wblocks)
        qr = jnp.concatenate([qr_s[pl.ds(qs, Q_BLOCK), g * HEAD_DIM:(g + 1) * HEAD_DIM]
                              for g in range(ATT_GROUPS)], axis=0)
        s_parts = []
        for j in range(wblocks):
            off = (i - kb - j) * Q_BLOCK
            bias = jnp.where(jnp.abs(rel + off) <= WINDOW, 0.0, -jnp.inf).astype(F32)
            s_parts.append(jnp.dot(qr, kt_s[kb + j], preferred_element_type=F32) * scale
                           + jnp.concatenate([bias] * ATT_GROUPS, axis=0))
        s2 = jnp.dot(qr, kct, preferred_element_type=F32) * scale
        s_parts += [s2[:, j * LANE:(j + 1) * LANE] for j in range(s2.shape[1] // LANE)]
        m = functools.reduce(jnp.maximum, s_parts)
        m = jnp.maximum(jnp.max(m, axis=-1, keepdims=True), sk)
        p_parts = [jnp.exp(s - m) for s in s_parts]
        denom = jnp.sum(functools.reduce(jnp.add, p_parts), axis=-1, keepdims=True) + jnp.exp(sk - m)
        r = 1.0 / denom
        o = None
        for j, p in enumerate(p_parts):
            if j < wblocks:
                vw = vb_s[pl.ds(pl.multiple_of((kb + j) * Q_BLOCK, Q_BLOCK), Q_BLOCK), :]
            else:
                vw = vc[(j - wblocks) * LANE:(j - wblocks + 1) * LANE, :]
            ov = jnp.dot((p * r).astype(BF16), vw, preferred_element_type=F32)
            o = ov if o is None else o + ov
        for g in range(ATT_GROUPS):
            o_ref[pl.ds(qs, Q_BLOCK), g * HEAD_DIM:(g + 1) * HEAD_DIM] = (
                o[g * Q_BLOCK:(g + 1) * Q_BLOCK].astype(o_ref.dtype))
        return carry

    lax.fori_loop(0, nqb, block, 0, unroll=2)


def _attn_lat(proj, sink, cache_k, cache_v, layer, tables, nb, n):
    qw = ATT_HEADS * HEAD_DIM
    gw = ATT_GROUPS * HEAD_DIM
    past = cache_k.shape[2]
    kcol = qw // HEAD_DIM
    vcol = kcol + ATT_KV_HEADS
    cspec = pl.BlockSpec((None, None, past, HEAD_DIM), lambda b, h: (b, layer, 0, h))
    tspec = pl.BlockSpec((n, HEAD_DIM), lambda b, h: (0, 0))
    return pl.pallas_call(
        _attn_lat_kernel,
        grid=(nb, ATT_KV_HEADS),
        in_specs=[pl.BlockSpec(memory_space=pltpu.SMEM),
                  pl.BlockSpec((n, gw), lambda b, h: (b, h)),
                  pl.BlockSpec((n, HEAD_DIM), lambda b, h: (b, kcol + h)),
                  pl.BlockSpec((n, HEAD_DIM), lambda b, h: (b, vcol + h)),
                  cspec, cspec, tspec, tspec, tspec],
        out_specs=pl.BlockSpec((n, gw), lambda b, h: (b, h)),
        out_shape=jax.ShapeDtypeStruct((nb * n, qw), BF16),
        scratch_shapes=[pltpu.VMEM((n // Q_BLOCK, HEAD_DIM, Q_BLOCK), BF16),
                        pltpu.VMEM((n, HEAD_DIM), BF16), pltpu.VMEM((n, gw), BF16)],
        compiler_params=_params(("arbitrary", "arbitrary"),
                                2 * n * (gw + 2 * HEAD_DIM) * 4 + 2 * n * gw * 2 + 6 * n * HEAD_DIM * 4
                                + 4 * n * HEAD_DIM + n * gw * 2),
    )(sink, proj, proj, proj, cache_k, cache_v, *tables)


def _rope_tables(n):
    pos = jnp.arange(n)
    row = (pos // GRID_W).astype(F32)
    col = (pos % GRID_W).astype(F32)
    inv = ROPE_THETA ** (-jnp.arange(ROPE_FREQS, dtype=F32) / ROPE_FREQS)
    ang_r = row[:, None] * inv[None, :]
    ang_c = col[:, None] * inv[None, :]
    cr, sr, cc, sc = jnp.cos(ang_r), jnp.sin(ang_r), jnp.cos(ang_c), jnp.sin(ang_c)
    z = jnp.zeros_like(cr)
    cos = jnp.concatenate([cr, cr, cc, cc], axis=1)
    sa = jnp.concatenate([-sr, z, -sc, z], axis=1)
    sb = jnp.concatenate([z, sr, z, sc], axis=1)
    return cos, sa, sb


def _gla_kernel(*refs, has_init):
    if has_init:
        (q_ref, k_ref, v_ref, g_ref, dec_ref, wf_ref, wb_ref, bd_ref, ng_ref, s0_ref,
         o_ref, vb_s, qin_s, kdec_s, tot_s, o_s, kv_s, sin_s) = refs
        sfin_ref = None
    else:
        (q_ref, k_ref, v_ref, g_ref, dec_ref, wf_ref, wb_ref, bd_ref, ng_ref,
         o_ref, sfin_ref, vb_s, qin_s, kdec_s, tot_s, o_s, kv_s, sin_s) = refs
    n = q_ref.shape[0]
    c = GLA_CHUNK
    nc = n // c
    blk = min(n, GLA_BULK_ROWS)
    qscale = GLA_DK ** -0.5
    shift = c.bit_length() - 1
    ri = lax.broadcasted_iota(jnp.int32, (blk, blk), 0)
    ci = lax.broadcasted_iota(jnp.int32, (blk, blk), 1)
    same = lax.shift_right_logical(ri, shift) == lax.shift_right_logical(ci, shift)
    keep = (same & (ri >= ci), same & (ri <= ci))
    sums = tuple(jnp.concatenate([keep[d].astype(BF16), same.astype(BF16)], axis=0) for d in range(2))
    wdec = (wf_ref[...], wb_ref[...])

    def chunk_sums(t, x):
        hi = x.astype(BF16)
        r1 = x - hi.astype(F32)
        mid = r1.astype(BF16)
        lo = (r1 - mid.astype(F32)).astype(BF16)
        y = jnp.dot(t, jnp.concatenate([hi, mid, lo], axis=1), preferred_element_type=F32)
        y = (y[:, :LANE] + y[:, LANE:2 * LANE]) + y[:, 2 * LANE:]
        return y[:blk], y[blk:]

    def bulk(i, carry):
        rows = pl.ds(pl.multiple_of(i * blk, blk), blk)
        decb = dec_ref[rows, :].astype(BF16)
        qc = q_ref[rows, :] * qscale
        kc = k_ref[rows, :]
        vc = v_ref[rows, :].astype(BF16)
        vb_s[rows, :] = vc
        for d in range(2):
            z = jnp.dot(decb, wdec[d], preferred_element_type=F32) + bd_ref[d]
            cum, tot = chunk_sums(sums[d], _log_sigmoid(z) * (1.0 / GLA_TAU))
            q_in = (qc * jnp.exp(cum)).astype(BF16)
            k_in = (kc * jnp.exp(-cum)).astype(BF16)
            att = jnp.where(keep[d], lax.dot_general(q_in, k_in, NT_DIMS, preferred_element_type=F32), 0.0)
            o_s[d, rows, :] = jnp.dot(att.astype(BF16), vc, preferred_element_type=F32)
            qin_s[d, rows, :] = q_in
            kdec_s[d, rows, :] = (kc * jnp.exp(tot - cum)).astype(BF16)
            tot_s[d, rows, :] = tot
        return carry

    lax.fori_loop(0, n // blk, bulk, 0, unroll=min(n // blk, 2))

    def increments(j, carry):
        rows = pl.ds(pl.multiple_of(j * c, c), c)
        vc = vb_s[rows, :]
        for d in range(2):
            kv_s[d, j] = lax.dot_general(vc, kdec_s[d, rows, :], TN_DIMS, preferred_element_type=F32)
        return carry

    lax.fori_loop(0, nc, increments, 0, unroll=min(nc, 4))

    def scan(j, carry):
        sf, sb = carry
        jb = nc - 1 - j
        sin_s[0, j] = sf.astype(BF16)
        sin_s[1, jb] = sb.astype(BF16)
        sf = jnp.exp(tot_s[0, pl.ds(j * c, 1), :]) * sf + kv_s[0, j]
        sb = jnp.exp(tot_s[1, pl.ds(jb * c, 1), :]) * sb + kv_s[1, jb]
        return sf, sb

    if has_init:
        init = (s0_ref[0].T, s0_ref[1].T)
    else:
        init = (jnp.zeros((LANE, GLA_DK), F32), jnp.zeros((LANE, GLA_DK), F32))
    sf, sb = lax.fori_loop(0, nc, scan, init)
    if sfin_ref is not None:
        sfin_ref[0] = sf.T
        sfin_ref[1] = sb.T

    def finish(j, carry):
        rows = pl.ds(pl.multiple_of(j * c, c), c)
        o = ((o_s[0, rows, :] + lax.dot_general(qin_s[0, rows, :], sin_s[0, j], NT_DIMS,
                                                preferred_element_type=F32))
             + (o_s[1, rows, :] + lax.dot_general(qin_s[1, rows, :], sin_s[1, j], NT_DIMS,
                                                  preferred_element_type=F32)))
        o = o * lax.rsqrt(jnp.mean(o * o, axis=-1, keepdims=True) + EPS) * ng_ref[...]
        g = g_ref[rows, :]
        o_ref[rows, :] = (o * (g * _sigmoid(g))).astype(o_ref.dtype)
        return carry

    lax.fori_loop(0, nc, finish, 0, unroll=min(nc, 4))


def _gla(proj, wf, wb, b_dec, norm_g, nb, n, init=None, layer=0):
    hh = GLA_HEADS
    nc = n // GLA_CHUNK
    col = lambda base: pl.BlockSpec((n, LANE), lambda b, h: (b, base + h))
    in_specs = [col(24), col(32), col(40), col(48),
                pl.BlockSpec((n, LANE), lambda b, h: (b, 72)),
                pl.BlockSpec((None, LANE, LANE), lambda b, h: (h, 0, 0)),
                pl.BlockSpec((None, LANE, LANE), lambda b, h: (h, 0, 0)),
                pl.BlockSpec((2, None, 1, LANE), lambda b, h: (0, h, 0, 0)),
                pl.BlockSpec((None, 1, LANE), lambda b, h: (h, 0, 0))]
    args = [proj, proj, proj, proj, proj, wf, wb, b_dec, norm_g]
    o_shape = jax.ShapeDtypeStruct((nb * n, hh * LANE), BF16)
    o_spec = pl.BlockSpec((n, LANE), lambda b, h: (b, h))
    if init is not None:
        in_specs.append(pl.BlockSpec((None, None, 2, None, GLA_DK, LANE),
                                     lambda b, h: (b, layer, 0, h, 0, 0)))
        args.append(init)
        out_shape, out_specs = o_shape, o_spec
    else:
        out_shape = (o_shape, jax.ShapeDtypeStruct((nb, 2, hh, GLA_DK, LANE), F32))
        out_specs = (o_spec, pl.BlockSpec((None, 2, None, GLA_DK, LANE), lambda b, h: (b, 0, h, 0, 0)))
    return pl.pallas_call(
        functools.partial(_gla_kernel, has_init=init is not None),
        grid=(nb, hh),
        in_specs=in_specs,
        out_specs=out_specs,
        out_shape=out_shape,
        scratch_shapes=[pltpu.VMEM((n, LANE), BF16),
                        pltpu.VMEM((2, n, GLA_DK), BF16),
                        pltpu.VMEM((2, n, GLA_DK), BF16),
                        pltpu.VMEM((2, n, GLA_DK), F32),
                        pltpu.VMEM((2, n, LANE), F32),
                        pltpu.VMEM((2, nc, LANE, GLA_DK), F32),
                        pltpu.VMEM((2, nc, LANE, GLA_DK), BF16)],
        compiler_params=_params(("arbitrary", "arbitrary"),
                                2 * 5 * n * LANE * 4 + 2 * n * LANE * 2 + n * LANE * (2 + 8 + 16)
                                + nc * LANE * GLA_DK * 12 + 4 * 2**20),
    )(*args)


def _scan_levels(n):
    levels = []
    size = n
    while size > SCAN_TOP_ROWS:
        radix = 4 if not levels else 8
        levels.append((size, radix))
        size //= radix
    return levels, size


def _linear_scan(a, u, lv, n, h0, rev):
    levels, top = _scan_levels(n)

    def sweep(a_l, u_l, size, radix, body):
        m = size // radix
        sb = min(m, SCAN_SLAB_ROWS)
        first = radix - 1 if rev else 0
        order = list(range(radix - 2, -1, -1)) if rev else list(range(1, radix))

        def blk(j, carry):
            sl = lambda r: pl.ds(j * (sb * radix) + r, sb, stride=radix)
            body(sl, pl.ds(pl.multiple_of(j * sb, sb), sb), first, order)
            return carry

        if m // sb == 1:
            blk(0, 0)
        else:
            lax.fori_loop(0, m // sb, blk, 0)

    src_a, src_u = a, u
    for l, (size, radix) in enumerate(levels):
        nxt = lv[l]

        def up(sl, dense, first, order, src_a=src_a, src_u=src_u, nxt=nxt):
            pa = src_a[sl(first), :]
            pu = src_u[sl(first), :]
            for r in order:
                ar = src_a[sl(r), :]
                pu = ar * pu + src_u[sl(r), :]
                pa = ar * pa
                src_a[sl(r), :] = pa
                src_u[sl(r), :] = pu
            nxt[0, dense, :] = pa
            nxt[1, dense, :] = pu

        sweep(src_a, src_u, size, radix, up)
        src_a, src_u = nxt.at[0], nxt.at[1]

    carry = h0
    top_c = lv[len(levels) - 1].at[2]
    for g in (range(top - 1, -1, -1) if rev else range(top)):
        top_c[g:g + 1, :] = carry
        carry = src_a[g:g + 1, :] * carry + src_u[g:g + 1, :]

    for l in range(len(levels) - 1, -1, -1):
        size, radix = levels[l]
        a_l, u_l = (a, u) if l == 0 else (lv[l - 1].at[0], lv[l - 1].at[1])
        c_l = lv[l].at[2]
        c_below = None if l == 0 else lv[l - 1].at[2]

        def down(sl, dense, first, order, a_l=a_l, u_l=u_l, c_l=c_l, c_below=c_below):
            cin = c_l[dense, :]
            prev = cin
            for r in [first] + order:
                h = u_l[sl(r), :] + a_l[sl(r), :] * cin
                if c_below is None:
                    u_l[sl(r), :] = h
                else:
                    c_below[sl(r), :] = prev
                prev = h

        sweep(a_l, u_l, size, radix, down)
    return carry


def _lru_kernel(*refs, has_init):
    if has_init:
        (x_ref, y_ref, cw_ref, cb_ref, wr_ref, br_ref, wi_ref, bi_ref, lam_ref, h0_ref,
         o_ref, xp_s, a_s, u_s, *lv_s) = refs
        hfin_ref = None
    else:
        (x_ref, y_ref, cw_ref, cb_ref, wr_ref, br_ref, wi_ref, bi_ref, lam_ref,
         o_ref, hfin_ref, xp_s, a_s, u_s, *lv_s) = refs
    n = x_ref.shape[0]
    pad = SUBLANE
    xp_s[0:pad, :] = jnp.zeros((pad, LANE), F32)
    xp_s[pad + n:pad + n + pad, :] = jnp.zeros((pad, LANE), F32)
    xp_s[pad:pad + n, :] = x_ref[...]
    xc = xp_s[pl.ds(pad - CONV_LEFT, n), :] * cw_ref[0:1, :]
    for j in range(1, LRU_CONV):
        xc = xc + xp_s[pl.ds(pad - CONV_LEFT + j, n), :] * cw_ref[j:j + 1, :]
    xc = xc + cb_ref[...]
    xcb = xc.astype(BF16)
    for d in range(2):
        r = _sigmoid(jnp.dot(xcb, wr_ref[d], preferred_element_type=F32) + br_ref[d])
        i = _sigmoid(jnp.dot(xcb, wi_ref[d], preferred_element_type=F32) + bi_ref[d])
        log_a = -LRU_C * r * _softplus(-lam_ref[d])
        a = jnp.exp(log_a)
        a_s[d] = a
        u_s[d] = jnp.sqrt(-jnp.tanh(log_a) * (a * a + 1.0)) * (i * xc)

    for d in range(2):
        h0 = h0_ref[d] if has_init else jnp.zeros((1, LANE), F32)
        h_last = _linear_scan(a_s.at[d], u_s.at[d], [lv.at[d] for lv in lv_s], n, h0, rev=d == 1)
        if hfin_ref is not None:
            hfin_ref[d] = h_last
    o_ref[...] = ((u_s[0] + u_s[1]) * jax.nn.gelu(y_ref[...])).astype(o_ref.dtype)


def _lru(proj, conv_w, conv_b, w_r, b_r, w_i, b_i, lam, nb, n, init=None, layer=0):
    kb = conv_w.shape[1] // LRU_BLOCK
    vec = lambda rows: pl.BlockSpec((rows, LANE), lambda b, k: (0, k))
    vec2 = pl.BlockSpec((2, 1, LANE), lambda b, k: (0, 0, k))
    wspec = pl.BlockSpec((2, None, LRU_BLOCK, LRU_BLOCK), lambda b, k: (0, k, 0, 0))
    in_specs = [pl.BlockSpec((n, LANE), lambda b, k: (b, 56 + k)),
                pl.BlockSpec((n, LANE), lambda b, k: (b, 64 + k)),
                vec(LRU_CONV), vec(1), wspec, vec2, wspec, vec2, vec2]
    args = [proj, proj, conv_w, conv_b, w_r, b_r, w_i, b_i, lam]
    o_shape = jax.ShapeDtypeStruct((nb * n, kb * LANE), BF16)
    o_spec = pl.BlockSpec((n, LANE), lambda b, k: (b, k))
    if init is not None:
        in_specs.append(pl.BlockSpec((None, None, 2, 1, LANE), lambda b, k: (b, layer, 0, 0, k)))
        args.append(init)
        out_shape, out_specs = o_shape, o_spec
    else:
        out_shape = (o_shape, jax.ShapeDtypeStruct((nb, 2, 1, kb * LANE), F32))
        out_specs = (o_spec, pl.BlockSpec((None, 2, 1, LANE), lambda b, k: (b, 0, 0, k)))
    return pl.pallas_call(
        functools.partial(_lru_kernel, has_init=init is not None),
        grid=(nb, kb),
        in_specs=in_specs,
        out_specs=out_specs,
        out_shape=out_shape,
        scratch_shapes=[pltpu.VMEM((n + 2 * SUBLANE, LANE), F32),
                        pltpu.VMEM((2, n, LANE), F32), pltpu.VMEM((2, n, LANE), F32)]
                       + [pltpu.VMEM((2, 3, size // radix, LANE), F32) for size, radix in _scan_levels(n)[0]],
        compiler_params=_params(("arbitrary", "arbitrary"), 2 * 2 * n * LANE * 4 + 12 * n * LANE * 4),
    )(*args)


def _pad_cols(w, n):
    return jnp.pad(w, ((0, 0), (0, n - w.shape[1])))


def kernel(x_prompt, x_sample, cache_attn_k, cache_attn_v, state_gla, state_lru, c, c_ctx, w_ada, b_ada, norm1_g, w_in, attn_sink, gla_w_decay, gla_b_decay, gla_norm_g, lru_conv_w, lru_conv_b, lru_w_rgate, lru_b_rgate, lru_w_igate, lru_b_igate, lru_lambda, w_out, norm2_g, w_gu, w_down, final_norm_g):
    nb_c, s_len, d = x_prompt.shape
    nb_l, n_lat, _ = x_sample.shape
    depth = w_in.shape[0]
    d_ff = w_down.shape[1]
    lru_w = lru_conv_w.shape[2]
    qw, kw = ATT_HEADS * HEAD_DIM, ATT_KV_HEADS * HEAD_DIM
    gdec0 = qw + 2 * kw + 4 * GLA_HEADS * GLA_DK
    gdec1 = gdec0 + 2 * GLA_LOWRANK
    in_pad = 19 * 512
    ff_pad = -(-d_ff // 512) * 512

    cvecs = jnp.concatenate([c_ctx[None], c, jnp.zeros((MOD_ROWS - 1 - nb_l, d), F32)], axis=0)
    modr = _ada(cvecs, w_ada, b_ada).reshape(depth, MOD_ROWS, 6, 1, d)
    row_ctx = lambda r0: 0
    row_lat = lambda r0: 1 + r0 // n_lat

    tables = _rope_tables(n_lat)
    cache_k = cache_attn_k.reshape(nb_l, depth, cache_attn_k.shape[2], kw)
    cache_v = cache_attn_v.reshape(nb_l, depth, cache_attn_v.shape[2], kw)
    lru_init = state_lru.reshape(nb_l, depth, 2, 1, lru_w)

    xc = x_prompt.reshape(nb_c * s_len, d)
    xl = x_sample.reshape(nb_l * n_lat, d)
    new_k, new_v, new_gla, new_lru = [], [], [], []
    for l in range(depth):
        w_in_l = _pad_cols(jnp.concatenate(
            [w_in[l][:, :gdec0], w_in[l][:, gdec1:], w_in[l][:, gdec0:gdec1]], axis=1), in_pad).astype(BF16)
        w_out_l = w_out[l].astype(BF16)
        w_g = _pad_cols(w_gu[l][:, :d_ff], ff_pad).astype(BF16)
        w_u = _pad_cols(w_gu[l][:, d_ff:], ff_pad).astype(BF16)
        w_dn = jnp.pad(w_down[l], ((0, ff_pad - d_ff), (0, 0))).astype(BF16)
        wdec = gla_w_decay[l].reshape(2, GLA_LOWRANK, GLA_HEADS, GLA_DK).transpose(0, 2, 1, 3)
        zpad = jnp.zeros((GLA_HEADS, GLA_LOWRANK, GLA_DK), F32)
        zrest = jnp.zeros((GLA_HEADS, LANE - 2 * GLA_LOWRANK, GLA_DK), F32)
        wf = jnp.concatenate([wdec[0], zpad, zrest], axis=1).astype(BF16)
        wb = jnp.concatenate([zpad, wdec[1], zrest], axis=1).astype(BF16)
        b_dec = gla_b_decay[l].reshape(2, GLA_HEADS, 1, GLA_DK)
        ng = gla_norm_g[l].reshape(GLA_HEADS, 1, LANE)
        lru_args = (lru_conv_w[l], lru_conv_b[l].reshape(1, lru_w),
                    lru_w_rgate[l].astype(BF16), lru_b_rgate[l].reshape(2, 1, lru_w),
                    lru_w_igate[l].astype(BF16), lru_b_igate[l].reshape(2, 1, lru_w),
                    lru_lambda[l].reshape(2, 1, lru_w))
        sink = attn_sink[l]

        def ffn(x, mix, row_of):
            x = _out_proj(*mix, w_out_l, x, modr, l, row_of, 2)
            h2 = _norm_mod(x, norm2_g[l], modr, l, row_of, 4, 3)
            act = _gate_up(h2, w_g, w_u)
            return _down_proj(act, w_dn, x, modr, l, row_of, 5)

        h = _norm_mod(xc, norm1_g[l], modr, l, row_ctx, 1, 0)
        proj = _in_proj(h, w_in_l)
        attn = _attn_ctx(proj, sink, nb_c, s_len)
        gla, gla_fin = _gla(proj, wf, wb, b_dec, ng, nb_c, s_len)
        lru, lru_fin = _lru(proj, *lru_args, nb_c, s_len)
        new_k.append(proj[:, qw:qw + kw].reshape(nb_c, s_len, ATT_KV_HEADS, HEAD_DIM))
        new_v.append(proj[:, qw + kw:qw + 2 * kw].reshape(nb_c, s_len, ATT_KV_HEADS, HEAD_DIM))
        new_gla.append(gla_fin)
        new_lru.append(lru_fin.reshape(nb_c, 2, lru_w))
        xc = ffn(xc, (attn, gla, lru), row_ctx)

        h = _norm_mod(xl, norm1_g[l], modr, l, row_lat, 1, 0)
        proj = _in_proj(h, w_in_l)
        attn = _attn_lat(proj, sink, cache_k, cache_v, l, tables, nb_l, n_lat)
        gla = _gla(proj, wf, wb, b_dec, ng, nb_l, n_lat, init=state_gla, layer=l)
        lru = _lru(proj, *lru_args, nb_l, n_lat, init=lru_init, layer=l)
        xl = ffn(xl, (attn, gla, lru), row_lat)

    y_prompt = _final_norm(xc, final_norm_g).reshape(nb_c, s_len, d)
    y_sample = _final_norm(xl, final_norm_g).reshape(nb_l, n_lat, d)
    return (y_prompt, y_sample, jnp.stack(new_k, axis=1), jnp.stack(new_v, axis=1),
            jnp.stack(new_gla, axis=1), jnp.stack(new_lru, axis=1))
```

```python
import functools
from typing import NamedTuple

import jax
import jax.numpy as jnp
from jax import lax
from jax.experimental import pallas as pl
from jax.experimental.pallas import tpu as pltpu

F32 = jnp.float32
BF16 = jnp.bfloat16

EPS = 1e-6
HEAD_DIM = 128
ATT_HEADS = 16
ATT_KV_HEADS = 4
ATT_GROUPS = ATT_HEADS // ATT_KV_HEADS
WINDOW = 128
Q_BLOCK = 128
GRID_W = 64
ROPE_THETA = 10000.0
ROPE_FREQS = HEAD_DIM // 4
GLA_HEADS = 8
GLA_DK = 128
GLA_LOWRANK = 16
GLA_TAU = 16.0
GLA_CHUNK = 64
GLA_BULK_ROWS = 256
LRU_BLOCK = 128
LRU_CONV = 4
CONV_LEFT = 2
LRU_C = 8.0
MIXER_ROWS_PER_STEP = 1024
SCAN_TOP_ROWS = 8
SCAN_SLAB_ROWS = 64

LANE = 128
SUBLANE = 8
VMEM_BYTES_V7X = 64 * 2**20
MOD_ROWS = 16

NT_DIMS = (((1,), (1,)), ((), ()))
TN_DIMS = (((0,), (0,)), ((), ()))


def _params(sem, est_bytes):
    limit = int(min(VMEM_BYTES_V7X - 4 * 2**20, max(est_bytes + 8 * 2**20, 32 * 2**20)))
    return pltpu.CompilerParams(dimension_semantics=sem, vmem_limit_bytes=limit)


def _seqs_per_step(nb, n):
    bt = max(1, MIXER_ROWS_PER_STEP // n)
    while nb % bt:
        bt -= 1
    return bt


def _softplus(x):
    return jnp.maximum(x, 0.0) + jnp.log1p(jnp.exp(-jnp.abs(x)))


def _log_sigmoid(x):
    return -_softplus(-x)


def _sigmoid(x):
    return 0.5 * jnp.tanh(0.5 * x) + 0.5


def _ada_kernel(c_ref, w_ref, b_ref, o_ref):
    cv = c_ref[...]
    s = (cv * jax.nn.sigmoid(cv)).astype(BF16)
    o_ref[...] = jnp.dot(s, w_ref[...].astype(BF16), preferred_element_type=F32) + b_ref[...]


def _ada(cvecs, w_ada, b_ada):
    nl, d, n6 = w_ada.shape
    tn = 512
    return pl.pallas_call(
        _ada_kernel,
        grid=(nl, n6 // tn),
        in_specs=[pl.BlockSpec((MOD_ROWS, d), lambda l, j: (0, 0)),
                  pl.BlockSpec((None, d, tn), lambda l, j: (l, 0, j)),
                  pl.BlockSpec((None, 1, tn), lambda l, j: (l, 0, j))],
        out_specs=pl.BlockSpec((None, MOD_ROWS, tn), lambda l, j: (l, 0, j)),
        out_shape=jax.ShapeDtypeStruct((nl, MOD_ROWS, n6), F32),
        compiler_params=_params(("arbitrary", "arbitrary"), 2 * d * tn * 4 + d * tn * 2),
    )(cvecs, w_ada, b_ada.reshape(nl, 1, n6))


def _norm_mod_kernel(x_ref, g_ref, sc_ref, sh_ref, o_ref):
    x = x_ref[...]
    y = x * lax.rsqrt(jnp.mean(x * x, axis=-1, keepdims=True) + EPS) * g_ref[...]
    o_ref[...] = (y * (1.0 + sc_ref[...]) + sh_ref[...]).astype(o_ref.dtype)


def _norm_kernel(x_ref, g_ref, o_ref):
    x = x_ref[...]
    y = x * lax.rsqrt(jnp.mean(x * x, axis=-1, keepdims=True) + EPS) * g_ref[...]
    o_ref[...] = y.astype(o_ref.dtype)


class _ModRows(NamedTuple):
    first: int
    segment: int

    def tile(self, m, want):
        tm = min(want, m, self.segment)
        assert m % tm == 0 and self.segment % tm == 0
        return tm

    def __call__(self, row0):
        return self.first + row0 // self.segment


def _norm_mod(x, g, modr, layer, row_of, sc_idx, sh_idx, tr=256):
    m, d = x.shape
    tr = row_of.tile(m, tr)
    mspec = lambda k: pl.BlockSpec((None, None, None, 1, d), lambda i: (layer, row_of(i * tr), k, 0, 0))
    return pl.pallas_call(
        _norm_mod_kernel,
        grid=(m // tr,),
        in_specs=[pl.BlockSpec((tr, d), lambda i: (i, 0)),
                  pl.BlockSpec((1, d), lambda i: (0, 0)),
                  mspec(sc_idx), mspec(sh_idx)],
        out_specs=pl.BlockSpec((tr, d), lambda i: (i, 0)),
        out_shape=jax.ShapeDtypeStruct((m, d), BF16),
        compiler_params=_params(("arbitrary",), 2 * tr * d * 6 + 2 * tr * d * 4),
    )(x, g.reshape(1, d), modr, modr)


def _final_norm(x, g, tr=256):
    m, d = x.shape
    return pl.pallas_call(
        _norm_kernel,
        grid=(m // tr,),
        in_specs=[pl.BlockSpec((tr, d), lambda i: (i, 0)),
                  pl.BlockSpec((1, d), lambda i: (0, 0))],
        out_specs=pl.BlockSpec((tr, d), lambda i: (i, 0)),
        out_shape=jax.ShapeDtypeStruct((m, d), F32),
        compiler_params=_params(("arbitrary",), 2 * tr * d * 8 + 2 * tr * d * 4),
    )(x, g.reshape(1, d))


def _mm_kernel(x_ref, w_ref, o_ref):
    o_ref[...] = jnp.dot(x_ref[...], w_ref[...], preferred_element_type=F32)


def _in_proj(h, w, layer, tm=2048, tn=512):
    m, k = h.shape
    n = w.shape[2]
    tm = min(tm, m)
    return pl.pallas_call(
        _mm_kernel,
        grid=(m // tm, n // tn),
        in_specs=[pl.BlockSpec((tm, k), lambda i, j: (i, 0)),
                  pl.BlockSpec((None, k, tn), lambda i, j: (layer, 0, j))],
        out_specs=pl.BlockSpec((tm, tn), lambda i, j: (i, j)),
        out_shape=jax.ShapeDtypeStruct((m, n), F32),
        compiler_params=_params(("arbitrary", "arbitrary"), 2 * (tm * k * 2 + k * tn * 2 + tm * tn * 4)),
    )(h, w)


def _out_proj_kernel(a_ref, b_ref, c_ref, w_ref, x_ref, g_ref, o_ref):
    ka, kb = a_ref.shape[1], b_ref.shape[1]
    acc = jnp.dot(a_ref[...], w_ref[0:ka, :], preferred_element_type=F32)
    acc += jnp.dot(b_ref[...], w_ref[ka:ka + kb, :], preferred_element_type=F32)
    acc += jnp.dot(c_ref[...], w_ref[ka + kb:, :], preferred_element_type=F32)
    o_ref[...] = x_ref[...] + g_ref[...] * acc


def _out_proj(attn, gla, lru, w, x, modr, layer, row_of, gate_idx, tm=1024, tn=1024):
    m, d = x.shape
    k = w.shape[1]
    tm = row_of.tile(m, tm)
    xs = lambda a: pl.BlockSpec((tm, a.shape[1]), lambda i, j: (i, 0))
    return pl.pallas_call(
        _out_proj_kernel,
        grid=(m // tm, d // tn),
        in_specs=[xs(attn), xs(gla), xs(lru),
                  pl.BlockSpec((None, k, tn), lambda i, j: (layer, 0, j)),
                  pl.BlockSpec((tm, tn), lambda i, j: (i, j)),
                  pl.BlockSpec((None, None, None, 1, tn),
                               lambda i, j: (layer, row_of(i * tm), gate_idx, 0, j))],
        out_specs=pl.BlockSpec((tm, tn), lambda i, j: (i, j)),
        out_shape=jax.ShapeDtypeStruct((m, d), F32),
        compiler_params=_params(("arbitrary", "arbitrary"),
                                2 * (tm * k * 2 + k * tn * 2 + 2 * tm * tn * 4) + tm * tn * 4),
    )(attn, gla, lru, w, x, modr)


def _gate_up_kernel(h_ref, wg_ref, wu_ref, o_ref):
    h = h_ref[...]
    a = jnp.dot(h, wg_ref[...], preferred_element_type=F32)
    u = jnp.dot(h, wu_ref[...], preferred_element_type=F32)
    o_ref[...] = (a * _sigmoid(a) * u).astype(o_ref.dtype)


def _gate_up(h, w, layer, tm=2048, tn=256):
    m, k = h.shape
    f = w.shape[2] // 2
    tm = min(tm, m)
    nj = f // tn
    return pl.pallas_call(
        _gate_up_kernel,
        grid=(m // tm, nj),
        in_specs=[pl.BlockSpec((tm, k), lambda i, j: (i, 0)),
                  pl.BlockSpec((None, k, tn), lambda i, j: (layer, 0, j)),
                  pl.BlockSpec((None, k, tn), lambda i, j: (layer, 0, nj + j))],
        out_specs=pl.BlockSpec((tm, tn), lambda i, j: (i, j)),
        out_shape=jax.ShapeDtypeStruct((m, f), BF16),
        compiler_params=_params(("arbitrary", "arbitrary"),
                                2 * (tm * k * 2 + 2 * k * tn * 2 + tm * tn * 2) + 3 * tm * tn * 4),
    )(h, w, w)


def _down_kernel(a_ref, w_ref, x_ref, g_ref, o_ref):
    kk = pl.program_id(2)
    part = jnp.dot(a_ref[...], w_ref[...], preferred_element_type=F32)

    last = pl.num_programs(2) - 1

    @pl.when(kk == 0)
    def _():
        o_ref[...] = part

    @pl.when(jnp.logical_and(kk > 0, kk < last))
    def _():
        o_ref[...] += part

    @pl.when(kk == last)
    def _():
        o_ref[...] = x_ref[...] + g_ref[...] * (o_ref[...] + part)


def _down_proj(act, w, x, modr, layer, row_of, gate_idx, tm=1024, tn=512, ksplit=2):
    m, d = x.shape
    tm = row_of.tile(m, tm)
    assert ksplit >= 2 and w.shape[1] % (ksplit * LANE) == 0
    tk = w.shape[1] // ksplit
    return pl.pallas_call(
        _down_kernel,
        grid=(m // tm, d // tn, ksplit),
        in_specs=[pl.BlockSpec((tm, tk), lambda i, j, kk: (i, kk)),
                  pl.BlockSpec((None, tk, tn), lambda i, j, kk: (layer, kk, j)),
                  pl.BlockSpec((tm, tn), lambda i, j, kk: (i, j)),
                  pl.BlockSpec((None, None, None, 1, tn),
                               lambda i, j, kk: (layer, row_of(i * tm), gate_idx, 0, j))],
        out_specs=pl.BlockSpec((tm, tn), lambda i, j, kk: (i, j)),
        out_shape=jax.ShapeDtypeStruct((m, d), F32),
        compiler_params=_params(("arbitrary", "arbitrary", "arbitrary"),
                                2 * (tm * tk * 2 + tk * tn * 2 + 2 * tm * tn * 4) + tm * tn * 4),
    )(act, w, x, modr)


def _sink_column(sink_ref, kh, rows):
    return jnp.concatenate(
        [jnp.full((rows, 1), sink_ref[kh * ATT_GROUPS + g], F32) for g in range(ATT_GROUPS)], axis=0)


def _attn_ctx_kernel(sink_ref, q_ref, k_ref, v_ref, o_ref):
    s_len = q_ref.shape[0]
    scale = HEAD_DIM ** -0.5
    for kh in range(ATT_KV_HEADS):
        kk = k_ref[:, kh * HEAD_DIM:(kh + 1) * HEAD_DIM].astype(BF16)
        vv = v_ref[:, kh * HEAD_DIM:(kh + 1) * HEAD_DIM].astype(BF16)
        h0 = kh * ATT_GROUPS
        q = jnp.concatenate([q_ref[:, (h0 + g) * HEAD_DIM:(h0 + g + 1) * HEAD_DIM]
                             for g in range(ATT_GROUPS)], axis=0).astype(BF16)
        s = lax.dot_general(q, kk, NT_DIMS, preferred_element_type=F32) * scale
        sk = _sink_column(sink_ref, kh, s_len)
        m = jnp.maximum(jnp.max(s, axis=-1, keepdims=True), sk)
        p = jnp.exp(s - m)
        denom = jnp.sum(p, axis=-1, keepdims=True) + jnp.exp(sk - m)
        pn = (p * (1.0 / denom)).astype(BF16)
        o = jnp.dot(pn, vv, preferred_element_type=F32)
        for g in range(ATT_GROUPS):
            o_ref[:, (h0 + g) * HEAD_DIM:(h0 + g + 1) * HEAD_DIM] = (
                o[g * s_len:(g + 1) * s_len].astype(o_ref.dtype))


def _attn_ctx(proj, sink, nb, s_len):
    qw = ATT_HEADS * HEAD_DIM
    kw = ATT_KV_HEADS * HEAD_DIM
    return pl.pallas_call(
        _attn_ctx_kernel,
        grid=(nb,),
        in_specs=[pl.BlockSpec(memory_space=pltpu.SMEM),
                  pl.BlockSpec((s_len, qw), lambda b: (b, 0)),
                  pl.BlockSpec((s_len, kw), lambda b: (b, qw // kw)),
                  pl.BlockSpec((s_len, kw), lambda b: (b, qw // kw + 1))],
        out_specs=pl.BlockSpec((s_len, qw), lambda b: (b, 0)),
        out_shape=jax.ShapeDtypeStruct((nb * s_len, qw), BF16),
        compiler_params=_params(("arbitrary",), 2 * s_len * (qw + 2 * kw) * 4 + 2 * s_len * qw * 2
                                + 8 * ATT_GROUPS * s_len * s_len * 4),
    )(sink, proj, proj, proj)


def _attn_lat_kernel(sink_ref, q_ref, k_ref, v_ref, kc_ref, vc_ref, cos_ref, sa_ref, sb_ref,
                     o_ref, kt_s, vb_s, qr_s):
    n = q_ref.shape[0]
    nqb = n // Q_BLOCK
    kh = pl.program_id(1)
    scale = HEAD_DIM ** -0.5
    wblocks = (Q_BLOCK + 2 * WINDOW) // Q_BLOCK

    def rope(x, rows):
        return (x * cos_ref[rows, :] + pltpu.roll(x, HEAD_DIM - ROPE_FREQS, 1) * sa_ref[rows, :]
                + pltpu.roll(x, ROPE_FREQS, 1) * sb_ref[rows, :])

    def prep(j, carry):
        rows = pl.ds(pl.multiple_of(j * Q_BLOCK, Q_BLOCK), Q_BLOCK)
        kt_s[j] = rope(k_ref[rows, :], rows).T.astype(BF16)
        for g in range(ATT_GROUPS):
            lanes = slice(g * HEAD_DIM, (g + 1) * HEAD_DIM)
            qr_s[rows, lanes] = rope(q_ref[rows, lanes], rows).astype(BF16)
        return carry

    lax.fori_loop(0, nqb, prep, 0)
    vb_s[...] = v_ref[...].astype(BF16)
    kct = kc_ref[...].T.astype(BF16)
    vc = vc_ref[...].astype(BF16)
    sk = jnp.concatenate(
        [jnp.full((Q_BLOCK, 1), sink_ref[kh * ATT_GROUPS + g], F32) for g in range(ATT_GROUPS)], axis=0)
    qi = lax.broadcasted_iota(jnp.int32, (Q_BLOCK, Q_BLOCK), 0)
    ki = lax.broadcasted_iota(jnp.int32, (Q_BLOCK, Q_BLOCK), 1)
    rel = qi - ki

    def block(i, carry):
        qs = pl.multiple_of(i * Q_BLOCK, Q_BLOCK)
        kb = jnp.clip(i - WINDOW // Q_BLOCK, 0, nqb - wblocks)
        qr = jnp.concatenate([qr_s[pl.ds(qs, Q_BLOCK), g * HEAD_DIM:(g + 1) * HEAD_DIM]
                              for g in range(ATT_GROUPS)], axis=0)
        s_parts = []
        for j in range(wblocks):
            off = (i - kb - j) * Q_BLOCK
            bias = jnp.where(jnp.abs(rel + off) <= WINDOW, 0.0, -jnp.inf).astype(F32)
            s_parts.append(jnp.dot(qr, kt_s[kb + j], preferred_element_type=F32) * scale
                           + jnp.concatenate([bias] * ATT_GROUPS, axis=0))
        s2 = jnp.dot(qr, kct, preferred_element_type=F32) * scale
        s_parts += [s2[:, j * LANE:(j + 1) * LANE] for j in range(s2.shape[1] // LANE)]
        m = functools.reduce(jnp.maximum, s_parts)
        m = jnp.maximum(jnp.max(m, axis=-1, keepdims=True), sk)
        p_parts = [jnp.exp(s - m) for s in s_parts]
        denom = jnp.sum(functools.reduce(jnp.add, p_parts), axis=-1, keepdims=True) + jnp.exp(sk - m)
        r = 1.0 / denom
        o = None
        for j, p in enumerate(p_parts):
            if j < wblocks:
                vw = vb_s[pl.ds(pl.multiple_of((kb + j) * Q_BLOCK, Q_BLOCK), Q_BLOCK), :]
            else:
                vw = vc[(j - wblocks) * LANE:(j - wblocks + 1) * LANE, :]
            ov = jnp.dot((p * r).astype(BF16), vw, preferred_element_type=F32)
            o = ov if o is None else o + ov
        for g in range(ATT_GROUPS):
            o_ref[pl.ds(qs, Q_BLOCK), g * HEAD_DIM:(g + 1) * HEAD_DIM] = (
                o[g * Q_BLOCK:(g + 1) * Q_BLOCK].astype(o_ref.dtype))
        return carry

    lax.fori_loop(0, nqb, block, 0, unroll=2)


def _attn_lat(proj, sink, cache_k, cache_v, layer, tables, nb, n):
    qw = ATT_HEADS * HEAD_DIM
    gw = ATT_GROUPS * HEAD_DIM
    past = cache_k.shape[2]
    kcol = qw // HEAD_DIM
    vcol = kcol + ATT_KV_HEADS
    cspec = pl.BlockSpec((None, None, past, HEAD_DIM), lambda b, h: (b, layer, 0, h))
    tspec = pl.BlockSpec((n, HEAD_DIM), lambda b, h: (0, 0))
    return pl.pallas_call(
        _attn_lat_kernel,
        grid=(nb, ATT_KV_HEADS),
        in_specs=[pl.BlockSpec(memory_space=pltpu.SMEM),
                  pl.BlockSpec((n, gw), lambda b, h: (b, h)),
                  pl.BlockSpec((n, HEAD_DIM), lambda b, h: (b, kcol + h)),
                  pl.BlockSpec((n, HEAD_DIM), lambda b, h: (b, vcol + h)),
                  cspec, cspec, tspec, tspec, tspec],
        out_specs=pl.BlockSpec((n, gw), lambda b, h: (b, h)),
        out_shape=jax.ShapeDtypeStruct((nb * n, qw), BF16),
        scratch_shapes=[pltpu.VMEM((n // Q_BLOCK, HEAD_DIM, Q_BLOCK), BF16),
                        pltpu.VMEM((n, HEAD_DIM), BF16), pltpu.VMEM((n, gw), BF16)],
        compiler_params=_params(("arbitrary", "arbitrary"),
                                2 * n * (gw + 2 * HEAD_DIM) * 4 + 2 * n * gw * 2 + 6 * n * HEAD_DIM * 4
                                + 4 * n * HEAD_DIM + n * gw * 2),
    )(sink, proj, proj, proj, cache_k, cache_v, *tables)


def _rope_tables(n):
    pos = jnp.arange(n)
    row = (pos // GRID_W).astype(F32)
    col = (pos % GRID_W).astype(F32)
    inv = ROPE_THETA ** (-jnp.arange(ROPE_FREQS, dtype=F32) / ROPE_FREQS)
    ang_r = row[:, None] * inv[None, :]
    ang_c = col[:, None] * inv[None, :]
    cr, sr, cc, sc = jnp.cos(ang_r), jnp.sin(ang_r), jnp.cos(ang_c), jnp.sin(ang_c)
    z = jnp.zeros_like(cr)
    cos = jnp.concatenate([cr, cr, cc, cc], axis=1)
    sa = jnp.concatenate([-sr, z, -sc, z], axis=1)
    sb = jnp.concatenate([z, sr, z, sc], axis=1)
    return cos, sa, sb


def _gla_kernel(*refs, has_init, seq):
    if has_init:
        (q_ref, k_ref, v_ref, g_ref, dec_ref, wf_ref, wb_ref, bd_ref, ng_ref, s0_ref,
         o_ref, vb_s, qin_s, kdec_s, tot_s, o_s, kv_s, sin_s) = refs
        sfin_ref = None
    else:
        (q_ref, k_ref, v_ref, g_ref, dec_ref, wf_ref, wb_ref, bd_ref, ng_ref,
         o_ref, sfin_ref, vb_s, qin_s, kdec_s, tot_s, o_s, kv_s, sin_s) = refs
    n = q_ref.shape[0]
    c = GLA_CHUNK
    nc = n // c
    blk = min(seq, GLA_BULK_ROWS)
    qscale = GLA_DK ** -0.5
    shift = c.bit_length() - 1
    ri = lax.broadcasted_iota(jnp.int32, (blk, blk), 0)
    ci = lax.broadcasted_iota(jnp.int32, (blk, blk), 1)
    same = lax.shift_right_logical(ri, shift) == lax.shift_right_logical(ci, shift)
    keep = (same & (ri >= ci), same & (ri <= ci))
    sums = tuple(jnp.concatenate([keep[d].astype(BF16), same.astype(BF16)], axis=0) for d in range(2))
    wdec = (wf_ref[...], wb_ref[...])

    def chunk_sums(t, x):
        hi = x.astype(BF16)
        r1 = x - hi.astype(F32)
        mid = r1.astype(BF16)
        lo = (r1 - mid.astype(F32)).astype(BF16)
        y = jnp.dot(t, jnp.concatenate([hi, mid, lo], axis=1), preferred_element_type=F32)
        y = (y[:, :LANE] + y[:, LANE:2 * LANE]) + y[:, 2 * LANE:]
        return y[:blk], y[blk:]

    def bulk(i, carry):
        rows = pl.ds(pl.multiple_of(i * blk, blk), blk)
        decb = dec_ref[rows, :].astype(BF16)
        qc = q_ref[rows, :] * qscale
        kc = k_ref[rows, :]
        vc = v_ref[rows, :].astype(BF16)
        vb_s[rows, :] = vc
        for d in range(2):
            z = jnp.dot(decb, wdec[d], preferred_element_type=F32) + bd_ref[d]
            cum, tot = chunk_sums(sums[d], _log_sigmoid(z) * (1.0 / GLA_TAU))
            q_in = (qc * jnp.exp(cum)).astype(BF16)
            k_in = (kc * jnp.exp(-cum)).astype(BF16)
            att = jnp.where(keep[d], lax.dot_general(q_in, k_in, NT_DIMS, preferred_element_type=F32), 0.0)
            o_s[d, rows, :] = jnp.dot(att.astype(BF16), vc, preferred_element_type=F32)
            qin_s[d, rows, :] = q_in
            kdec_s[d, rows, :] = (kc * jnp.exp(tot - cum)).astype(BF16)
            tot_s[d, rows, :] = tot
        return carry

    lax.fori_loop(0, n // blk, bulk, 0, unroll=min(n // blk, 2))

    def increments(j, carry):
        rows = pl.ds(pl.multiple_of(j * c, c), c)
        vc = vb_s[rows, :]
        for d in range(2):
            kv_s[d, j] = lax.dot_general(vc, kdec_s[d, rows, :], TN_DIMS, preferred_element_type=F32)
        return carry

    lax.fori_loop(0, nc, increments, 0, unroll=min(nc, 4))

    ncs = seq // c
    for s in range(n // seq):
        def scan(j, carry, base=s * ncs):
            sf, sb = carry
            jf = base + j
            jb = base + ncs - 1 - j
            sin_s[0, jf] = sf.astype(BF16)
            sin_s[1, jb] = sb.astype(BF16)
            sf = jnp.exp(tot_s[0, pl.ds(jf * c, 1), :]) * sf + kv_s[0, jf]
            sb = jnp.exp(tot_s[1, pl.ds(jb * c, 1), :]) * sb + kv_s[1, jb]
            return sf, sb

        if has_init:
            init = (s0_ref[s, 0].T, s0_ref[s, 1].T)
        else:
            init = (jnp.zeros((LANE, GLA_DK), F32), jnp.zeros((LANE, GLA_DK), F32))
        sf, sb = lax.fori_loop(0, ncs, scan, init)
        if sfin_ref is not None:
            sfin_ref[s, 0] = sf.T
            sfin_ref[s, 1] = sb.T

    def finish(j, carry):
        rows = pl.ds(pl.multiple_of(j * c, c), c)
        o = ((o_s[0, rows, :] + lax.dot_general(qin_s[0, rows, :], sin_s[0, j], NT_DIMS,
                                                preferred_element_type=F32))
             + (o_s[1, rows, :] + lax.dot_general(qin_s[1, rows, :], sin_s[1, j], NT_DIMS,
                                                  preferred_element_type=F32)))
        o = o * lax.rsqrt(jnp.mean(o * o, axis=-1, keepdims=True) + EPS) * ng_ref[...]
        g = g_ref[rows, :]
        o_ref[rows, :] = (o * (g * _sigmoid(g))).astype(o_ref.dtype)
        return carry

    lax.fori_loop(0, nc, finish, 0, unroll=min(nc, 4))


def _gla(proj, wf, wb, b_dec, norm_g, nb, n, init=None, layer=0):
    hh = GLA_HEADS
    bt = _seqs_per_step(nb, n)
    rows = bt * n
    nc = rows // GLA_CHUNK
    col = lambda base: pl.BlockSpec((rows, LANE), lambda b, h: (b, base + h))
    in_specs = [col(24), col(32), col(40), col(48),
                pl.BlockSpec((rows, LANE), lambda b, h: (b, 72)),
                pl.BlockSpec((None, LANE, LANE), lambda b, h: (h, 0, 0)),
                pl.BlockSpec((None, LANE, LANE), lambda b, h: (h, 0, 0)),
                pl.BlockSpec((2, None, 1, LANE), lambda b, h: (0, h, 0, 0)),
                pl.BlockSpec((None, 1, LANE), lambda b, h: (h, 0, 0))]
    args = [proj, proj, proj, proj, proj, wf, wb, b_dec, norm_g]
    o_shape = jax.ShapeDtypeStruct((nb * n, hh * LANE), BF16)
    o_spec = pl.BlockSpec((rows, LANE), lambda b, h: (b, h))
    if init is not None:
        in_specs.append(pl.BlockSpec((bt, None, 2, None, GLA_DK, LANE),
                                     lambda b, h: (b, layer, 0, h, 0, 0)))
        args.append(init)
        out_shape, out_specs = o_shape, o_spec
    else:
        out_shape = (o_shape, jax.ShapeDtypeStruct((nb, 2, hh, GLA_DK, LANE), F32))
        out_specs = (o_spec, pl.BlockSpec((bt, 2, None, GLA_DK, LANE), lambda b, h: (b, 0, h, 0, 0)))
    return pl.pallas_call(
        functools.partial(_gla_kernel, has_init=init is not None, seq=n),
        grid=(nb // bt, hh),
        in_specs=in_specs,
        out_specs=out_specs,
        out_shape=out_shape,
        scratch_shapes=[pltpu.VMEM((rows, LANE), BF16),
                        pltpu.VMEM((2, rows, GLA_DK), BF16),
                        pltpu.VMEM((2, rows, GLA_DK), BF16),
                        pltpu.VMEM((2, rows, GLA_DK), F32),
                        pltpu.VMEM((2, rows, LANE), F32),
                        pltpu.VMEM((2, nc, LANE, GLA_DK), F32),
                        pltpu.VMEM((2, nc, LANE, GLA_DK), BF16)],
        compiler_params=_params(("arbitrary", "arbitrary"),
                                2 * 5 * rows * LANE * 4 + 2 * rows * LANE * 2 + rows * LANE * (2 + 8 + 16)
                                + nc * LANE * GLA_DK * 12 + 4 * 2**20),
    )(*args)


def _scan_levels(n):
    levels = []
    size = n
    while size > SCAN_TOP_ROWS:
        radix = 4 if not levels else 8
        levels.append((size, radix))
        size //= radix
    return levels, size


def _linear_scan(a, u, lv, n, h0, rev):
    levels, top = _scan_levels(n)

    def sweep(a_l, u_l, size, radix, body):
        m = size // radix
        sb = min(m, SCAN_SLAB_ROWS)
        first = radix - 1 if rev else 0
        order = list(range(radix - 2, -1, -1)) if rev else list(range(1, radix))

        def blk(j, carry):
            sl = lambda r: pl.ds(j * (sb * radix) + r, sb, stride=radix)
            body(sl, pl.ds(pl.multiple_of(j * sb, sb), sb), first, order)
            return carry

        if m // sb == 1:
            blk(0, 0)
        else:
            lax.fori_loop(0, m // sb, blk, 0)

    src_a, src_u = a, u
    for l, (size, radix) in enumerate(levels):
        nxt = lv[l]

        def up(sl, dense, first, order, src_a=src_a, src_u=src_u, nxt=nxt):
            pa = src_a[sl(first), :]
            pu = src_u[sl(first), :]
            for r in order:
                ar = src_a[sl(r), :]
                pu = ar * pu + src_u[sl(r), :]
                pa = ar * pa
                src_a[sl(r), :] = pa
                src_u[sl(r), :] = pu
            nxt[0, dense, :] = pa
            nxt[1, dense, :] = pu

        sweep(src_a, src_u, size, radix, up)
        src_a, src_u = nxt.at[0], nxt.at[1]

    carry = h0
    top_c = lv[len(levels) - 1].at[2]
    for g in (range(top - 1, -1, -1) if rev else range(top)):
        top_c[g:g + 1, :] = carry
        carry = src_a[g:g + 1, :] * carry + src_u[g:g + 1, :]

    for l in range(len(levels) - 1, -1, -1):
        size, radix = levels[l]
        a_l, u_l = (a, u) if l == 0 else (lv[l - 1].at[0], lv[l - 1].at[1])
        c_l = lv[l].at[2]
        c_below = None if l == 0 else lv[l - 1].at[2]

        def down(sl, dense, first, order, a_l=a_l, u_l=u_l, c_l=c_l, c_below=c_below):
            cin = c_l[dense, :]
            prev = cin
            for r in [first] + order:
                h = u_l[sl(r), :] + a_l[sl(r), :] * cin
                if c_below is None:
                    u_l[sl(r), :] = h
                else:
                    c_below[sl(r), :] = prev
                prev = h

        sweep(a_l, u_l, size, radix, down)
    return carry


def _lru_kernel(*refs, has_init, seq):
    if has_init:
        (x_ref, y_ref, cw_ref, cb_ref, wr_ref, br_ref, wi_ref, bi_ref, lam_ref, h0_ref,
         o_ref, xp_s, a_s, u_s, *lv_s) = refs
        hfin_ref = None
    else:
        (x_ref, y_ref, cw_ref, cb_ref, wr_ref, br_ref, wi_ref, bi_ref, lam_ref,
         o_ref, hfin_ref, xp_s, a_s, u_s, *lv_s) = refs
    n = x_ref.shape[0]
    nseq = n // seq
    pad = SUBLANE
    xp_s[0:pad, :] = jnp.zeros((pad, LANE), F32)
    xp_s[pad + n:pad + n + pad, :] = jnp.zeros((pad, LANE), F32)
    xp_s[pad:pad + n, :] = x_ref[...]
    t = lax.broadcasted_iota(jnp.int32, (n, 1), 0)
    xc = cb_ref[...]
    for j in range(LRU_CONV):
        off = j - CONV_LEFT
        tap = xp_s[pl.ds(pad + off, n), :]
        edge = [s * seq + e for s in range(1, nseq) for e in range(0, -off)] if off < 0 else \
               [s * seq - 1 - e for s in range(1, nseq) for e in range(0, off)]
        if edge:
            tap = jnp.where(functools.reduce(jnp.logical_or, [t == e for e in edge]), 0.0, tap)
        xc = xc + tap * cw_ref[j:j + 1, :]
    xcb = xc.astype(BF16)
    for d in range(2):
        r = _sigmoid(jnp.dot(xcb, wr_ref[d], preferred_element_type=F32) + br_ref[d])
        i = _sigmoid(jnp.dot(xcb, wi_ref[d], preferred_element_type=F32) + bi_ref[d])
        log_a = -LRU_C * r * _softplus(-lam_ref[d])
        a = jnp.exp(log_a)
        a_s[d] = a
        v = -jnp.tanh(log_a) * (a * a + 1.0)
        u_s[d] = jnp.where(v > 0.0, v * lax.rsqrt(v), 0.0) * (i * xc)

    for s in range(nseq):
        rows = pl.ds(s * seq, seq)
        for d in range(2):
            h0 = h0_ref[s, d] if has_init else jnp.zeros((1, LANE), F32)
            h_last = _linear_scan(a_s.at[d, rows], u_s.at[d, rows], [lv.at[d] for lv in lv_s], seq, h0,
                                  rev=d == 1)
            if hfin_ref is not None:
                hfin_ref[s, d] = h_last
    o_ref[...] = ((u_s[0] + u_s[1]) * jax.nn.gelu(y_ref[...])).astype(o_ref.dtype)


def _lru(proj, conv_w, conv_b, w_r, b_r, w_i, b_i, lam, nb, n, init=None, layer=0):
    kb = conv_w.shape[1] // LRU_BLOCK
    bt = _seqs_per_step(nb, n)
    rows = bt * n
    vec = lambda r: pl.BlockSpec((r, LANE), lambda b, k: (0, k))
    vec2 = pl.BlockSpec((2, 1, LANE), lambda b, k: (0, 0, k))
    wspec = pl.BlockSpec((2, None, LRU_BLOCK, LRU_BLOCK), lambda b, k: (0, k, 0, 0))
    in_specs = [pl.BlockSpec((rows, LANE), lambda b, k: (b, 56 + k)),
                pl.BlockSpec((rows, LANE), lambda b, k: (b, 64 + k)),
                vec(LRU_CONV), vec(1), wspec, vec2, wspec, vec2, vec2]
    args = [proj, proj, conv_w, conv_b, w_r, b_r, w_i, b_i, lam]
    o_shape = jax.ShapeDtypeStruct((nb * n, kb * LANE), BF16)
    o_spec = pl.BlockSpec((rows, LANE), lambda b, k: (b, k))
    if init is not None:
        in_specs.append(pl.BlockSpec((bt, None, 2, 1, LANE), lambda b, k: (b, layer, 0, 0, k)))
        args.append(init)
        out_shape, out_specs = o_shape, o_spec
    else:
        out_shape = (o_shape, jax.ShapeDtypeStruct((nb, 2, 1, kb * LANE), F32))
        out_specs = (o_spec, pl.BlockSpec((bt, 2, 1, LANE), lambda b, k: (b, 0, 0, k)))
    return pl.pallas_call(
        functools.partial(_lru_kernel, has_init=init is not None, seq=n),
        grid=(nb // bt, kb),
        in_specs=in_specs,
        out_specs=out_specs,
        out_shape=out_shape,
        scratch_shapes=[pltpu.VMEM((rows + 2 * SUBLANE, LANE), F32),
                        pltpu.VMEM((2, rows, LANE), F32), pltpu.VMEM((2, rows, LANE), F32)]
                       + [pltpu.VMEM((2, 3, size // radix, LANE), F32) for size, radix in _scan_levels(n)[0]],
        compiler_params=_params(("arbitrary", "arbitrary"), 2 * 2 * rows * LANE * 4 + 12 * rows * LANE * 4),
    )(*args)


def kernel(x_prompt, x_sample, cache_attn_k, cache_attn_v, state_gla, state_lru, c, c_ctx, w_ada, b_ada, norm1_g, w_in, attn_sink, gla_w_decay, gla_b_decay, gla_norm_g, lru_conv_w, lru_conv_b, lru_w_rgate, lru_b_rgate, lru_w_igate, lru_b_igate, lru_lambda, w_out, norm2_g, w_gu, w_down, final_norm_g):
    nb_c, s_len, d = x_prompt.shape
    nb_l, n_lat, _ = x_sample.shape
    depth = w_in.shape[0]
    lru_w = lru_conv_w.shape[2]
    qw, kw = ATT_HEADS * HEAD_DIM, ATT_KV_HEADS * HEAD_DIM
    gdec0 = qw + 2 * kw + 4 * GLA_HEADS * GLA_DK
    gdec1 = gdec0 + 2 * GLA_LOWRANK
    in_tile = 512
    in_pad = -(-w_in.shape[2] // in_tile) * in_tile

    w_in_b = jnp.concatenate(
        [w_in[:, :, :gdec0], w_in[:, :, gdec1:], w_in[:, :, gdec0:gdec1],
         jnp.zeros((depth, d, in_pad - w_in.shape[2]), w_in.dtype)], axis=2).astype(BF16)
    w_out_b = w_out.astype(BF16)
    w_gu_b = w_gu.astype(BF16)
    w_down_b = w_down.astype(BF16)

    cvecs = jnp.concatenate([c_ctx[None], c, jnp.zeros((MOD_ROWS - 1 - nb_l, d), F32)], axis=0)
    modr = _ada(cvecs, w_ada, b_ada).reshape(depth, MOD_ROWS, 6, 1, d)
    row_ctx = _ModRows(0, nb_c * s_len)
    row_lat = _ModRows(1, n_lat)

    tables = _rope_tables(n_lat)
    cache_k = cache_attn_k.reshape(nb_l, depth, cache_attn_k.shape[2], kw)
    cache_v = cache_attn_v.reshape(nb_l, depth, cache_attn_v.shape[2], kw)
    lru_init = state_lru.reshape(nb_l, depth, 2, 1, lru_w)

    xc = x_prompt.reshape(nb_c * s_len, d)
    xl = x_sample.reshape(nb_l * n_lat, d)
    new_k, new_v, new_gla, new_lru = [], [], [], []
    for l in range(depth):
        wdec = gla_w_decay[l].reshape(2, GLA_LOWRANK, GLA_HEADS, GLA_DK).transpose(0, 2, 1, 3)
        zpad = jnp.zeros((GLA_HEADS, GLA_LOWRANK, GLA_DK), F32)
        zrest = jnp.zeros((GLA_HEADS, LANE - 2 * GLA_LOWRANK, GLA_DK), F32)
        wf = jnp.concatenate([wdec[0], zpad, zrest], axis=1).astype(BF16)
        wb = jnp.concatenate([zpad, wdec[1], zrest], axis=1).astype(BF16)
        b_dec = gla_b_decay[l].reshape(2, GLA_HEADS, 1, GLA_DK)
        ng = gla_norm_g[l].reshape(GLA_HEADS, 1, LANE)
        lru_args = (lru_conv_w[l], lru_conv_b[l].reshape(1, lru_w),
                    lru_w_rgate[l].astype(BF16), lru_b_rgate[l].reshape(2, 1, lru_w),
                    lru_w_igate[l].astype(BF16), lru_b_igate[l].reshape(2, 1, lru_w),
                    lru_lambda[l].reshape(2, 1, lru_w))
        sink = attn_sink[l]

        def ffn(x, mix, row_of):
            x = _out_proj(*mix, w_out_b, x, modr, l, row_of, 2)
            h2 = _norm_mod(x, norm2_g[l], modr, l, row_of, 4, 3)
            act = _gate_up(h2, w_gu_b, l)
            return _down_proj(act, w_down_b, x, modr, l, row_of, 5)

        h = _norm_mod(xc, norm1_g[l], modr, l, row_ctx, 1, 0)
        proj = _in_proj(h, w_in_b, l)
        attn = _attn_ctx(proj, sink, nb_c, s_len)
        gla, gla_fin = _gla(proj, wf, wb, b_dec, ng, nb_c, s_len)
        lru, lru_fin = _lru(proj, *lru_args, nb_c, s_len)
        new_k.append(proj[:, qw:qw + kw].reshape(nb_c, s_len, ATT_KV_HEADS, HEAD_DIM))
        new_v.append(proj[:, qw + kw:qw + 2 * kw].reshape(nb_c, s_len, ATT_KV_HEADS, HEAD_DIM))
        new_gla.append(gla_fin)
        new_lru.append(lru_fin.reshape(nb_c, 2, lru_w))
        xc = ffn(xc, (attn, gla, lru), row_ctx)

        h = _norm_mod(xl, norm1_g[l], modr, l, row_lat, 1, 0)
        proj = _in_proj(h, w_in_b, l)
        attn = _attn_lat(proj, sink, cache_k, cache_v, l, tables, nb_l, n_lat)
        gla = _gla(proj, wf, wb, b_dec, ng, nb_l, n_lat, init=state_gla, layer=l)
        lru = _lru(proj, *lru_args, nb_l, n_lat, init=lru_init, layer=l)
        xl = ffn(xl, (attn, gla, lru), row_lat)

    y_prompt = _final_norm(xc, final_norm_g).reshape(nb_c, s_len, d)
    y_sample = _final_norm(xl, final_norm_g).reshape(nb_l, n_lat, d)
    return (y_prompt, y_sample, jnp.stack(new_k, axis=1), jnp.stack(new_v, axis=1),
            jnp.stack(new_gla, axis=1), jnp.stack(new_lru, axis=1))
```

```python
import functools
from typing import NamedTuple

import jax
import jax.numpy as jnp
from jax import lax
from jax.experimental import pallas as pl
from jax.experimental.pallas import tpu as pltpu

F32 = jnp.float32
BF16 = jnp.bfloat16

EPS = 1e-6
HEAD_DIM = 128
ATT_HEADS = 16
ATT_KV_HEADS = 4
ATT_GROUPS = ATT_HEADS // ATT_KV_HEADS
WINDOW = 128
Q_BLOCK = 128
GRID_W = 64
ROPE_THETA = 10000.0
ROPE_FREQS = HEAD_DIM // 4
GLA_HEADS = 8
GLA_DK = 128
GLA_LOWRANK = 16
GLA_TAU = 16.0
GLA_CHUNK = 64
GLA_BULK_ROWS = 256
GLA_ELEMENTWISE_ROWS = 512
LRU_BLOCK = 128
LRU_CONV = 4
CONV_LEFT = 2
LRU_C = 8.0
MIXER_ROWS_PER_STEP = 1024
SCAN_TOP_ROWS = 8
SCAN_SLAB_ROWS = 64

LANE = 128
SUBLANE = 8
VMEM_BYTES_V7X = 64 * 2**20
MXU_COLS = 256
GATE_UP_SLAB_ROWS = 1024
MOD_ROWS = 16

NT_DIMS = (((1,), (1,)), ((), ()))
TN_DIMS = (((0,), (0,)), ((), ()))


def _params(sem, est_bytes):
    limit = int(min(VMEM_BYTES_V7X - 4 * 2**20, max(est_bytes + 8 * 2**20, 32 * 2**20)))
    return pltpu.CompilerParams(dimension_semantics=sem, vmem_limit_bytes=limit)


def _seqs_per_step(nb, n):
    bt = max(1, MIXER_ROWS_PER_STEP // n)
    while nb % bt:
        bt -= 1
    return bt


def _softplus(x):
    return jnp.maximum(x, 0.0) + jnp.log1p(jnp.exp(-jnp.abs(x)))


def _log_sigmoid(x):
    return -_softplus(-x)


def _sigmoid(x):
    return 0.5 * jnp.tanh(0.5 * x) + 0.5


def _ada_kernel(c_ref, w_ref, b_ref, o_ref):
    cv = c_ref[...]
    s = (cv * jax.nn.sigmoid(cv)).astype(BF16)
    o_ref[...] = jnp.dot(s, w_ref[...].astype(BF16), preferred_element_type=F32) + b_ref[...]


def _ada(cvecs, w_ada, b_ada):
    nl, d, n6 = w_ada.shape
    tn = 512
    return pl.pallas_call(
        _ada_kernel,
        grid=(nl, n6 // tn),
        in_specs=[pl.BlockSpec((MOD_ROWS, d), lambda l, j: (0, 0)),
                  pl.BlockSpec((None, d, tn), lambda l, j: (l, 0, j)),
                  pl.BlockSpec((None, 1, tn), lambda l, j: (l, 0, j))],
        out_specs=pl.BlockSpec((None, MOD_ROWS, tn), lambda l, j: (l, 0, j)),
        out_shape=jax.ShapeDtypeStruct((nl, MOD_ROWS, n6), F32),
        compiler_params=_params(("arbitrary", "arbitrary"), 2 * d * tn * 4 + d * tn * 2),
    )(cvecs, w_ada, b_ada.reshape(nl, 1, n6))


def _norm_mod_kernel(x_ref, g_ref, sc_ref, sh_ref, o_ref):
    x = x_ref[...]
    y = x * lax.rsqrt(jnp.mean(x * x, axis=-1, keepdims=True) + EPS) * g_ref[...]
    o_ref[...] = (y * (1.0 + sc_ref[...]) + sh_ref[...]).astype(o_ref.dtype)


def _norm_kernel(x_ref, g_ref, o_ref):
    x = x_ref[...]
    y = x * lax.rsqrt(jnp.mean(x * x, axis=-1, keepdims=True) + EPS) * g_ref[...]
    o_ref[...] = y.astype(o_ref.dtype)


class _ModRows(NamedTuple):
    first: int
    segment: int

    def tile(self, m, want):
        tm = min(want, m, self.segment)
        assert m % tm == 0 and self.segment % tm == 0
        return tm

    def __call__(self, row0):
        return self.first + row0 // self.segment


def _norm_mod(x, g, modr, layer, row_of, sc_idx, sh_idx, tr=256):
    m, d = x.shape
    tr = row_of.tile(m, tr)
    mspec = lambda k: pl.BlockSpec((None, None, None, 1, d), lambda i: (layer, row_of(i * tr), k, 0, 0))
    return pl.pallas_call(
        _norm_mod_kernel,
        grid=(m // tr,),
        in_specs=[pl.BlockSpec((tr, d), lambda i: (i, 0)),
                  pl.BlockSpec((1, d), lambda i: (0, 0)),
                  mspec(sc_idx), mspec(sh_idx)],
        out_specs=pl.BlockSpec((tr, d), lambda i: (i, 0)),
        out_shape=jax.ShapeDtypeStruct((m, d), BF16),
        compiler_params=_params(("arbitrary",), 2 * tr * d * 6 + 2 * tr * d * 4),
    )(x, g.reshape(1, d), modr, modr)


def _final_norm(x, g, tr=256):
    m, d = x.shape
    return pl.pallas_call(
        _norm_kernel,
        grid=(m // tr,),
        in_specs=[pl.BlockSpec((tr, d), lambda i: (i, 0)),
                  pl.BlockSpec((1, d), lambda i: (0, 0))],
        out_specs=pl.BlockSpec((tr, d), lambda i: (i, 0)),
        out_shape=jax.ShapeDtypeStruct((m, d), F32),
        compiler_params=_params(("arbitrary",), 2 * tr * d * 8 + 2 * tr * d * 4),
    )(x, g.reshape(1, d))


def _mm_kernel(x_ref, w_ref, o_ref):
    o_ref[...] = jnp.dot(x_ref[...], w_ref[...], preferred_element_type=F32)


def _in_proj(h, w, layer, n, tm=2048, tn=512):
    m, k = h.shape
    assert n % tn == 0 and n <= w.shape[2]
    tm = min(tm, m)
    return pl.pallas_call(
        _mm_kernel,
        grid=(m // tm, n // tn),
        in_specs=[pl.BlockSpec((tm, k), lambda i, j: (i, 0)),
                  pl.BlockSpec((None, k, tn), lambda i, j: (layer, 0, j))],
        out_specs=pl.BlockSpec((tm, tn), lambda i, j: (i, j)),
        out_shape=jax.ShapeDtypeStruct((m, n), F32),
        compiler_params=_params(("arbitrary", "arbitrary"), 2 * (tm * k * 2 + k * tn * 2 + tm * tn * 4)),
    )(h, w)


def _out_proj_kernel(a_ref, b_ref, c_ref, w_ref, x_ref, g_ref, o_ref):
    ka, kb = a_ref.shape[1], b_ref.shape[1]
    for j0 in range(0, o_ref.shape[1], MXU_COLS):
        cols = slice(j0, j0 + MXU_COLS)
        acc = jnp.dot(a_ref[...], w_ref[0:ka, cols], preferred_element_type=F32)
        acc += jnp.dot(b_ref[...], w_ref[ka:ka + kb, cols], preferred_element_type=F32)
        acc += jnp.dot(c_ref[...], w_ref[ka + kb:, cols], preferred_element_type=F32)
        o_ref[:, cols] = x_ref[:, cols] + g_ref[:, cols] * acc


def _out_proj(attn, gla, lru, w, x, modr, layer, row_of, gate_idx, tm=1024, tn=1024):
    m, d = x.shape
    k = w.shape[1]
    tm = row_of.tile(m, tm)
    xs = lambda a: pl.BlockSpec((tm, a.shape[1]), lambda i, j: (i, 0))
    return pl.pallas_call(
        _out_proj_kernel,
        grid=(m // tm, d // tn),
        in_specs=[xs(attn), xs(gla), xs(lru),
                  pl.BlockSpec((None, k, tn), lambda i, j: (layer, 0, j)),
                  pl.BlockSpec((tm, tn), lambda i, j: (i, j)),
                  pl.BlockSpec((None, None, None, 1, tn),
                               lambda i, j: (layer, row_of(i * tm), gate_idx, 0, j))],
        out_specs=pl.BlockSpec((tm, tn), lambda i, j: (i, j)),
        out_shape=jax.ShapeDtypeStruct((m, d), F32),
        compiler_params=_params(("arbitrary", "arbitrary"),
                                2 * (tm * k * 2 + k * tn * 2 + 2 * tm * tn * 4) + tm * tn * 4),
    )(attn, gla, lru, w, x, modr)


def _gate_up_kernel(h_ref, wg_ref, wu_ref, o_ref):
    tm = h_ref.shape[0]
    slab = min(tm, GATE_UP_SLAB_ROWS)
    for r0 in range(0, tm, slab):
        h = h_ref[r0:r0 + slab, :]
        a = jnp.dot(h, wg_ref[...], preferred_element_type=F32)
        u = jnp.dot(h, wu_ref[...], preferred_element_type=F32)
        o_ref[r0:r0 + slab, :] = (a * _sigmoid(a) * u).astype(o_ref.dtype)


def _gate_up(h, w, layer, tm=2048, tn=256):
    m, k = h.shape
    f = w.shape[2] // 2
    tm = min(tm, m)
    nj = f // tn
    return pl.pallas_call(
        _gate_up_kernel,
        grid=(m // tm, nj),
        in_specs=[pl.BlockSpec((tm, k), lambda i, j: (i, 0)),
                  pl.BlockSpec((None, k, tn), lambda i, j: (layer, 0, j)),
                  pl.BlockSpec((None, k, tn), lambda i, j: (layer, 0, nj + j))],
        out_specs=pl.BlockSpec((tm, tn), lambda i, j: (i, j)),
        out_shape=jax.ShapeDtypeStruct((m, f), BF16),
        compiler_params=_params(("arbitrary", "arbitrary"),
                                2 * (tm * k * 2 + 2 * k * tn * 2 + tm * tn * 2) + 3 * tm * tn * 4),
    )(h, w, w)


def _down_kernel(a_ref, w_ref, x_ref, g_ref, o_ref):
    for j0 in range(0, o_ref.shape[1], MXU_COLS):
        cols = slice(j0, j0 + MXU_COLS)
        acc = jnp.dot(a_ref[...], w_ref[:, cols], preferred_element_type=F32)
        o_ref[:, cols] = x_ref[:, cols] + g_ref[:, cols] * acc


def _down_proj(act, w, x, modr, layer, row_of, gate_idx, tm=512, tn=512):
    m, d = x.shape
    k = w.shape[1]
    tm = row_of.tile(m, tm)
    return pl.pallas_call(
        _down_kernel,
        grid=(m // tm, d // tn),
        in_specs=[pl.BlockSpec((tm, k), lambda i, j: (i, 0)),
                  pl.BlockSpec((None, k, tn), lambda i, j: (layer, 0, j)),
                  pl.BlockSpec((tm, tn), lambda i, j: (i, j)),
                  pl.BlockSpec((None, None, None, 1, tn),
                               lambda i, j: (layer, row_of(i * tm), gate_idx, 0, j))],
        out_specs=pl.BlockSpec((tm, tn), lambda i, j: (i, j)),
        out_shape=jax.ShapeDtypeStruct((m, d), F32),
        compiler_params=_params(("arbitrary", "arbitrary"),
                                2 * (tm * k * 2 + k * tn * 2 + 2 * tm * tn * 4) + tm * tn * 4),
    )(act, w, x, modr)


def _sink_column(sink_ref, kh, rows):
    return jnp.concatenate(
        [jnp.full((rows, 1), sink_ref[kh * ATT_GROUPS + g], F32) for g in range(ATT_GROUPS)], axis=0)


def _attn_ctx_kernel(sink_ref, q_ref, k_ref, v_ref, o_ref):
    s_len = q_ref.shape[0]
    scale = HEAD_DIM ** -0.5
    for kh in range(ATT_KV_HEADS):
        kk = k_ref[:, kh * HEAD_DIM:(kh + 1) * HEAD_DIM].astype(BF16)
        vv = v_ref[:, kh * HEAD_DIM:(kh + 1) * HEAD_DIM].astype(BF16)
        h0 = kh * ATT_GROUPS
        q = jnp.concatenate([q_ref[:, (h0 + g) * HEAD_DIM:(h0 + g + 1) * HEAD_DIM]
                             for g in range(ATT_GROUPS)], axis=0).astype(BF16)
        s = lax.dot_general(q, kk, NT_DIMS, preferred_element_type=F32) * scale
        sk = _sink_column(sink_ref, kh, s_len)
        m = jnp.maximum(jnp.max(s, axis=-1, keepdims=True), sk)
        p = jnp.exp(s - m)
        denom = jnp.sum(p, axis=-1, keepdims=True) + jnp.exp(sk - m)
        pn = (p * (1.0 / denom)).astype(BF16)
        o = jnp.dot(pn, vv, preferred_element_type=F32)
        for g in range(ATT_GROUPS):
            o_ref[:, (h0 + g) * HEAD_DIM:(h0 + g + 1) * HEAD_DIM] = (
                o[g * s_len:(g + 1) * s_len].astype(o_ref.dtype))


def _attn_ctx(proj, sink, nb, s_len):
    qw = ATT_HEADS * HEAD_DIM
    kw = ATT_KV_HEADS * HEAD_DIM
    return pl.pallas_call(
        _attn_ctx_kernel,
        grid=(nb,),
        in_specs=[pl.BlockSpec(memory_space=pltpu.SMEM),
                  pl.BlockSpec((s_len, qw), lambda b: (b, 0)),
                  pl.BlockSpec((s_len, kw), lambda b: (b, qw // kw)),
                  pl.BlockSpec((s_len, kw), lambda b: (b, qw // kw + 1))],
        out_specs=pl.BlockSpec((s_len, qw), lambda b: (b, 0)),
        out_shape=jax.ShapeDtypeStruct((nb * s_len, qw), BF16),
        compiler_params=_params(("arbitrary",), 2 * s_len * (qw + 2 * kw) * 4 + 2 * s_len * qw * 2
                                + 8 * ATT_GROUPS * s_len * s_len * 4),
    )(sink, proj, proj, proj)


def _attn_lat_kernel(sink_ref, q_ref, k_ref, v_ref, kc_ref, vc_ref, cos_ref, sa_ref, sb_ref,
                     o_ref, kt_s, vb_s, qr_s):
    n = q_ref.shape[0]
    nqb = n // Q_BLOCK
    kh = pl.program_id(1)
    scale = HEAD_DIM ** -0.5
    wblocks = (Q_BLOCK + 2 * WINDOW) // Q_BLOCK

    def rope(x, rows):
        return (x * cos_ref[rows, :] + pltpu.roll(x, HEAD_DIM - ROPE_FREQS, 1) * sa_ref[rows, :]
                + pltpu.roll(x, ROPE_FREQS, 1) * sb_ref[rows, :])

    def prep(j, carry):
        rows = pl.ds(pl.multiple_of(j * Q_BLOCK, Q_BLOCK), Q_BLOCK)
        kt_s[j] = rope(k_ref[rows, :], rows).T.astype(BF16)
        for g in range(ATT_GROUPS):
            lanes = slice(g * HEAD_DIM, (g + 1) * HEAD_DIM)
            qr_s[rows, lanes] = rope(q_ref[rows, lanes], rows).astype(BF16)
        return carry

    lax.fori_loop(0, nqb, prep, 0)
    vb_s[...] = v_ref[...].astype(BF16)
    kct = kc_ref[...].T.astype(BF16)
    vc = vc_ref[...].astype(BF16)
    sk = jnp.concatenate(
        [jnp.full((Q_BLOCK, 1), sink_ref[kh * ATT_GROUPS + g], F32) for g in range(ATT_GROUPS)], axis=0)
    qi = lax.broadcasted_iota(jnp.int32, (Q_BLOCK, Q_BLOCK), 0)
    ki = lax.broadcasted_iota(jnp.int32, (Q_BLOCK, Q_BLOCK), 1)
    rel = qi - ki

    def block(i, carry):
        qs = pl.multiple_of(i * Q_BLOCK, Q_BLOCK)
        kb = jnp.clip(i - WINDOW // Q_BLOCK, 0, nqb - wblocks)
        qr = jnp.concatenate([qr_s[pl.ds(qs, Q_BLOCK), g * HEAD_DIM:(g + 1) * HEAD_DIM]
                              for g in range(ATT_GROUPS)], axis=0)
        s_parts = []
        for j in range(wblocks):
            off = (i - kb - j) * Q_BLOCK
            bias = jnp.where(jnp.abs(rel + off) <= WINDOW, 0.0, -jnp.inf).astype(F32)
            s_parts.append(jnp.dot(qr, kt_s[kb + j], preferred_element_type=F32) * scale
                           + jnp.concatenate([bias] * ATT_GROUPS, axis=0))
        s2 = jnp.dot(qr, kct, preferred_element_type=F32) * scale
        s_parts += [s2[:, j * LANE:(j + 1) * LANE] for j in range(s2.shape[1] // LANE)]
        m = functools.reduce(jnp.maximum, s_parts)
        m = jnp.maximum(jnp.max(m, axis=-1, keepdims=True), sk)
        p_parts = [jnp.exp(s - m) for s in s_parts]
        denom = jnp.sum(functools.reduce(jnp.add, p_parts), axis=-1, keepdims=True) + jnp.exp(sk - m)
        r = 1.0 / denom
        o = None
        for j, p in enumerate(p_parts):
            if j < wblocks:
                vw = vb_s[pl.ds(pl.multiple_of((kb + j) * Q_BLOCK, Q_BLOCK), Q_BLOCK), :]
            else:
                vw = vc[(j - wblocks) * LANE:(j - wblocks + 1) * LANE, :]
            ov = jnp.dot((p * r).astype(BF16), vw, preferred_element_type=F32)
            o = ov if o is None else o + ov
        for g in range(ATT_GROUPS):
            o_ref[pl.ds(qs, Q_BLOCK), g * HEAD_DIM:(g + 1) * HEAD_DIM] = (
                o[g * Q_BLOCK:(g + 1) * Q_BLOCK].astype(o_ref.dtype))
        return carry

    lax.fori_loop(0, nqb, block, 0, unroll=2)


def _attn_lat(proj, sink, cache_k, cache_v, layer, tables, nb, n):
    qw = ATT_HEADS * HEAD_DIM
    gw = ATT_GROUPS * HEAD_DIM
    past = cache_k.shape[2]
    kcol = qw // HEAD_DIM
    vcol = kcol + ATT_KV_HEADS
    cspec = pl.BlockSpec((None, None, past, HEAD_DIM), lambda b, h: (b, layer, 0, h))
    tspec = pl.BlockSpec((n, HEAD_DIM), lambda b, h: (0, 0))
    return pl.pallas_call(
        _attn_lat_kernel,
        grid=(nb, ATT_KV_HEADS),
        in_specs=[pl.BlockSpec(memory_space=pltpu.SMEM),
                  pl.BlockSpec((n, gw), lambda b, h: (b, h)),
                  pl.BlockSpec((n, HEAD_DIM), lambda b, h: (b, kcol + h)),
                  pl.BlockSpec((n, HEAD_DIM), lambda b, h: (b, vcol + h)),
                  cspec, cspec, tspec, tspec, tspec],
        out_specs=pl.BlockSpec((n, gw), lambda b, h: (b, h)),
        out_shape=jax.ShapeDtypeStruct((nb * n, qw), BF16),
        scratch_shapes=[pltpu.VMEM((n // Q_BLOCK, HEAD_DIM, Q_BLOCK), BF16),
                        pltpu.VMEM((n, HEAD_DIM), BF16), pltpu.VMEM((n, gw), BF16)],
        compiler_params=_params(("arbitrary", "arbitrary"),
                                2 * n * (gw + 2 * HEAD_DIM) * 4 + 2 * n * gw * 2 + 6 * n * HEAD_DIM * 4
                                + 4 * n * HEAD_DIM + n * gw * 2),
    )(sink, proj, proj, proj, cache_k, cache_v, *tables)


def _rope_tables(n):
    pos = jnp.arange(n)
    row = (pos // GRID_W).astype(F32)
    col = (pos % GRID_W).astype(F32)
    inv = ROPE_THETA ** (-jnp.arange(ROPE_FREQS, dtype=F32) / ROPE_FREQS)
    ang_r = row[:, None] * inv[None, :]
    ang_c = col[:, None] * inv[None, :]
    cr, sr, cc, sc = jnp.cos(ang_r), jnp.sin(ang_r), jnp.cos(ang_c), jnp.sin(ang_c)
    z = jnp.zeros_like(cr)
    cos = jnp.concatenate([cr, cr, cc, cc], axis=1)
    sa = jnp.concatenate([-sr, z, -sc, z], axis=1)
    sb = jnp.concatenate([z, sr, z, sc], axis=1)
    return cos, sa, sb


def _gla_kernel(*refs, has_init, seq):
    if has_init:
        (q_ref, k_ref, v_ref, g_ref, dec_ref, wf_ref, wb_ref, bd_ref, ng_ref, s0_ref,
         o_ref, vb_s, la_s, qin_s, kin_s, kdec_s, tot_s, o_s, kv_s, sin_s) = refs
        sfin_ref = None
    else:
        (q_ref, k_ref, v_ref, g_ref, dec_ref, wf_ref, wb_ref, bd_ref, ng_ref,
         o_ref, sfin_ref, vb_s, la_s, qin_s, kin_s, kdec_s, tot_s, o_s, kv_s, sin_s) = refs
    n = q_ref.shape[0]
    c = GLA_CHUNK
    nc = n // c
    blk = min(seq, GLA_BULK_ROWS)
    qscale = GLA_DK ** -0.5
    shift = c.bit_length() - 1
    ri = lax.broadcasted_iota(jnp.int32, (blk, blk), 0)
    ci = lax.broadcasted_iota(jnp.int32, (blk, blk), 1)
    same = lax.shift_right_logical(ri, shift) == lax.shift_right_logical(ci, shift)
    keep = (same & (ri >= ci), same & (ri <= ci))
    sums = tuple(jnp.concatenate([keep[d].astype(BF16), same.astype(BF16)], axis=0) for d in range(2))
    wdec = jnp.concatenate([wf_ref[...], wb_ref[...]], axis=1)
    bdec = jnp.concatenate([bd_ref[0], bd_ref[1]], axis=1)
    ew = min(n, GLA_ELEMENTWISE_ROWS)

    def log_decays(i, carry):
        rows = pl.ds(pl.multiple_of(i * ew, ew), ew)
        z = jnp.dot(dec_ref[rows, :].astype(BF16), wdec, preferred_element_type=F32) + bdec
        la = _log_sigmoid(z) * (1.0 / GLA_TAU)
        hi = la.astype(BF16)
        r1 = la - hi.astype(F32)
        mid = r1.astype(BF16)
        la_s[0, rows, :] = hi
        la_s[1, rows, :] = mid
        la_s[2, rows, :] = (r1 - mid.astype(F32)).astype(BF16)
        return carry

    lax.fori_loop(0, n // ew, log_decays, 0)

    def decay_sums(i, carry):
        rows = pl.ds(pl.multiple_of(i * blk, blk), blk)
        for d in range(2):
            lanes = slice(d * GLA_DK, (d + 1) * GLA_DK)
            parts = jnp.concatenate([la_s[t, rows, lanes] for t in range(3)], axis=1)
            y = jnp.dot(sums[d], parts, preferred_element_type=F32)
            y = (y[:, :LANE] + y[:, LANE:2 * LANE]) + y[:, 2 * LANE:]
            o_s[d, rows, :] = y[:blk]
            tot_s[d, rows, :] = y[blk:]
        return carry

    lax.fori_loop(0, n // blk, decay_sums, 0, unroll=min(n // blk, 2))

    def decayed(i, carry):
        rows = pl.ds(pl.multiple_of(i * ew, ew), ew)
        qc = q_ref[rows, :] * qscale
        kc = k_ref[rows, :]
        vb_s[rows, :] = v_ref[rows, :].astype(BF16)
        for d in range(2):
            cum = o_s[d, rows, :]
            qin_s[d, rows, :] = (qc * jnp.exp(cum)).astype(BF16)
            kin_s[d, rows, :] = (kc * jnp.exp(-cum)).astype(BF16)
            kdec_s[d, rows, :] = (kc * jnp.exp(tot_s[d, rows, :] - cum)).astype(BF16)
        return carry

    lax.fori_loop(0, n // ew, decayed, 0)

    def within(i, carry):
        rows = pl.ds(pl.multiple_of(i * blk, blk), blk)
        vc = vb_s[rows, :]
        for d in range(2):
            att = jnp.where(keep[d], lax.dot_general(qin_s[d, rows, :], kin_s[d, rows, :], NT_DIMS,
                                                     preferred_element_type=F32), 0.0)
            o_s[d, rows, :] = jnp.dot(att.astype(BF16), vc, preferred_element_type=F32)
        return carry

    lax.fori_loop(0, n // blk, within, 0, unroll=min(n // blk, 2))

    def increments(j, carry):
        rows = pl.ds(pl.multiple_of(j * c, c), c)
        vc = vb_s[rows, :]
        for d in range(2):
            kv_s[d, j] = lax.dot_general(vc, kdec_s[d, rows, :], TN_DIMS, preferred_element_type=F32)
        return carry

    lax.fori_loop(0, nc, increments, 0, unroll=min(nc, 4))

    ncs = seq // c
    for s in range(n // seq):
        def scan(j, carry, base=s * ncs):
            sf, sb = carry
            jf = base + j
            jb = base + ncs - 1 - j
            sin_s[0, jf] = sf.astype(BF16)
            sin_s[1, jb] = sb.astype(BF16)
            sf = jnp.exp(tot_s[0, pl.ds(jf * c, 1), :]) * sf + kv_s[0, jf]
            sb = jnp.exp(tot_s[1, pl.ds(jb * c, 1), :]) * sb + kv_s[1, jb]
            return sf, sb

        if has_init:
            init = (s0_ref[s, 0].T, s0_ref[s, 1].T)
        else:
            init = (jnp.zeros((LANE, GLA_DK), F32), jnp.zeros((LANE, GLA_DK), F32))
        sf, sb = lax.fori_loop(0, ncs, scan, init)
        if sfin_ref is not None:
            sfin_ref[s, 0] = sf.T
            sfin_ref[s, 1] = sb.T

    def finish(j, carry):
        rows = pl.ds(pl.multiple_of(j * c, c), c)
        o = ((o_s[0, rows, :] + lax.dot_general(qin_s[0, rows, :], sin_s[0, j], NT_DIMS,
                                                preferred_element_type=F32))
             + (o_s[1, rows, :] + lax.dot_general(qin_s[1, rows, :], sin_s[1, j], NT_DIMS,
                                                  preferred_element_type=F32)))
        o = o * lax.rsqrt(jnp.mean(o * o, axis=-1, keepdims=True) + EPS) * ng_ref[...]
        g = g_ref[rows, :]
        o_ref[rows, :] = (o * (g * _sigmoid(g))).astype(o_ref.dtype)
        return carry

    lax.fori_loop(0, nc, finish, 0, unroll=min(nc, 4))


def _gla(proj, proj_tail, dec_col, wf, wb, b_dec, norm_g, nb, n, init=None, layer=0):
    hh = GLA_HEADS
    q0 = (ATT_HEADS + 2 * ATT_KV_HEADS) * HEAD_DIM // LANE
    bt = _seqs_per_step(nb, n)
    rows = bt * n
    nc = rows // GLA_CHUNK
    col = lambda base: pl.BlockSpec((rows, LANE), lambda b, h: (b, base + h))
    in_specs = [col(q0), col(q0 + hh), col(q0 + 2 * hh), col(q0 + 3 * hh),
                pl.BlockSpec((rows, LANE), lambda b, h: (b, dec_col)),
                pl.BlockSpec((None, LANE, LANE), lambda b, h: (h, 0, 0)),
                pl.BlockSpec((None, LANE, LANE), lambda b, h: (h, 0, 0)),
                pl.BlockSpec((2, None, 1, LANE), lambda b, h: (0, h, 0, 0)),
                pl.BlockSpec((None, 1, LANE), lambda b, h: (h, 0, 0))]
    args = [proj, proj, proj, proj, proj_tail, wf, wb, b_dec, norm_g]
    o_shape = jax.ShapeDtypeStruct((nb * n, hh * LANE), BF16)
    o_spec = pl.BlockSpec((rows, LANE), lambda b, h: (b, h))
    if init is not None:
        in_specs.append(pl.BlockSpec((bt, None, 2, None, GLA_DK, LANE),
                                     lambda b, h: (b, layer, 0, h, 0, 0)))
        args.append(init)
        out_shape, out_specs = o_shape, o_spec
    else:
        out_shape = (o_shape, jax.ShapeDtypeStruct((nb, 2, hh, GLA_DK, LANE), F32))
        out_specs = (o_spec, pl.BlockSpec((bt, 2, None, GLA_DK, LANE), lambda b, h: (b, 0, h, 0, 0)))
    return pl.pallas_call(
        functools.partial(_gla_kernel, has_init=init is not None, seq=n),
        grid=(nb // bt, hh),
        in_specs=in_specs,
        out_specs=out_specs,
        out_shape=out_shape,
        scratch_shapes=[pltpu.VMEM((rows, LANE), BF16),
                        pltpu.VMEM((3, rows, 2 * GLA_DK), BF16),
                        pltpu.VMEM((2, rows, GLA_DK), BF16),
                        pltpu.VMEM((2, rows, GLA_DK), BF16),
                        pltpu.VMEM((2, rows, GLA_DK), BF16),
                        pltpu.VMEM((2, rows, GLA_DK), F32),
                        pltpu.VMEM((2, rows, LANE), F32),
                        pltpu.VMEM((2, nc, LANE, GLA_DK), F32),
                        pltpu.VMEM((2, nc, LANE, GLA_DK), BF16)],
        compiler_params=_params(("arbitrary", "arbitrary"),
                                2 * 5 * rows * LANE * 4 + 2 * rows * LANE * 2 + rows * LANE * (2 + 8 + 16)
                                + nc * LANE * GLA_DK * 12 + 4 * 2**20),
    )(*args)


def _scan_levels(n):
    levels = []
    size = n
    while size > SCAN_TOP_ROWS:
        radix = 4 if not levels else 8
        levels.append((size, radix))
        size //= radix
    return levels, size


def _linear_scan(a, u, lv, n, h0, rev):
    levels, top = _scan_levels(n)

    def sweep(a_l, u_l, size, radix, body):
        m = size // radix
        sb = min(m, SCAN_SLAB_ROWS)
        first = radix - 1 if rev else 0
        order = list(range(radix - 2, -1, -1)) if rev else list(range(1, radix))

        def blk(j, carry):
            sl = lambda r: pl.ds(j * (sb * radix) + r, sb, stride=radix)
            body(sl, pl.ds(pl.multiple_of(j * sb, sb), sb), first, order)
            return carry

        if m // sb == 1:
            blk(0, 0)
        else:
            lax.fori_loop(0, m // sb, blk, 0)

    src_a, src_u = a, u
    for l, (size, radix) in enumerate(levels):
        nxt = lv[l]

        def up(sl, dense, first, order, src_a=src_a, src_u=src_u, nxt=nxt):
            pa = src_a[sl(first), :]
            pu = src_u[sl(first), :]
            for r in order:
                ar = src_a[sl(r), :]
                pu = ar * pu + src_u[sl(r), :]
                pa = ar * pa
                src_a[sl(r), :] = pa
                src_u[sl(r), :] = pu
            nxt[0, dense, :] = pa
            nxt[1, dense, :] = pu

        sweep(src_a, src_u, size, radix, up)
        src_a, src_u = nxt.at[0], nxt.at[1]

    carry = h0
    top_c = lv[len(levels) - 1].at[2]
    for g in (range(top - 1, -1, -1) if rev else range(top)):
        top_c[g:g + 1, :] = carry
        carry = src_a[g:g + 1, :] * carry + src_u[g:g + 1, :]

    for l in range(len(levels) - 1, -1, -1):
        size, radix = levels[l]
        a_l, u_l = (a, u) if l == 0 else (lv[l - 1].at[0], lv[l - 1].at[1])
        c_l = lv[l].at[2]
        c_below = None if l == 0 else lv[l - 1].at[2]

        def down(sl, dense, first, order, a_l=a_l, u_l=u_l, c_l=c_l, c_below=c_below):
            cin = c_l[dense, :]
            prev = cin
            for r in [first] + order:
                h = u_l[sl(r), :] + a_l[sl(r), :] * cin
                if c_below is None:
                    u_l[sl(r), :] = h
                else:
                    c_below[sl(r), :] = prev
                prev = h

        sweep(a_l, u_l, size, radix, down)
    return carry


def _lru_kernel(*refs, has_init, seq):
    if has_init:
        (x_ref, y_ref, cw_ref, cb_ref, wr_ref, br_ref, wi_ref, bi_ref, lam_ref, h0_ref,
         o_ref, xp_s, a_s, u_s, *lv_s) = refs
        hfin_ref = None
    else:
        (x_ref, y_ref, cw_ref, cb_ref, wr_ref, br_ref, wi_ref, bi_ref, lam_ref,
         o_ref, hfin_ref, xp_s, a_s, u_s, *lv_s) = refs
    n = x_ref.shape[0]
    nseq = n // seq
    pad = SUBLANE
    xp_s[0:pad, :] = jnp.zeros((pad, LANE), F32)
    xp_s[pad + n:pad + n + pad, :] = jnp.zeros((pad, LANE), F32)
    xp_s[pad:pad + n, :] = x_ref[...]
    t = lax.broadcasted_iota(jnp.int32, (n, 1), 0)
    xc = cb_ref[...]
    for j in range(LRU_CONV):
        off = j - CONV_LEFT
        tap = xp_s[pl.ds(pad + off, n), :]
        edge = [s * seq + e for s in range(1, nseq) for e in range(0, -off)] if off < 0 else \
               [s * seq - 1 - e for s in range(1, nseq) for e in range(0, off)]
        if edge:
            tap = jnp.where(functools.reduce(jnp.logical_or, [t == e for e in edge]), 0.0, tap)
        xc = xc + tap * cw_ref[j:j + 1, :]
    xcb = xc.astype(BF16)
    for d in range(2):
        r = _sigmoid(jnp.dot(xcb, wr_ref[d], preferred_element_type=F32) + br_ref[d])
        i = _sigmoid(jnp.dot(xcb, wi_ref[d], preferred_element_type=F32) + bi_ref[d])
        log_a = -LRU_C * r * _softplus(-lam_ref[d])
        a = jnp.exp(log_a)
        a_s[d] = a
        v = -jnp.tanh(log_a) * (a * a + 1.0)
        u_s[d] = jnp.where(v > 0.0, v * lax.rsqrt(v), 0.0) * (i * xc)

    for s in range(nseq):
        rows = pl.ds(s * seq, seq)
        for d in range(2):
            h0 = h0_ref[s, d] if has_init else jnp.zeros((1, LANE), F32)
            h_last = _linear_scan(a_s.at[d, rows], u_s.at[d, rows], [lv.at[d] for lv in lv_s], seq, h0,
                                  rev=d == 1)
            if hfin_ref is not None:
                hfin_ref[s, d] = h_last
    o_ref[...] = ((u_s[0] + u_s[1]) * jax.nn.gelu(y_ref[...])).astype(o_ref.dtype)


def _lru(proj, conv_w, conv_b, w_r, b_r, w_i, b_i, lam, nb, n, init=None, layer=0):
    kb = conv_w.shape[1] // LRU_BLOCK
    bt = _seqs_per_step(nb, n)
    rows = bt * n
    vec = lambda r: pl.BlockSpec((r, LANE), lambda b, k: (0, k))
    vec2 = pl.BlockSpec((2, 1, LANE), lambda b, k: (0, 0, k))
    wspec = pl.BlockSpec((2, None, LRU_BLOCK, LRU_BLOCK), lambda b, k: (0, k, 0, 0))
    in_specs = [pl.BlockSpec((rows, LANE), lambda b, k: (b, k)),
                pl.BlockSpec((rows, LANE), lambda b, k: (b, kb + k)),
                vec(LRU_CONV), vec(1), wspec, vec2, wspec, vec2, vec2]
    args = [proj, proj, conv_w, conv_b, w_r, b_r, w_i, b_i, lam]
    o_shape = jax.ShapeDtypeStruct((nb * n, kb * LANE), BF16)
    o_spec = pl.BlockSpec((rows, LANE), lambda b, k: (b, k))
    if init is not None:
        in_specs.append(pl.BlockSpec((bt, None, 2, 1, LANE), lambda b, k: (b, layer, 0, 0, k)))
        args.append(init)
        out_shape, out_specs = o_shape, o_spec
    else:
        out_shape = (o_shape, jax.ShapeDtypeStruct((nb, 2, 1, kb * LANE), F32))
        out_specs = (o_spec, pl.BlockSpec((bt, 2, 1, LANE), lambda b, k: (b, 0, 0, k)))
    return pl.pallas_call(
        functools.partial(_lru_kernel, has_init=init is not None, seq=n),
        grid=(nb // bt, kb),
        in_specs=in_specs,
        out_specs=out_specs,
        out_shape=out_shape,
        scratch_shapes=[pltpu.VMEM((rows + 2 * SUBLANE, LANE), F32),
                        pltpu.VMEM((2, rows, LANE), F32), pltpu.VMEM((2, rows, LANE), F32)]
                       + [pltpu.VMEM((2, 3, size // radix, LANE), F32) for size, radix in _scan_levels(n)[0]],
        compiler_params=_params(("arbitrary", "arbitrary"), 2 * 2 * rows * LANE * 4 + 12 * rows * LANE * 4),
    )(*args)


def kernel(x_prompt, x_sample, cache_attn_k, cache_attn_v, state_gla, state_lru, c, c_ctx, w_ada, b_ada, norm1_g, w_in, attn_sink, gla_w_decay, gla_b_decay, gla_norm_g, lru_conv_w, lru_conv_b, lru_w_rgate, lru_b_rgate, lru_w_igate, lru_b_igate, lru_lambda, w_out, norm2_g, w_gu, w_down, final_norm_g):
    nb_c, s_len, d = x_prompt.shape
    nb_l, n_lat, _ = x_sample.shape
    depth = w_in.shape[0]
    lru_w = lru_conv_w.shape[2]
    qw, kw = ATT_HEADS * HEAD_DIM, ATT_KV_HEADS * HEAD_DIM
    gdec0 = qw + 2 * kw + 4 * GLA_HEADS * GLA_DK
    gdec1 = gdec0 + 2 * GLA_LOWRANK
    in_tile = 512
    tail_w = 2 * lru_w + 2 * GLA_LOWRANK
    tail_pad = -(-tail_w // in_tile) * in_tile
    assert gdec0 % in_tile == 0 and w_in.shape[2] == gdec0 + tail_w

    w_in_b = w_in.astype(BF16)
    w_tail_b = jnp.concatenate(
        [w_in_b[:, :, gdec1:], w_in_b[:, :, gdec0:gdec1],
         jnp.zeros((depth, d, tail_pad - tail_w), BF16)], axis=2)
    dec_col = 2 * lru_w // LANE
    w_out_b = w_out.astype(BF16)
    w_gu_b = w_gu.astype(BF16)
    w_down_b = w_down.astype(BF16)

    cvecs = jnp.concatenate([c_ctx[None], c, jnp.zeros((MOD_ROWS - 1 - nb_l, d), F32)], axis=0)
    modr = _ada(cvecs, w_ada, b_ada).reshape(depth, MOD_ROWS, 6, 1, d)
    row_ctx = _ModRows(0, nb_c * s_len)
    row_lat = _ModRows(1, n_lat)

    tables = _rope_tables(n_lat)
    cache_k = cache_attn_k.reshape(nb_l, depth, cache_attn_k.shape[2], kw)
    cache_v = cache_attn_v.reshape(nb_l, depth, cache_attn_v.shape[2], kw)
    lru_init = state_lru.reshape(nb_l, depth, 2, 1, lru_w)

    xc = x_prompt.reshape(nb_c * s_len, d)
    xl = x_sample.reshape(nb_l * n_lat, d)
    new_k, new_v, new_gla, new_lru = [], [], [], []
    for l in range(depth):
        wdec = gla_w_decay[l].reshape(2, GLA_LOWRANK, GLA_HEADS, GLA_DK).transpose(0, 2, 1, 3)
        zpad = jnp.zeros((GLA_HEADS, GLA_LOWRANK, GLA_DK), F32)
        zrest = jnp.zeros((GLA_HEADS, LANE - 2 * GLA_LOWRANK, GLA_DK), F32)
        wf = jnp.concatenate([wdec[0], zpad, zrest], axis=1).astype(BF16)
        wb = jnp.concatenate([zpad, wdec[1], zrest], axis=1).astype(BF16)
        b_dec = gla_b_decay[l].reshape(2, GLA_HEADS, 1, GLA_DK)
        ng = gla_norm_g[l].reshape(GLA_HEADS, 1, LANE)
        lru_args = (lru_conv_w[l], lru_conv_b[l].reshape(1, lru_w),
                    lru_w_rgate[l].astype(BF16), lru_b_rgate[l].reshape(2, 1, lru_w),
                    lru_w_igate[l].astype(BF16), lru_b_igate[l].reshape(2, 1, lru_w),
                    lru_lambda[l].reshape(2, 1, lru_w))
        sink = attn_sink[l]

        def ffn(x, mix, row_of):
            x = _out_proj(*mix, w_out_b, x, modr, l, row_of, 2)
            h2 = _norm_mod(x, norm2_g[l], modr, l, row_of, 4, 3)
            act = _gate_up(h2, w_gu_b, l)
            return _down_proj(act, w_down_b, x, modr, l, row_of, 5)

        h = _norm_mod(xc, norm1_g[l], modr, l, row_ctx, 1, 0)
        proj = _in_proj(h, w_in_b, l, gdec0)
        tail = _in_proj(h, w_tail_b, l, tail_pad)
        attn = _attn_ctx(proj, sink, nb_c, s_len)
        gla, gla_fin = _gla(proj, tail, dec_col, wf, wb, b_dec, ng, nb_c, s_len)
        lru, lru_fin = _lru(tail, *lru_args, nb_c, s_len)
        new_k.append(proj[:, qw:qw + kw].reshape(nb_c, s_len, ATT_KV_HEADS, HEAD_DIM))
        new_v.append(proj[:, qw + kw:qw + 2 * kw].reshape(nb_c, s_len, ATT_KV_HEADS, HEAD_DIM))
        new_gla.append(gla_fin)
        new_lru.append(lru_fin.reshape(nb_c, 2, lru_w))
        xc = ffn(xc, (attn, gla, lru), row_ctx)

        h = _norm_mod(xl, norm1_g[l], modr, l, row_lat, 1, 0)
        proj = _in_proj(h, w_in_b, l, gdec0)
        tail = _in_proj(h, w_tail_b, l, tail_pad)
        attn = _attn_lat(proj, sink, cache_k, cache_v, l, tables, nb_l, n_lat)
        gla = _gla(proj, tail, dec_col, wf, wb, b_dec, ng, nb_l, n_lat, init=state_gla, layer=l)
        lru = _lru(tail, *lru_args, nb_l, n_lat, init=lru_init, layer=l)
        xl = ffn(xl, (attn, gla, lru), row_lat)

    y_prompt = _final_norm(xc, final_norm_g).reshape(nb_c, s_len, d)
    y_sample = _final_norm(xl, final_norm_g).reshape(nb_l, n_lat, d)
    return (y_prompt, y_sample, jnp.stack(new_k, axis=1), jnp.stack(new_v, axis=1),
            jnp.stack(new_gla, axis=1), jnp.stack(new_lru, axis=1))
```

```python
import functools
from typing import NamedTuple

import jax
import jax.numpy as jnp
from jax import lax
from jax.experimental import pallas as pl
from jax.experimental.pallas import tpu as pltpu

F32 = jnp.float32
BF16 = jnp.bfloat16

EPS = 1e-6
HEAD_DIM = 128
ATT_HEADS = 16
ATT_KV_HEADS = 4
ATT_GROUPS = ATT_HEADS // ATT_KV_HEADS
WINDOW = 128
Q_BLOCK = 128
ATT_TILE_ROWS = 128
GRID_W = 64
ROPE_THETA = 10000.0
ROPE_FREQS = HEAD_DIM // 4
GLA_HEADS = 8
GLA_DK = 128
GLA_LOWRANK = 16
GLA_TAU = 16.0
GLA_CHUNK = 64
GLA_BULK_ROWS = 256
GLA_ELEMENTWISE_ROWS = 512
LRU_BLOCK = 128
LRU_CONV = 4
CONV_LEFT = 2
LRU_C = 8.0
MIXER_ROWS_PER_STEP = 1024
SCAN_TOP_ROWS = 8
SCAN_SLAB_ROWS = 64

LANE = 128
SUBLANE = 8
VMEM_BYTES_V7X = 64 * 2**20
MXU_COLS = 256
GATE_UP_SLAB_ROWS = 1024
MOD_ROWS = 16

NT_DIMS = (((1,), (1,)), ((), ()))
TN_DIMS = (((0,), (0,)), ((), ()))


def _params(sem, est_bytes):
    limit = int(min(VMEM_BYTES_V7X - 4 * 2**20, max(est_bytes + 8 * 2**20, 32 * 2**20)))
    return pltpu.CompilerParams(dimension_semantics=sem, vmem_limit_bytes=limit)


def _seqs_per_step(nb, n):
    bt = max(1, MIXER_ROWS_PER_STEP // n)
    while nb % bt:
        bt -= 1
    return bt


def _softplus(x):
    return jnp.maximum(x, 0.0) + jnp.log1p(jnp.exp(-jnp.abs(x)))


def _log_sigmoid(x):
    return -_softplus(-x)


def _sigmoid(x):
    return 0.5 * jnp.tanh(0.5 * x) + 0.5


def _ada_kernel(c_ref, w_ref, b_ref, o_ref):
    cv = c_ref[...]
    s = (cv * jax.nn.sigmoid(cv)).astype(BF16)
    o_ref[...] = jnp.dot(s, w_ref[...].astype(BF16), preferred_element_type=F32) + b_ref[...]


def _ada(cvecs, w_ada, b_ada):
    nl, d, n6 = w_ada.shape
    tn = 512
    return pl.pallas_call(
        _ada_kernel,
        grid=(nl, n6 // tn),
        in_specs=[pl.BlockSpec((MOD_ROWS, d), lambda l, j: (0, 0)),
                  pl.BlockSpec((None, d, tn), lambda l, j: (l, 0, j)),
                  pl.BlockSpec((None, 1, tn), lambda l, j: (l, 0, j))],
        out_specs=pl.BlockSpec((None, MOD_ROWS, tn), lambda l, j: (l, 0, j)),
        out_shape=jax.ShapeDtypeStruct((nl, MOD_ROWS, n6), F32),
        compiler_params=_params(("arbitrary", "arbitrary"), 2 * d * tn * 4 + d * tn * 2),
    )(cvecs, w_ada, b_ada.reshape(nl, 1, n6))


def _norm_mod_kernel(x_ref, g_ref, sc_ref, sh_ref, o_ref):
    x = x_ref[...]
    y = x * lax.rsqrt(jnp.mean(x * x, axis=-1, keepdims=True) + EPS) * g_ref[...]
    o_ref[...] = (y * (1.0 + sc_ref[...]) + sh_ref[...]).astype(o_ref.dtype)


def _norm_kernel(x_ref, g_ref, o_ref):
    x = x_ref[...]
    y = x * lax.rsqrt(jnp.mean(x * x, axis=-1, keepdims=True) + EPS) * g_ref[...]
    o_ref[...] = y.astype(o_ref.dtype)


class _ModRows(NamedTuple):
    first: int
    segment: int

    def tile(self, m, want):
        tm = min(want, m, self.segment)
        assert m % tm == 0 and self.segment % tm == 0
        return tm

    def __call__(self, row0):
        return self.first + row0 // self.segment


def _norm_mod(x, g, modr, layer, row_of, sc_idx, sh_idx, tr=256):
    m, d = x.shape
    tr = row_of.tile(m, tr)
    mspec = lambda k: pl.BlockSpec((None, None, None, 1, d), lambda i: (layer, row_of(i * tr), k, 0, 0))
    return pl.pallas_call(
        _norm_mod_kernel,
        grid=(m // tr,),
        in_specs=[pl.BlockSpec((tr, d), lambda i: (i, 0)),
                  pl.BlockSpec((1, d), lambda i: (0, 0)),
                  mspec(sc_idx), mspec(sh_idx)],
        out_specs=pl.BlockSpec((tr, d), lambda i: (i, 0)),
        out_shape=jax.ShapeDtypeStruct((m, d), BF16),
        compiler_params=_params(("arbitrary",), 2 * tr * d * 6 + 2 * tr * d * 4),
    )(x, g.reshape(1, d), modr, modr)


def _final_norm(x, g, tr=256):
    m, d = x.shape
    return pl.pallas_call(
        _norm_kernel,
        grid=(m // tr,),
        in_specs=[pl.BlockSpec((tr, d), lambda i: (i, 0)),
                  pl.BlockSpec((1, d), lambda i: (0, 0))],
        out_specs=pl.BlockSpec((tr, d), lambda i: (i, 0)),
        out_shape=jax.ShapeDtypeStruct((m, d), F32),
        compiler_params=_params(("arbitrary",), 2 * tr * d * 8 + 2 * tr * d * 4),
    )(x, g.reshape(1, d))


def _mm_kernel(x_ref, w_ref, o_ref):
    o_ref[...] = jnp.dot(x_ref[...], w_ref[...], preferred_element_type=F32)


def _in_proj(h, w, layer, n, tm=2048, tn=512):
    m, k = h.shape
    assert n % tn == 0 and n <= w.shape[2]
    tm = min(tm, m)
    return pl.pallas_call(
        _mm_kernel,
        grid=(m // tm, n // tn),
        in_specs=[pl.BlockSpec((tm, k), lambda i, j: (i, 0)),
                  pl.BlockSpec((None, k, tn), lambda i, j: (layer, 0, j))],
        out_specs=pl.BlockSpec((tm, tn), lambda i, j: (i, j)),
        out_shape=jax.ShapeDtypeStruct((m, n), F32),
        compiler_params=_params(("arbitrary", "arbitrary"), 2 * (tm * k * 2 + k * tn * 2 + tm * tn * 4)),
    )(h, w)


def _out_proj_kernel(a_ref, b_ref, c_ref, w_ref, x_ref, g_ref, o_ref):
    ka, kb = a_ref.shape[1], b_ref.shape[1]
    for j0 in range(0, o_ref.shape[1], MXU_COLS):
        cols = slice(j0, j0 + MXU_COLS)
        acc = jnp.dot(a_ref[...], w_ref[0:ka, cols], preferred_element_type=F32)
        acc += jnp.dot(b_ref[...], w_ref[ka:ka + kb, cols], preferred_element_type=F32)
        acc += jnp.dot(c_ref[...], w_ref[ka + kb:, cols], preferred_element_type=F32)
        o_ref[:, cols] = x_ref[:, cols] + g_ref[:, cols] * acc


def _out_proj(attn, gla, lru, w, x, modr, layer, row_of, gate_idx, tm=1024, tn=1024):
    m, d = x.shape
    k = w.shape[1]
    tm = row_of.tile(m, tm)
    xs = lambda a: pl.BlockSpec((tm, a.shape[1]), lambda i, j: (i, 0))
    return pl.pallas_call(
        _out_proj_kernel,
        grid=(m // tm, d // tn),
        in_specs=[xs(attn), xs(gla), xs(lru),
                  pl.BlockSpec((None, k, tn), lambda i, j: (layer, 0, j)),
                  pl.BlockSpec((tm, tn), lambda i, j: (i, j)),
                  pl.BlockSpec((None, None, None, 1, tn),
                               lambda i, j: (layer, row_of(i * tm), gate_idx, 0, j))],
        out_specs=pl.BlockSpec((tm, tn), lambda i, j: (i, j)),
        out_shape=jax.ShapeDtypeStruct((m, d), F32),
        compiler_params=_params(("arbitrary", "arbitrary"),
                                2 * (tm * k * 2 + k * tn * 2 + 2 * tm * tn * 4) + tm * tn * 4),
    )(attn, gla, lru, w, x, modr)


def _gate_up_kernel(h_ref, wg_ref, wu_ref, o_ref):
    tm = h_ref.shape[0]
    slab = min(tm, GATE_UP_SLAB_ROWS)
    for r0 in range(0, tm, slab):
        h = h_ref[r0:r0 + slab, :]
        a = jnp.dot(h, wg_ref[...], preferred_element_type=F32)
        u = jnp.dot(h, wu_ref[...], preferred_element_type=F32)
        o_ref[r0:r0 + slab, :] = (a * _sigmoid(a) * u).astype(o_ref.dtype)


def _gate_up(h, w, layer, tm=2048, tn=256):
    m, k = h.shape
    f = w.shape[2] // 2
    tm = min(tm, m)
    nj = f // tn
    return pl.pallas_call(
        _gate_up_kernel,
        grid=(m // tm, nj),
        in_specs=[pl.BlockSpec((tm, k), lambda i, j: (i, 0)),
                  pl.BlockSpec((None, k, tn), lambda i, j: (layer, 0, j)),
                  pl.BlockSpec((None, k, tn), lambda i, j: (layer, 0, nj + j))],
        out_specs=pl.BlockSpec((tm, tn), lambda i, j: (i, j)),
        out_shape=jax.ShapeDtypeStruct((m, f), BF16),
        compiler_params=_params(("arbitrary", "arbitrary"),
                                2 * (tm * k * 2 + 2 * k * tn * 2 + tm * tn * 2) + 3 * tm * tn * 4),
    )(h, w, w)


def _down_kernel(a_ref, w_ref, x_ref, g_ref, o_ref):
    for j0 in range(0, o_ref.shape[1], MXU_COLS):
        cols = slice(j0, j0 + MXU_COLS)
        acc = jnp.dot(a_ref[...], w_ref[:, cols], preferred_element_type=F32)
        o_ref[:, cols] = x_ref[:, cols] + g_ref[:, cols] * acc


def _down_proj(act, w, x, modr, layer, row_of, gate_idx, tm=512, tn=512):
    m, d = x.shape
    k = w.shape[1]
    tm = row_of.tile(m, tm)
    return pl.pallas_call(
        _down_kernel,
        grid=(m // tm, d // tn),
        in_specs=[pl.BlockSpec((tm, k), lambda i, j: (i, 0)),
                  pl.BlockSpec((None, k, tn), lambda i, j: (layer, 0, j)),
                  pl.BlockSpec((tm, tn), lambda i, j: (i, j)),
                  pl.BlockSpec((None, None, None, 1, tn),
                               lambda i, j: (layer, row_of(i * tm), gate_idx, 0, j))],
        out_specs=pl.BlockSpec((tm, tn), lambda i, j: (i, j)),
        out_shape=jax.ShapeDtypeStruct((m, d), F32),
        compiler_params=_params(("arbitrary", "arbitrary"),
                                2 * (tm * k * 2 + k * tn * 2 + 2 * tm * tn * 4) + tm * tn * 4),
    )(act, w, x, modr)


def _attn_ctx_kernel(sink_ref, q_ref, k_ref, v_ref, o_ref):
    scale = HEAD_DIM ** -0.5
    head = lambda ref, h: ref[:, h * HEAD_DIM:(h + 1) * HEAD_DIM]
    kt = [head(k_ref, kh).T.astype(BF16) for kh in range(ATT_KV_HEADS)]
    vv = [head(v_ref, kh).astype(BF16) for kh in range(ATT_KV_HEADS)]

    def scores(h):
        return jnp.dot(head(q_ref, h).astype(BF16), kt[h // ATT_GROUPS],
                       preferred_element_type=F32) * scale

    s_next = scores(0)
    for h in range(ATT_HEADS):
        s = s_next
        if h + 1 < ATT_HEADS:
            s_next = scores(h + 1)
        sk = sink_ref[h]
        m = jnp.maximum(jnp.max(s, axis=-1, keepdims=True), sk)
        p = jnp.exp(s - m)
        denom = jnp.sum(p, axis=-1, keepdims=True) + jnp.exp(sk - m)
        pn = (p * (1.0 / denom)).astype(BF16)
        o_ref[:, h * HEAD_DIM:(h + 1) * HEAD_DIM] = jnp.dot(
            pn, vv[h // ATT_GROUPS], preferred_element_type=F32).astype(o_ref.dtype)


def _attn_ctx(proj, sink, nb, s_len):
    qw = ATT_HEADS * HEAD_DIM
    kw = ATT_KV_HEADS * HEAD_DIM
    return pl.pallas_call(
        _attn_ctx_kernel,
        grid=(nb,),
        in_specs=[pl.BlockSpec(memory_space=pltpu.SMEM),
                  pl.BlockSpec((s_len, qw), lambda b: (b, 0)),
                  pl.BlockSpec((s_len, kw), lambda b: (b, qw // kw)),
                  pl.BlockSpec((s_len, kw), lambda b: (b, qw // kw + 1))],
        out_specs=pl.BlockSpec((s_len, qw), lambda b: (b, 0)),
        out_shape=jax.ShapeDtypeStruct((nb * s_len, qw), BF16),
        compiler_params=_params(("arbitrary",), 2 * s_len * (qw + 2 * kw) * 4 + 2 * s_len * qw * 2
                                + 8 * ATT_GROUPS * s_len * s_len * 4),
    )(sink, proj, proj, proj)


def _attn_lat_kernel(sink_ref, q_ref, k_ref, v_ref, kc_ref, vc_ref, cos_ref, sa_ref, sb_ref,
                     o_ref, kt_s, vb_s, qr_s):
    n = q_ref.shape[0]
    nqb = n // Q_BLOCK
    kh = pl.program_id(1)
    scale = HEAD_DIM ** -0.5
    wblocks = (Q_BLOCK + 2 * WINDOW) // Q_BLOCK

    def rope(x, rows):
        return (x * cos_ref[rows, :] + pltpu.roll(x, HEAD_DIM - ROPE_FREQS, 1) * sa_ref[rows, :]
                + pltpu.roll(x, ROPE_FREQS, 1) * sb_ref[rows, :])

    def prep(j, carry):
        rows = pl.ds(pl.multiple_of(j * Q_BLOCK, Q_BLOCK), Q_BLOCK)
        kt_s[j] = rope(k_ref[rows, :], rows).T.astype(BF16)
        for g in range(ATT_GROUPS):
            lanes = slice(g * HEAD_DIM, (g + 1) * HEAD_DIM)
            qr_s[rows, lanes] = rope(q_ref[rows, lanes], rows).astype(BF16)
        return carry

    lax.fori_loop(0, nqb, prep, 0)
    vb_s[...] = v_ref[...].astype(BF16)
    kct = kc_ref[...].T.astype(BF16)
    vc = vc_ref[...].astype(BF16)
    qi = lax.broadcasted_iota(jnp.int32, (Q_BLOCK, Q_BLOCK), 0)
    ki = lax.broadcasted_iota(jnp.int32, (Q_BLOCK, Q_BLOCK), 1)
    rel = qi - ki

    def block(i, carry):
        qs = pl.multiple_of(i * Q_BLOCK, Q_BLOCK)
        kb = jnp.clip(i - WINDOW // Q_BLOCK, 0, nqb - wblocks)
        biases = []
        for j in range(wblocks):
            off = (i - kb - j) * Q_BLOCK
            biases.append(jnp.where(jnp.abs(rel + off) <= WINDOW, 0.0, -jnp.inf).astype(F32))

        tr = ATT_TILE_ROWS
        tiles = [(g, r0) for g in range(ATT_GROUPS) for r0 in range(0, Q_BLOCK, tr)]

        def scores(g, r0):
            q = qr_s[pl.ds(qs + r0, tr), g * HEAD_DIM:(g + 1) * HEAD_DIM]
            parts = [jnp.dot(q, kt_s[kb + j], preferred_element_type=F32) * scale + biases[j][r0:r0 + tr]
                     for j in range(wblocks)]
            s2 = jnp.dot(q, kct, preferred_element_type=F32) * scale
            return parts + [s2[:, j * LANE:(j + 1) * LANE] for j in range(s2.shape[1] // LANE)]

        s_next = scores(*tiles[0])
        for t, (g, r0) in enumerate(tiles):
            s_parts = s_next
            if t + 1 < len(tiles):
                s_next = scores(*tiles[t + 1])
            sk = sink_ref[kh * ATT_GROUPS + g]
            m = functools.reduce(jnp.maximum, s_parts)
            m = jnp.maximum(jnp.max(m, axis=-1, keepdims=True), sk)
            p_parts = [jnp.exp(s - m) for s in s_parts]
            denom = (jnp.sum(functools.reduce(jnp.add, p_parts), axis=-1, keepdims=True)
                     + jnp.exp(sk - m))
            r = 1.0 / denom
            o = None
            for j, p in enumerate(p_parts):
                if j < wblocks:
                    vw = vb_s[pl.ds(pl.multiple_of((kb + j) * Q_BLOCK, Q_BLOCK), Q_BLOCK), :]
                else:
                    vw = vc[(j - wblocks) * LANE:(j - wblocks + 1) * LANE, :]
                ov = jnp.dot((p * r).astype(BF16), vw, preferred_element_type=F32)
                o = ov if o is None else o + ov
            o_ref[pl.ds(qs + r0, tr), g * HEAD_DIM:(g + 1) * HEAD_DIM] = o.astype(o_ref.dtype)
        return carry

    lax.fori_loop(0, nqb, block, 0, unroll=2)


def _attn_lat(proj, sink, cache_k, cache_v, layer, tables, nb, n):
    qw = ATT_HEADS * HEAD_DIM
    gw = ATT_GROUPS * HEAD_DIM
    past = cache_k.shape[2]
    kcol = qw // HEAD_DIM
    vcol = kcol + ATT_KV_HEADS
    cspec = pl.BlockSpec((None, None, past, HEAD_DIM), lambda b, h: (b, layer, 0, h))
    tspec = pl.BlockSpec((n, HEAD_DIM), lambda b, h: (0, 0))
    return pl.pallas_call(
        _attn_lat_kernel,
        grid=(nb, ATT_KV_HEADS),
        in_specs=[pl.BlockSpec(memory_space=pltpu.SMEM),
                  pl.BlockSpec((n, gw), lambda b, h: (b, h)),
                  pl.BlockSpec((n, HEAD_DIM), lambda b, h: (b, kcol + h)),
                  pl.BlockSpec((n, HEAD_DIM), lambda b, h: (b, vcol + h)),
                  cspec, cspec, tspec, tspec, tspec],
        out_specs=pl.BlockSpec((n, gw), lambda b, h: (b, h)),
        out_shape=jax.ShapeDtypeStruct((nb * n, qw), BF16),
        scratch_shapes=[pltpu.VMEM((n // Q_BLOCK, HEAD_DIM, Q_BLOCK), BF16),
                        pltpu.VMEM((n, HEAD_DIM), BF16), pltpu.VMEM((n, gw), BF16)],
        compiler_params=_params(("arbitrary", "arbitrary"),
                                2 * n * (gw + 2 * HEAD_DIM) * 4 + 2 * n * gw * 2 + 6 * n * HEAD_DIM * 4
                                + 4 * n * HEAD_DIM + n * gw * 2),
    )(sink, proj, proj, proj, cache_k, cache_v, *tables)


def _rope_tables(n):
    pos = jnp.arange(n)
    row = (pos // GRID_W).astype(F32)
    col = (pos % GRID_W).astype(F32)
    inv = ROPE_THETA ** (-jnp.arange(ROPE_FREQS, dtype=F32) / ROPE_FREQS)
    ang_r = row[:, None] * inv[None, :]
    ang_c = col[:, None] * inv[None, :]
    cr, sr, cc, sc = jnp.cos(ang_r), jnp.sin(ang_r), jnp.cos(ang_c), jnp.sin(ang_c)
    z = jnp.zeros_like(cr)
    cos = jnp.concatenate([cr, cr, cc, cc], axis=1)
    sa = jnp.concatenate([-sr, z, -sc, z], axis=1)
    sb = jnp.concatenate([z, sr, z, sc], axis=1)
    return cos, sa, sb


def _gla_kernel(*refs, has_init, seq):
    if has_init:
        (q_ref, k_ref, v_ref, g_ref, dec_ref, wf_ref, wb_ref, bd_ref, ng_ref, s0_ref,
         o_ref, vb_s, qin_s, kdec_s, tot_s, o_s, kv_s, sin_s) = refs
        sfin_ref = None
    else:
        (q_ref, k_ref, v_ref, g_ref, dec_ref, wf_ref, wb_ref, bd_ref, ng_ref,
         o_ref, sfin_ref, vb_s, qin_s, kdec_s, tot_s, o_s, kv_s, sin_s) = refs
    n = q_ref.shape[0]
    c = GLA_CHUNK
    nc = n // c
    blk = min(seq, GLA_BULK_ROWS)
    qscale = GLA_DK ** -0.5
    shift = c.bit_length() - 1
    ri = lax.broadcasted_iota(jnp.int32, (blk, blk), 0)
    ci = lax.broadcasted_iota(jnp.int32, (blk, blk), 1)
    same = lax.shift_right_logical(ri, shift) == lax.shift_right_logical(ci, shift)
    keep = (same & (ri >= ci), same & (ri <= ci))
    sums = tuple(jnp.concatenate([keep[d].astype(BF16), same.astype(BF16)], axis=0) for d in range(2))
    wdec = jnp.concatenate([wf_ref[...], wb_ref[...]], axis=1)
    bdec = jnp.concatenate([bd_ref[0], bd_ref[1]], axis=1)

    def split3(x):
        hi = x.astype(BF16)
        r1 = x - hi.astype(F32)
        mid = r1.astype(BF16)
        return jnp.concatenate([hi, mid, (r1 - mid.astype(F32)).astype(BF16)], axis=1)

    def fold3(y):
        y = (y[:, :LANE] + y[:, LANE:2 * LANE]) + y[:, 2 * LANE:]
        return y[:blk], y[blk:]

    def chain(rows, d, z, qc, kc, vc):
        st = {}

        def log_decay():
            st["p"] = split3(_log_sigmoid(z[:, d * GLA_DK:(d + 1) * GLA_DK]) * (1.0 / GLA_TAU))

        def chunk_sums():
            st["y"] = jnp.dot(sums[d], st["p"], preferred_element_type=F32)

        def decayed():
            cum, tot = fold3(st["y"])
            tot_s[d, rows, :] = tot
            st["q"] = (qc * jnp.exp(cum)).astype(BF16)
            st["k"] = (kc * jnp.exp(-cum)).astype(BF16)
            qin_s[d, rows, :] = st["q"]
            kdec_s[d, rows, :] = (kc * jnp.exp(tot - cum)).astype(BF16)

        def scores():
            st["a"] = lax.dot_general(st["q"], st["k"], NT_DIMS, preferred_element_type=F32)

        def mask():
            st["att"] = jnp.where(keep[d], st["a"], 0.0).astype(BF16)

        def within():
            o_s[d, rows, :] = jnp.dot(st["att"], vc, preferred_element_type=F32)

        return [log_decay, chunk_sums, decayed, scores, mask, within]

    per_step = 2 if (n // blk) % 2 == 0 else 1

    def bulk(i, carry):
        chains = []
        for b in range(per_step):
            rows = pl.ds(pl.multiple_of((i * per_step + b) * blk, blk), blk)
            z = jnp.dot(dec_ref[rows, :].astype(BF16), wdec, preferred_element_type=F32) + bdec
            vc = v_ref[rows, :].astype(BF16)
            vb_s[rows, :] = vc
            chains += [chain(rows, d, z, q_ref[rows, :] * qscale, k_ref[rows, :], vc) for d in range(2)]
        nstage = len(chains[0])
        for t in range(nstage + len(chains) - 1):
            for lag, stages in enumerate(chains):
                if 0 <= t - lag < nstage:
                    stages[t - lag]()
        return carry

    lax.fori_loop(0, n // (blk * per_step), bulk, 0)

    def increments(j, carry):
        rows = pl.ds(pl.multiple_of(j * c, c), c)
        vc = vb_s[rows, :]
        for d in range(2):
            kv_s[d, j] = lax.dot_general(vc, kdec_s[d, rows, :], TN_DIMS, preferred_element_type=F32)
        return carry

    lax.fori_loop(0, nc, increments, 0, unroll=min(nc, 4))

    ncs = seq // c
    for s in range(n // seq):
        def scan(j, carry, base=s * ncs):
            sf, sb = carry
            jf = base + j
            jb = base + ncs - 1 - j
            sin_s[0, jf] = sf.astype(BF16)
            sin_s[1, jb] = sb.astype(BF16)
            sf = jnp.exp(tot_s[0, pl.ds(jf * c, 1), :]) * sf + kv_s[0, jf]
            sb = jnp.exp(tot_s[1, pl.ds(jb * c, 1), :]) * sb + kv_s[1, jb]
            return sf, sb

        if has_init:
            init = (s0_ref[s, 0].T, s0_ref[s, 1].T)
        else:
            init = (jnp.zeros((LANE, GLA_DK), F32), jnp.zeros((LANE, GLA_DK), F32))
        sf, sb = lax.fori_loop(0, ncs, scan, init)
        if sfin_ref is not None:
            sfin_ref[s, 0] = sf.T
            sfin_ref[s, 1] = sb.T

    def carried(j, carry):
        rows = pl.ds(pl.multiple_of(j * c, c), c)
        o_s[0, rows, :] = ((o_s[0, rows, :] + lax.dot_general(qin_s[0, rows, :], sin_s[0, j], NT_DIMS,
                                                              preferred_element_type=F32))
                           + (o_s[1, rows, :] + lax.dot_general(qin_s[1, rows, :], sin_s[1, j], NT_DIMS,
                                                                preferred_element_type=F32)))
        return carry

    lax.fori_loop(0, nc, carried, 0, unroll=min(nc, 4))

    ew = min(n, GLA_ELEMENTWISE_ROWS)

    def finish(i, carry):
        rows = pl.ds(pl.multiple_of(i * ew, ew), ew)
        o = o_s[0, rows, :]
        o = o * lax.rsqrt(jnp.mean(o * o, axis=-1, keepdims=True) + EPS) * ng_ref[...]
        g = g_ref[rows, :]
        o_ref[rows, :] = (o * (g * _sigmoid(g))).astype(o_ref.dtype)
        return carry

    lax.fori_loop(0, n // ew, finish, 0)


def _gla(proj, proj_tail, dec_col, wf, wb, b_dec, norm_g, nb, n, init=None, layer=0):
    hh = GLA_HEADS
    q0 = (ATT_HEADS + 2 * ATT_KV_HEADS) * HEAD_DIM // LANE
    bt = _seqs_per_step(nb, n)
    rows = bt * n
    nc = rows // GLA_CHUNK
    col = lambda base: pl.BlockSpec((rows, LANE), lambda b, h: (b, base + h))
    in_specs = [col(q0), col(q0 + hh), col(q0 + 2 * hh), col(q0 + 3 * hh),
                pl.BlockSpec((rows, LANE), lambda b, h: (b, dec_col)),
                pl.BlockSpec((None, LANE, LANE), lambda b, h: (h, 0, 0)),
                pl.BlockSpec((None, LANE, LANE), lambda b, h: (h, 0, 0)),
                pl.BlockSpec((2, None, 1, LANE), lambda b, h: (0, h, 0, 0)),
                pl.BlockSpec((None, 1, LANE), lambda b, h: (h, 0, 0))]
    args = [proj, proj, proj, proj, proj_tail, wf, wb, b_dec, norm_g]
    o_shape = jax.ShapeDtypeStruct((nb * n, hh * LANE), BF16)
    o_spec = pl.BlockSpec((rows, LANE), lambda b, h: (b, h))
    if init is not None:
        in_specs.append(pl.BlockSpec((bt, None, 2, None, GLA_DK, LANE),
                                     lambda b, h: (b, layer, 0, h, 0, 0)))
        args.append(init)
        out_shape, out_specs = o_shape, o_spec
    else:
        out_shape = (o_shape, jax.ShapeDtypeStruct((nb, 2, hh, GLA_DK, LANE), F32))
        out_specs = (o_spec, pl.BlockSpec((bt, 2, None, GLA_DK, LANE), lambda b, h: (b, 0, h, 0, 0)))
    return pl.pallas_call(
        functools.partial(_gla_kernel, has_init=init is not None, seq=n),
        grid=(nb // bt, hh),
        in_specs=in_specs,
        out_specs=out_specs,
        out_shape=out_shape,
        scratch_shapes=[pltpu.VMEM((rows, LANE), BF16),
                        pltpu.VMEM((2, rows, GLA_DK), BF16),
                        pltpu.VMEM((2, rows, GLA_DK), BF16),
                        pltpu.VMEM((2, rows, GLA_DK), F32),
                        pltpu.VMEM((2, rows, LANE), F32),
                        pltpu.VMEM((2, nc, LANE, GLA_DK), F32),
                        pltpu.VMEM((2, nc, LANE, GLA_DK), BF16)],
        compiler_params=_params(("arbitrary", "arbitrary"),
                                2 * 5 * rows * LANE * 4 + 2 * rows * LANE * 2 + rows * LANE * (2 + 8 + 16)
                                + nc * LANE * GLA_DK * 12 + 4 * 2**20),
    )(*args)


def _scan_levels(n):
    levels = []
    size = n
    while size > SCAN_TOP_ROWS:
        radix = 4 if not levels else 8
        levels.append((size, radix))
        size //= radix
    return levels, size


def _linear_scan(a, u, lv, n, h0, rev):
    levels, top = _scan_levels(n)

    def sweep(a_l, u_l, size, radix, body):
        m = size // radix
        sb = min(m, SCAN_SLAB_ROWS)
        first = radix - 1 if rev else 0
        order = list(range(radix - 2, -1, -1)) if rev else list(range(1, radix))

        def blk(j, carry):
            sl = lambda r: pl.ds(j * (sb * radix) + r, sb, stride=radix)
            body(sl, pl.ds(pl.multiple_of(j * sb, sb), sb), first, order)
            return carry

        if m // sb == 1:
            blk(0, 0)
        else:
            lax.fori_loop(0, m // sb, blk, 0)

    src_a, src_u = a, u
    for l, (size, radix) in enumerate(levels):
        nxt = lv[l]

        def up(sl, dense, first, order, src_a=src_a, src_u=src_u, nxt=nxt):
            pa = src_a[sl(first), :]
            pu = src_u[sl(first), :]
            for r in order:
                ar = src_a[sl(r), :]
                pu = ar * pu + src_u[sl(r), :]
                pa = ar * pa
                src_a[sl(r), :] = pa
                src_u[sl(r), :] = pu
            nxt[0, dense, :] = pa
            nxt[1, dense, :] = pu

        sweep(src_a, src_u, size, radix, up)
        src_a, src_u = nxt.at[0], nxt.at[1]

    carry = h0
    top_c = lv[len(levels) - 1].at[2]
    for g in (range(top - 1, -1, -1) if rev else range(top)):
        top_c[g:g + 1, :] = carry
        carry = src_a[g:g + 1, :] * carry + src_u[g:g + 1, :]

    for l in range(len(levels) - 1, -1, -1):
        size, radix = levels[l]
        a_l, u_l = (a, u) if l == 0 else (lv[l - 1].at[0], lv[l - 1].at[1])
        c_l = lv[l].at[2]
        c_below = None if l == 0 else lv[l - 1].at[2]

        def down(sl, dense, first, order, a_l=a_l, u_l=u_l, c_l=c_l, c_below=c_below):
            cin = c_l[dense, :]
            prev = cin
            for r in [first] + order:
                h = u_l[sl(r), :] + a_l[sl(r), :] * cin
                if c_below is None:
                    u_l[sl(r), :] = h
                else:
                    c_below[sl(r), :] = prev
                prev = h

        sweep(a_l, u_l, size, radix, down)
    return carry


def _lru_kernel(*refs, has_init, seq):
    if has_init:
        (x_ref, y_ref, cw_ref, cb_ref, wr_ref, br_ref, wi_ref, bi_ref, lam_ref, h0_ref,
         o_ref, xp_s, a_s, u_s, *lv_s) = refs
        hfin_ref = None
    else:
        (x_ref, y_ref, cw_ref, cb_ref, wr_ref, br_ref, wi_ref, bi_ref, lam_ref,
         o_ref, hfin_ref, xp_s, a_s, u_s, *lv_s) = refs
    n = x_ref.shape[0]
    nseq = n // seq
    pad = SUBLANE
    xp_s[0:pad, :] = jnp.zeros((pad, LANE), F32)
    xp_s[pad + n:pad + n + pad, :] = jnp.zeros((pad, LANE), F32)
    xp_s[pad:pad + n, :] = x_ref[...]
    t = lax.broadcasted_iota(jnp.int32, (n, 1), 0)
    xc = cb_ref[...]
    for j in range(LRU_CONV):
        off = j - CONV_LEFT
        tap = xp_s[pl.ds(pad + off, n), :]
        edge = [s * seq + e for s in range(1, nseq) for e in range(0, -off)] if off < 0 else \
               [s * seq - 1 - e for s in range(1, nseq) for e in range(0, off)]
        if edge:
            tap = jnp.where(functools.reduce(jnp.logical_or, [t == e for e in edge]), 0.0, tap)
        xc = xc + tap * cw_ref[j:j + 1, :]
    xcb = xc.astype(BF16)
    for d in range(2):
        r = _sigmoid(jnp.dot(xcb, wr_ref[d], preferred_element_type=F32) + br_ref[d])
        i = _sigmoid(jnp.dot(xcb, wi_ref[d], preferred_element_type=F32) + bi_ref[d])
        log_a = -LRU_C * r * _softplus(-lam_ref[d])
        a = jnp.exp(log_a)
        a_s[d] = a
        v = -jnp.tanh(log_a) * (a * a + 1.0)
        u_s[d] = jnp.where(v > 0.0, v * lax.rsqrt(v), 0.0) * (i * xc)

    for s in range(nseq):
        rows = pl.ds(s * seq, seq)
        for d in range(2):
            h0 = h0_ref[s, d] if has_init else jnp.zeros((1, LANE), F32)
            h_last = _linear_scan(a_s.at[d, rows], u_s.at[d, rows], [lv.at[d] for lv in lv_s], seq, h0,
                                  rev=d == 1)
            if hfin_ref is not None:
                hfin_ref[s, d] = h_last
    o_ref[...] = ((u_s[0] + u_s[1]) * jax.nn.gelu(y_ref[...])).astype(o_ref.dtype)


def _lru(proj, conv_w, conv_b, w_r, b_r, w_i, b_i, lam, nb, n, init=None, layer=0):
    kb = conv_w.shape[1] // LRU_BLOCK
    bt = _seqs_per_step(nb, n)
    rows = bt * n
    vec = lambda r: pl.BlockSpec((r, LANE), lambda b, k: (0, k))
    vec2 = pl.BlockSpec((2, 1, LANE), lambda b, k: (0, 0, k))
    wspec = pl.BlockSpec((2, None, LRU_BLOCK, LRU_BLOCK), lambda b, k: (0, k, 0, 0))
    in_specs = [pl.BlockSpec((rows, LANE), lambda b, k: (b, k)),
                pl.BlockSpec((rows, LANE), lambda b, k: (b, kb + k)),
                vec(LRU_CONV), vec(1), wspec, vec2, wspec, vec2, vec2]
    args = [proj, proj, conv_w, conv_b, w_r, b_r, w_i, b_i, lam]
    o_shape = jax.ShapeDtypeStruct((nb * n, kb * LANE), BF16)
    o_spec = pl.BlockSpec((rows, LANE), lambda b, k: (b, k))
    if init is not None:
        in_specs.append(pl.BlockSpec((bt, None, 2, 1, LANE), lambda b, k: (b, layer, 0, 0, k)))
        args.append(init)
        out_shape, out_specs = o_shape, o_spec
    else:
        out_shape = (o_shape, jax.ShapeDtypeStruct((nb, 2, 1, kb * LANE), F32))
        out_specs = (o_spec, pl.BlockSpec((bt, 2, 1, LANE), lambda b, k: (b, 0, 0, k)))
    return pl.pallas_call(
        functools.partial(_lru_kernel, has_init=init is not None, seq=n),
        grid=(nb // bt, kb),
        in_specs=in_specs,
        out_specs=out_specs,
        out_shape=out_shape,
        scratch_shapes=[pltpu.VMEM((rows + 2 * SUBLANE, LANE), F32),
                        pltpu.VMEM((2, rows, LANE), F32), pltpu.VMEM((2, rows, LANE), F32)]
                       + [pltpu.VMEM((2, 3, size // radix, LANE), F32) for size, radix in _scan_levels(n)[0]],
        compiler_params=_params(("arbitrary", "arbitrary"), 2 * 2 * rows * LANE * 4 + 12 * rows * LANE * 4),
    )(*args)


def kernel(x_prompt, x_sample, cache_attn_k, cache_attn_v, state_gla, state_lru, c, c_ctx, w_ada, b_ada, norm1_g, w_in, attn_sink, gla_w_decay, gla_b_decay, gla_norm_g, lru_conv_w, lru_conv_b, lru_w_rgate, lru_b_rgate, lru_w_igate, lru_b_igate, lru_lambda, w_out, norm2_g, w_gu, w_down, final_norm_g):
    nb_c, s_len, d = x_prompt.shape
    nb_l, n_lat, _ = x_sample.shape
    depth = w_in.shape[0]
    lru_w = lru_conv_w.shape[2]
    qw, kw = ATT_HEADS * HEAD_DIM, ATT_KV_HEADS * HEAD_DIM
    gdec0 = qw + 2 * kw + 4 * GLA_HEADS * GLA_DK
    gdec1 = gdec0 + 2 * GLA_LOWRANK
    in_tile = 512
    tail_w = 2 * lru_w + 2 * GLA_LOWRANK
    tail_pad = -(-tail_w // in_tile) * in_tile
    assert gdec0 % in_tile == 0 and w_in.shape[2] == gdec0 + tail_w

    w_in_b = w_in.astype(BF16)
    w_tail_b = jnp.concatenate(
        [w_in_b[:, :, gdec1:], w_in_b[:, :, gdec0:gdec1],
         jnp.zeros((depth, d, tail_pad - tail_w), BF16)], axis=2)
    dec_col = 2 * lru_w // LANE
    w_out_b = w_out.astype(BF16)
    w_gu_b = w_gu.astype(BF16)
    w_down_b = w_down.astype(BF16)

    cvecs = jnp.concatenate([c_ctx[None], c, jnp.zeros((MOD_ROWS - 1 - nb_l, d), F32)], axis=0)
    modr = _ada(cvecs, w_ada, b_ada).reshape(depth, MOD_ROWS, 6, 1, d)
    row_ctx = _ModRows(0, nb_c * s_len)
    row_lat = _ModRows(1, n_lat)

    tables = _rope_tables(n_lat)
    cache_k = cache_attn_k.reshape(nb_l, depth, cache_attn_k.shape[2], kw)
    cache_v = cache_attn_v.reshape(nb_l, depth, cache_attn_v.shape[2], kw)
    lru_init = state_lru.reshape(nb_l, depth, 2, 1, lru_w)

    xc = x_prompt.reshape(nb_c * s_len, d)
    xl = x_sample.reshape(nb_l * n_lat, d)
    new_k, new_v, new_gla, new_lru = [], [], [], []
    for l in range(depth):
        wdec = gla_w_decay[l].reshape(2, GLA_LOWRANK, GLA_HEADS, GLA_DK).transpose(0, 2, 1, 3)
        zpad = jnp.zeros((GLA_HEADS, GLA_LOWRANK, GLA_DK), F32)
        zrest = jnp.zeros((GLA_HEADS, LANE - 2 * GLA_LOWRANK, GLA_DK), F32)
        wf = jnp.concatenate([wdec[0], zpad, zrest], axis=1).astype(BF16)
        wb = jnp.concatenate([zpad, wdec[1], zrest], axis=1).astype(BF16)
        b_dec = gla_b_decay[l].reshape(2, GLA_HEADS, 1, GLA_DK)
        ng = gla_norm_g[l].reshape(GLA_HEADS, 1, LANE)
        lru_args = (lru_conv_w[l], lru_conv_b[l].reshape(1, lru_w),
                    lru_w_rgate[l].astype(BF16), lru_b_rgate[l].reshape(2, 1, lru_w),
                    lru_w_igate[l].astype(BF16), lru_b_igate[l].reshape(2, 1, lru_w),
                    lru_lambda[l].reshape(2, 1, lru_w))
        sink = attn_sink[l]

        def ffn(x, mix, row_of):
            x = _out_proj(*mix, w_out_b, x, modr, l, row_of, 2)
            h2 = _norm_mod(x, norm2_g[l], modr, l, row_of, 4, 3)
            act = _gate_up(h2, w_gu_b, l)
            return _down_proj(act, w_down_b, x, modr, l, row_of, 5)

        h = _norm_mod(xc, norm1_g[l], modr, l, row_ctx, 1, 0)
        proj = _in_proj(h, w_in_b, l, gdec0)
        tail = _in_proj(h, w_tail_b, l, tail_pad)
        attn = _attn_ctx(proj, sink, nb_c, s_len)
        gla, gla_fin = _gla(proj, tail, dec_col, wf, wb, b_dec, ng, nb_c, s_len)
        lru, lru_fin = _lru(tail, *lru_args, nb_c, s_len)
        new_k.append(proj[:, qw:qw + kw].reshape(nb_c, s_len, ATT_KV_HEADS, HEAD_DIM))
        new_v.append(proj[:, qw + kw:qw + 2 * kw].reshape(nb_c, s_len, ATT_KV_HEADS, HEAD_DIM))
        new_gla.append(gla_fin)
        new_lru.append(lru_fin.reshape(nb_c, 2, lru_w))
        xc = ffn(xc, (attn, gla, lru), row_ctx)

        h = _norm_mod(xl, norm1_g[l], modr, l, row_lat, 1, 0)
        proj = _in_proj(h, w_in_b, l, gdec0)
        tail = _in_proj(h, w_tail_b, l, tail_pad)
        attn = _attn_lat(proj, sink, cache_k, cache_v, l, tables, nb_l, n_lat)
        gla = _gla(proj, tail, dec_col, wf, wb, b_dec, ng, nb_l, n_lat, init=state_gla, layer=l)
        lru = _lru(tail, *lru_args, nb_l, n_lat, init=lru_init, layer=l)
        xl = ffn(xl, (attn, gla, lru), row_lat)

    y_prompt = _final_norm(xc, final_norm_g).reshape(nb_c, s_len, d)
    y_sample = _final_norm(xl, final_norm_g).reshape(nb_l, n_lat, d)
    return (y_prompt, y_sample, jnp.stack(new_k, axis=1), jnp.stack(new_v, axis=1),
            jnp.stack(new_gla, axis=1), jnp.stack(new_lru, axis=1))
```

```python
import functools
from typing import NamedTuple

import jax
import jax.numpy as jnp
from jax import lax
from jax.experimental import pallas as pl
from jax.experimental.pallas import tpu as pltpu

F32 = jnp.float32
BF16 = jnp.bfloat16

EPS = 1e-6
HEAD_DIM = 128
ATT_HEADS = 16
ATT_KV_HEADS = 4
ATT_GROUPS = ATT_HEADS // ATT_KV_HEADS
WINDOW = 128
Q_BLOCK = 128
ATT_TILE_ROWS = 128
GRID_W = 64
ROPE_THETA = 10000.0
ROPE_FREQS = HEAD_DIM // 4
GLA_HEADS = 8
GLA_DK = 128
GLA_LOWRANK = 16
GLA_TAU = 16.0
GLA_CHUNK = 64
GLA_BULK_ROWS = 256
GLA_ELEMENTWISE_ROWS = 512
LRU_BLOCK = 128
LRU_CONV = 4
CONV_LEFT = 2
LRU_C = 8.0
MIXER_ROWS_PER_STEP = 1024
SCAN_TOP_ROWS = 8
SCAN_SLAB_ROWS = 64

LANE = 128
SUBLANE = 8
VMEM_BYTES_V7X = 64 * 2**20
MXU_COLS = 256
GATE_UP_SLAB_ROWS = 1024
MOD_ROWS = 16

LOG2_E = 1.4426950408889634
NT_DIMS = (((1,), (1,)), ((), ()))
TN_DIMS = (((0,), (0,)), ((), ()))


def _params(sem, est_bytes):
    limit = int(min(VMEM_BYTES_V7X - 4 * 2**20, max(est_bytes + 8 * 2**20, 32 * 2**20)))
    return pltpu.CompilerParams(dimension_semantics=sem, vmem_limit_bytes=limit)


def _seqs_per_step(nb, n):
    bt = max(1, MIXER_ROWS_PER_STEP // n)
    while nb % bt:
        bt -= 1
    return bt


def _softplus(x):
    return jnp.maximum(x, 0.0) + jnp.log1p(jnp.exp(-jnp.abs(x)))


def _log_sigmoid(x):
    return -_softplus(-x)


def _sigmoid(x):
    return 0.5 * jnp.tanh(0.5 * x) + 0.5


def _ada_kernel(c_ref, w_ref, b_ref, o_ref):
    cv = c_ref[...]
    s = (cv * jax.nn.sigmoid(cv)).astype(BF16)
    o_ref[...] = jnp.dot(s, w_ref[...].astype(BF16), preferred_element_type=F32) + b_ref[...]


def _ada(cvecs, w_ada, b_ada):
    nl, d, n6 = w_ada.shape
    tn = 512
    return pl.pallas_call(
        _ada_kernel,
        grid=(nl, n6 // tn),
        in_specs=[pl.BlockSpec((MOD_ROWS, d), lambda l, j: (0, 0)),
                  pl.BlockSpec((None, d, tn), lambda l, j: (l, 0, j)),
                  pl.BlockSpec((None, 1, tn), lambda l, j: (l, 0, j))],
        out_specs=pl.BlockSpec((None, MOD_ROWS, tn), lambda l, j: (l, 0, j)),
        out_shape=jax.ShapeDtypeStruct((nl, MOD_ROWS, n6), F32),
        compiler_params=_params(("arbitrary", "arbitrary"), 2 * d * tn * 4 + d * tn * 2),
    )(cvecs, w_ada, b_ada.reshape(nl, 1, n6))


def _norm_mod_kernel(x_ref, g_ref, sc_ref, sh_ref, o_ref):
    x = x_ref[...]
    y = x * lax.rsqrt(jnp.mean(x * x, axis=-1, keepdims=True) + EPS) * g_ref[...]
    o_ref[...] = (y * (1.0 + sc_ref[...]) + sh_ref[...]).astype(o_ref.dtype)


def _norm_kernel(x_ref, g_ref, o_ref):
    x = x_ref[...]
    y = x * lax.rsqrt(jnp.mean(x * x, axis=-1, keepdims=True) + EPS) * g_ref[...]
    o_ref[...] = y.astype(o_ref.dtype)


class _ModRows(NamedTuple):
    first: int
    segment: int

    def tile(self, m, want):
        tm = min(want, m, self.segment)
        assert m % tm == 0 and self.segment % tm == 0
        return tm

    def __call__(self, row0):
        return self.first + row0 // self.segment


def _norm_mod(x, g, modr, layer, row_of, sc_idx, sh_idx, tr=256):
    m, d = x.shape
    tr = row_of.tile(m, tr)
    mspec = lambda k: pl.BlockSpec((None, None, None, 1, d), lambda i: (layer, row_of(i * tr), k, 0, 0))
    return pl.pallas_call(
        _norm_mod_kernel,
        grid=(m // tr,),
        in_specs=[pl.BlockSpec((tr, d), lambda i: (i, 0)),
                  pl.BlockSpec((1, d), lambda i: (0, 0)),
                  mspec(sc_idx), mspec(sh_idx)],
        out_specs=pl.BlockSpec((tr, d), lambda i: (i, 0)),
        out_shape=jax.ShapeDtypeStruct((m, d), BF16),
        compiler_params=_params(("arbitrary",), 2 * tr * d * 6 + 2 * tr * d * 4),
    )(x, g.reshape(1, d), modr, modr)


def _final_norm(x, g, tr=256):
    m, d = x.shape
    return pl.pallas_call(
        _norm_kernel,
        grid=(m // tr,),
        in_specs=[pl.BlockSpec((tr, d), lambda i: (i, 0)),
                  pl.BlockSpec((1, d), lambda i: (0, 0))],
        out_specs=pl.BlockSpec((tr, d), lambda i: (i, 0)),
        out_shape=jax.ShapeDtypeStruct((m, d), F32),
        compiler_params=_params(("arbitrary",), 2 * tr * d * 8 + 2 * tr * d * 4),
    )(x, g.reshape(1, d))


def _mm_kernel(x_ref, w_ref, o_ref):
    o_ref[...] = jnp.dot(x_ref[...], w_ref[...], preferred_element_type=F32)


def _in_proj(h, w, layer, n, tm=2048, tn=512):
    m, k = h.shape
    assert n % tn == 0 and n <= w.shape[2]
    tm = min(tm, m)
    return pl.pallas_call(
        _mm_kernel,
        grid=(m // tm, n // tn),
        in_specs=[pl.BlockSpec((tm, k), lambda i, j: (i, 0)),
                  pl.BlockSpec((None, k, tn), lambda i, j: (layer, 0, j))],
        out_specs=pl.BlockSpec((tm, tn), lambda i, j: (i, j)),
        out_shape=jax.ShapeDtypeStruct((m, n), F32),
        compiler_params=_params(("arbitrary", "arbitrary"), 2 * (tm * k * 2 + k * tn * 2 + tm * tn * 4)),
    )(h, w)


def _out_proj_kernel(a_ref, b_ref, c_ref, w_ref, x_ref, g_ref, o_ref):
    ka, kb = a_ref.shape[1], b_ref.shape[1]
    for j0 in range(0, o_ref.shape[1], MXU_COLS):
        cols = slice(j0, j0 + MXU_COLS)
        acc = jnp.dot(a_ref[...], w_ref[0:ka, cols], preferred_element_type=F32)
        acc += jnp.dot(b_ref[...], w_ref[ka:ka + kb, cols], preferred_element_type=F32)
        acc += jnp.dot(c_ref[...], w_ref[ka + kb:, cols], preferred_element_type=F32)
        o_ref[:, cols] = x_ref[:, cols] + g_ref[:, cols] * acc


def _out_proj(attn, gla, lru, w, x, modr, layer, row_of, gate_idx, tm=1024, tn=1024):
    m, d = x.shape
    k = w.shape[1]
    tm = row_of.tile(m, tm)
    xs = lambda a: pl.BlockSpec((tm, a.shape[1]), lambda i, j: (i, 0))
    return pl.pallas_call(
        _out_proj_kernel,
        grid=(m // tm, d // tn),
        in_specs=[xs(attn), xs(gla), xs(lru),
                  pl.BlockSpec((None, k, tn), lambda i, j: (layer, 0, j)),
                  pl.BlockSpec((tm, tn), lambda i, j: (i, j)),
                  pl.BlockSpec((None, None, None, 1, tn),
                               lambda i, j: (layer, row_of(i * tm), gate_idx, 0, j))],
        out_specs=pl.BlockSpec((tm, tn), lambda i, j: (i, j)),
        out_shape=jax.ShapeDtypeStruct((m, d), F32),
        compiler_params=_params(("arbitrary", "arbitrary"),
                                2 * (tm * k * 2 + k * tn * 2 + 2 * tm * tn * 4) + tm * tn * 4),
    )(attn, gla, lru, w, x, modr)


def _gate_up_kernel(h_ref, wg_ref, wu_ref, o_ref):
    tm = h_ref.shape[0]
    slab = min(tm, GATE_UP_SLAB_ROWS)
    for r0 in range(0, tm, slab):
        h = h_ref[r0:r0 + slab, :]
        a = jnp.dot(h, wg_ref[...], preferred_element_type=F32)
        u = jnp.dot(h, wu_ref[...], preferred_element_type=F32)
        o_ref[r0:r0 + slab, :] = (a * _sigmoid(a) * u).astype(o_ref.dtype)


def _gate_up(h, w, layer, tm=2048, tn=256):
    m, k = h.shape
    f = w.shape[2] // 2
    tm = min(tm, m)
    nj = f // tn
    return pl.pallas_call(
        _gate_up_kernel,
        grid=(m // tm, nj),
        in_specs=[pl.BlockSpec((tm, k), lambda i, j: (i, 0)),
                  pl.BlockSpec((None, k, tn), lambda i, j: (layer, 0, j)),
                  pl.BlockSpec((None, k, tn), lambda i, j: (layer, 0, nj + j))],
        out_specs=pl.BlockSpec((tm, tn), lambda i, j: (i, j)),
        out_shape=jax.ShapeDtypeStruct((m, f), BF16),
        compiler_params=_params(("arbitrary", "arbitrary"),
                                2 * (tm * k * 2 + 2 * k * tn * 2 + tm * tn * 2) + 3 * tm * tn * 4),
    )(h, w, w)


def _down_kernel(a_ref, w_ref, x_ref, g_ref, o_ref):
    for j0 in range(0, o_ref.shape[1], MXU_COLS):
        cols = slice(j0, j0 + MXU_COLS)
        acc = jnp.dot(a_ref[...], w_ref[:, cols], preferred_element_type=F32)
        o_ref[:, cols] = x_ref[:, cols] + g_ref[:, cols] * acc


def _down_proj(act, w, x, modr, layer, row_of, gate_idx, tm=512, tn=512):
    m, d = x.shape
    k = w.shape[1]
    tm = row_of.tile(m, tm)
    return pl.pallas_call(
        _down_kernel,
        grid=(m // tm, d // tn),
        in_specs=[pl.BlockSpec((tm, k), lambda i, j: (i, 0)),
                  pl.BlockSpec((None, k, tn), lambda i, j: (layer, 0, j)),
                  pl.BlockSpec((tm, tn), lambda i, j: (i, j)),
                  pl.BlockSpec((None, None, None, 1, tn),
                               lambda i, j: (layer, row_of(i * tm), gate_idx, 0, j))],
        out_specs=pl.BlockSpec((tm, tn), lambda i, j: (i, j)),
        out_shape=jax.ShapeDtypeStruct((m, d), F32),
        compiler_params=_params(("arbitrary", "arbitrary"),
                                2 * (tm * k * 2 + k * tn * 2 + 2 * tm * tn * 4) + tm * tn * 4),
    )(act, w, x, modr)


def _softmax_pv(s_parts, sk, scale, value_of):
    m = functools.reduce(jnp.maximum, s_parts)
    m2 = jnp.maximum(jnp.max(m, axis=-1, keepdims=True) * scale, sk) * LOG2_E
    p_parts = [jnp.exp2(s * (scale * LOG2_E) - m2) for s in s_parts]
    denom = (jnp.sum(functools.reduce(jnp.add, p_parts), axis=-1, keepdims=True)
             + jnp.exp2(sk * LOG2_E - m2))
    o = None
    for j, p in enumerate(p_parts):
        ov = jnp.dot(p.astype(BF16), value_of(j), preferred_element_type=F32)
        o = ov if o is None else o + ov
    return o, 1.0 / denom


def _attn_ctx_kernel(sink_ref, q_ref, k_ref, v_ref, o_ref):
    scale = HEAD_DIM ** -0.5
    head = lambda ref, h: ref[:, h * HEAD_DIM:(h + 1) * HEAD_DIM]
    kt = [head(k_ref, kh).T.astype(BF16) for kh in range(ATT_KV_HEADS)]
    vv = [head(v_ref, kh).astype(BF16) for kh in range(ATT_KV_HEADS)]

    def scores(h):
        s = jnp.dot(head(q_ref, h).astype(BF16), kt[h // ATT_GROUPS], preferred_element_type=F32)
        return [s[:, j:j + LANE] for j in range(0, s.shape[1], LANE)]

    s_next = scores(0)
    for h in range(ATT_HEADS):
        s_parts = s_next
        if h + 1 < ATT_HEADS:
            s_next = scores(h + 1)
        v = vv[h // ATT_GROUPS]
        o, r = _softmax_pv(s_parts, sink_ref[h], scale, lambda j: v[j * LANE:(j + 1) * LANE])
        o_ref[:, h * HEAD_DIM:(h + 1) * HEAD_DIM] = (o * r).astype(o_ref.dtype)


def _attn_ctx(proj, sink, nb, s_len):
    qw = ATT_HEADS * HEAD_DIM
    kw = ATT_KV_HEADS * HEAD_DIM
    return pl.pallas_call(
        _attn_ctx_kernel,
        grid=(nb,),
        in_specs=[pl.BlockSpec(memory_space=pltpu.SMEM),
                  pl.BlockSpec((s_len, qw), lambda b: (b, 0)),
                  pl.BlockSpec((s_len, kw), lambda b: (b, qw // kw)),
                  pl.BlockSpec((s_len, kw), lambda b: (b, qw // kw + 1))],
        out_specs=pl.BlockSpec((s_len, qw), lambda b: (b, 0)),
        out_shape=jax.ShapeDtypeStruct((nb * s_len, qw), BF16),
        compiler_params=_params(("arbitrary",), 2 * s_len * (qw + 2 * kw) * 4 + 2 * s_len * qw * 2
                                + 8 * ATT_GROUPS * s_len * s_len * 4),
    )(sink, proj, proj, proj)


def _attn_lat_kernel(sink_ref, q_ref, k_ref, v_ref, kc_ref, vc_ref, cos_ref, sa_ref, sb_ref,
                     o_ref, kt_s, vb_s, qr_s):
    n = q_ref.shape[0]
    nqb = n // Q_BLOCK
    kh = pl.program_id(1)
    scale = HEAD_DIM ** -0.5
    wblocks = (Q_BLOCK + 2 * WINDOW) // Q_BLOCK

    def rope(x, rows):
        return (x * cos_ref[rows, :] + pltpu.roll(x, HEAD_DIM - ROPE_FREQS, 1) * sa_ref[rows, :]
                + pltpu.roll(x, ROPE_FREQS, 1) * sb_ref[rows, :])

    def prep(j, carry):
        rows = pl.ds(pl.multiple_of(j * Q_BLOCK, Q_BLOCK), Q_BLOCK)
        kt_s[j] = rope(k_ref[rows, :], rows).T.astype(BF16)
        for g in range(ATT_GROUPS):
            lanes = slice(g * HEAD_DIM, (g + 1) * HEAD_DIM)
            qr_s[rows, lanes] = rope(q_ref[rows, lanes], rows).astype(BF16)
        return carry

    lax.fori_loop(0, nqb, prep, 0)
    vb_s[...] = v_ref[...].astype(BF16)
    kct = kc_ref[...].T.astype(BF16)
    vc = vc_ref[...].astype(BF16)
    qi = lax.broadcasted_iota(jnp.int32, (Q_BLOCK, Q_BLOCK), 0)
    ki = lax.broadcasted_iota(jnp.int32, (Q_BLOCK, Q_BLOCK), 1)
    rel = qi - ki

    def block(i, carry):
        qs = pl.multiple_of(i * Q_BLOCK, Q_BLOCK)
        kb = jnp.clip(i - WINDOW // Q_BLOCK, 0, nqb - wblocks)
        biases = []
        for j in range(wblocks):
            off = (i - kb - j) * Q_BLOCK
            biases.append(jnp.where(jnp.abs(rel + off) <= WINDOW, 0.0, -jnp.inf).astype(F32))

        tr = ATT_TILE_ROWS
        tiles = [(g, r0) for g in range(ATT_GROUPS) for r0 in range(0, Q_BLOCK, tr)]

        def scores(g, r0):
            q = qr_s[pl.ds(qs + r0, tr), g * HEAD_DIM:(g + 1) * HEAD_DIM]
            parts = [jnp.dot(q, kt_s[kb + j], preferred_element_type=F32) + biases[j][r0:r0 + tr]
                     for j in range(wblocks)]
            s2 = jnp.dot(q, kct, preferred_element_type=F32)
            return parts + [s2[:, j * LANE:(j + 1) * LANE] for j in range(s2.shape[1] // LANE)]

        s_next = scores(*tiles[0])
        for t, (g, r0) in enumerate(tiles):
            s_parts = s_next
            if t + 1 < len(tiles):
                s_next = scores(*tiles[t + 1])
            o, r = _softmax_pv(s_parts, sink_ref[kh * ATT_GROUPS + g], scale, lambda j: (
                vb_s[pl.ds(pl.multiple_of((kb + j) * Q_BLOCK, Q_BLOCK), Q_BLOCK), :] if j < wblocks
                else vc[(j - wblocks) * LANE:(j - wblocks + 1) * LANE, :]))
            o_ref[pl.ds(qs + r0, tr), g * HEAD_DIM:(g + 1) * HEAD_DIM] = (o * r).astype(o_ref.dtype)
        return carry

    lax.fori_loop(0, nqb, block, 0, unroll=2)


def _attn_lat(proj, sink, cache_k, cache_v, layer, tables, nb, n):
    qw = ATT_HEADS * HEAD_DIM
    gw = ATT_GROUPS * HEAD_DIM
    past = cache_k.shape[2]
    kcol = qw // HEAD_DIM
    vcol = kcol + ATT_KV_HEADS
    cspec = pl.BlockSpec((None, None, past, HEAD_DIM), lambda b, h: (b, layer, 0, h))
    tspec = pl.BlockSpec((n, HEAD_DIM), lambda b, h: (0, 0))
    return pl.pallas_call(
        _attn_lat_kernel,
        grid=(nb, ATT_KV_HEADS),
        in_specs=[pl.BlockSpec(memory_space=pltpu.SMEM),
                  pl.BlockSpec((n, gw), lambda b, h: (b, h)),
                  pl.BlockSpec((n, HEAD_DIM), lambda b, h: (b, kcol + h)),
                  pl.BlockSpec((n, HEAD_DIM), lambda b, h: (b, vcol + h)),
                  cspec, cspec, tspec, tspec, tspec],
        out_specs=pl.BlockSpec((n, gw), lambda b, h: (b, h)),
        out_shape=jax.ShapeDtypeStruct((nb * n, qw), BF16),
        scratch_shapes=[pltpu.VMEM((n // Q_BLOCK, HEAD_DIM, Q_BLOCK), BF16),
                        pltpu.VMEM((n, HEAD_DIM), BF16), pltpu.VMEM((n, gw), BF16)],
        compiler_params=_params(("arbitrary", "arbitrary"),
                                2 * n * (gw + 2 * HEAD_DIM) * 4 + 2 * n * gw * 2 + 6 * n * HEAD_DIM * 4
                                + 4 * n * HEAD_DIM + n * gw * 2),
    )(sink, proj, proj, proj, cache_k, cache_v, *tables)


def _rope_tables(n):
    pos = jnp.arange(n)
    row = (pos // GRID_W).astype(F32)
    col = (pos % GRID_W).astype(F32)
    inv = ROPE_THETA ** (-jnp.arange(ROPE_FREQS, dtype=F32) / ROPE_FREQS)
    ang_r = row[:, None] * inv[None, :]
    ang_c = col[:, None] * inv[None, :]
    cr, sr, cc, sc = jnp.cos(ang_r), jnp.sin(ang_r), jnp.cos(ang_c), jnp.sin(ang_c)
    z = jnp.zeros_like(cr)
    cos = jnp.concatenate([cr, cr, cc, cc], axis=1)
    sa = jnp.concatenate([-sr, z, -sc, z], axis=1)
    sb = jnp.concatenate([z, sr, z, sc], axis=1)
    return cos, sa, sb


def _gla_kernel(*refs, has_init, seq):
    if has_init:
        (q_ref, k_ref, v_ref, g_ref, dec_ref, wf_ref, wb_ref, bd_ref, ng_ref, s0_ref,
         o_ref, qin_s, tot_s, o_s, kv_s, sin_s) = refs
        sfin_ref = None
    else:
        (q_ref, k_ref, v_ref, g_ref, dec_ref, wf_ref, wb_ref, bd_ref, ng_ref,
         o_ref, sfin_ref, qin_s, tot_s, o_s, kv_s, sin_s) = refs
    n = q_ref.shape[0]
    c = GLA_CHUNK
    nc = n // c
    blk = min(seq, GLA_BULK_ROWS)
    qscale = GLA_DK ** -0.5
    shift = c.bit_length() - 1
    ri = lax.broadcasted_iota(jnp.int32, (blk, blk), 0)
    ci = lax.broadcasted_iota(jnp.int32, (blk, blk), 1)
    same = lax.shift_right_logical(ri, shift) == lax.shift_right_logical(ci, shift)
    keep = (same & (ri >= ci), same & (ri <= ci))
    sums = tuple(jnp.concatenate([keep[d].astype(BF16), same.astype(BF16)], axis=0) for d in range(2))
    wdec = jnp.concatenate([wf_ref[...], wb_ref[...]], axis=1)
    bdec = jnp.concatenate([bd_ref[0], bd_ref[1]], axis=1)

    def split3(x):
        hi = x.astype(BF16)
        r1 = x - hi.astype(F32)
        mid = r1.astype(BF16)
        return jnp.concatenate([hi, mid, (r1 - mid.astype(F32)).astype(BF16)], axis=1)

    def fold3(y):
        y = (y[:, :LANE] + y[:, LANE:2 * LANE]) + y[:, 2 * LANE:]
        return y[:blk], y[blk:]

    def chain(rows, chunk0, d, z, qc, kc, vc):
        st = {}

        def log_decay():
            st["p"] = split3(_log_sigmoid(z[:, d * GLA_DK:(d + 1) * GLA_DK]) * (1.0 / GLA_TAU))

        def chunk_sums():
            st["y"] = jnp.dot(sums[d], st["p"], preferred_element_type=F32)

        def decayed():
            cum, tot = fold3(st["y"])
            tot_s[d, rows, :] = tot
            st["q"] = (qc * jnp.exp(cum)).astype(BF16)
            st["k"] = (kc * jnp.exp(-cum)).astype(BF16)
            qin_s[d, rows, :] = st["q"]
            st["kd"] = (kc * jnp.exp(tot - cum)).astype(BF16)

        def scores():
            st["a"] = lax.dot_general(st["q"], st["k"], NT_DIMS, preferred_element_type=F32)

        def mask():
            st["att"] = jnp.where(keep[d], st["a"], 0.0).astype(BF16)

        def within():
            o_s[d, rows, :] = jnp.dot(st["att"], vc, preferred_element_type=F32)
            for u in range(blk // c):
                cr = slice(u * c, (u + 1) * c)
                kv_s[d, chunk0 + u] = lax.dot_general(vc[cr], st["kd"][cr], TN_DIMS,
                                                      preferred_element_type=F32)

        return [log_decay, chunk_sums, decayed, scores, mask, within]

    per_step = 2 if (n // blk) % 2 == 0 else 1

    def bulk(i, carry):
        chains = []
        for b in range(per_step):
            block = i * per_step + b
            rows = pl.ds(pl.multiple_of(block * blk, blk), blk)
            z = jnp.dot(dec_ref[rows, :].astype(BF16), wdec, preferred_element_type=F32) + bdec
            vc = v_ref[rows, :].astype(BF16)
            chains += [chain(rows, block * (blk // c), d, z, q_ref[rows, :] * qscale, k_ref[rows, :], vc)
                       for d in range(2)]
        nstage = len(chains[0])
        for t in range(nstage + len(chains) - 1):
            for lag, stages in enumerate(chains):
                if 0 <= t - lag < nstage:
                    stages[t - lag]()
        return carry

    lax.fori_loop(0, n // (blk * per_step), bulk, 0)

    ncs = seq // c
    for s in range(n // seq):
        def scan(j, carry, base=s * ncs):
            sf, sb = carry
            jf = base + j
            jb = base + ncs - 1 - j
            sin_s[0, jf] = sf.astype(BF16)
            sin_s[1, jb] = sb.astype(BF16)
            sf = jnp.exp(tot_s[0, pl.ds(jf * c, 1), :]) * sf + kv_s[0, jf]
            sb = jnp.exp(tot_s[1, pl.ds(jb * c, 1), :]) * sb + kv_s[1, jb]
            return sf, sb

        if has_init:
            init = (s0_ref[s, 0].T, s0_ref[s, 1].T)
        else:
            init = (jnp.zeros((LANE, GLA_DK), F32), jnp.zeros((LANE, GLA_DK), F32))
        sf, sb = lax.fori_loop(0, ncs, scan, init)
        if sfin_ref is not None:
            sfin_ref[s, 0] = sf.T
            sfin_ref[s, 1] = sb.T

    def carried(j, carry):
        rows = pl.ds(pl.multiple_of(j * c, c), c)
        o_s[0, rows, :] = ((o_s[0, rows, :] + lax.dot_general(qin_s[0, rows, :], sin_s[0, j], NT_DIMS,
                                                              preferred_element_type=F32))
                           + (o_s[1, rows, :] + lax.dot_general(qin_s[1, rows, :], sin_s[1, j], NT_DIMS,
                                                                preferred_element_type=F32)))
        return carry

    lax.fori_loop(0, nc, carried, 0, unroll=min(nc, 4))

    ew = min(n, GLA_ELEMENTWISE_ROWS)

    def finish(i, carry):
        rows = pl.ds(pl.multiple_of(i * ew, ew), ew)
        o = o_s[0, rows, :]
        o = o * lax.rsqrt(jnp.mean(o * o, axis=-1, keepdims=True) + EPS) * ng_ref[...]
        g = g_ref[rows, :]
        o_ref[rows, :] = (o * (g * _sigmoid(g))).astype(o_ref.dtype)
        return carry

    lax.fori_loop(0, n // ew, finish, 0)


def _gla(proj, proj_tail, dec_col, wf, wb, b_dec, norm_g, nb, n, init=None, layer=0):
    hh = GLA_HEADS
    q0 = (ATT_HEADS + 2 * ATT_KV_HEADS) * HEAD_DIM // LANE
    bt = _seqs_per_step(nb, n)
    rows = bt * n
    nc = rows // GLA_CHUNK
    col = lambda base: pl.BlockSpec((rows, LANE), lambda b, h: (b, base + h))
    in_specs = [col(q0), col(q0 + hh), col(q0 + 2 * hh), col(q0 + 3 * hh),
                pl.BlockSpec((rows, LANE), lambda b, h: (b, dec_col)),
                pl.BlockSpec((None, LANE, LANE), lambda b, h: (h, 0, 0)),
                pl.BlockSpec((None, LANE, LANE), lambda b, h: (h, 0, 0)),
                pl.BlockSpec((2, None, 1, LANE), lambda b, h: (0, h, 0, 0)),
                pl.BlockSpec((None, 1, LANE), lambda b, h: (h, 0, 0))]
    args = [proj, proj, proj, proj, proj_tail, wf, wb, b_dec, norm_g]
    o_shape = jax.ShapeDtypeStruct((nb * n, hh * LANE), BF16)
    o_spec = pl.BlockSpec((rows, LANE), lambda b, h: (b, h))
    if init is not None:
        in_specs.append(pl.BlockSpec((bt, None, 2, None, GLA_DK, LANE),
                                     lambda b, h: (b, layer, 0, h, 0, 0)))
        args.append(init)
        out_shape, out_specs = o_shape, o_spec
    else:
        out_shape = (o_shape, jax.ShapeDtypeStruct((nb, 2, hh, GLA_DK, LANE), F32))
        out_specs = (o_spec, pl.BlockSpec((bt, 2, None, GLA_DK, LANE), lambda b, h: (b, 0, h, 0, 0)))
    return pl.pallas_call(
        functools.partial(_gla_kernel, has_init=init is not None, seq=n),
        grid=(nb // bt, hh),
        in_specs=in_specs,
        out_specs=out_specs,
        out_shape=out_shape,
        scratch_shapes=[pltpu.VMEM((2, rows, GLA_DK), BF16),
                        pltpu.VMEM((2, rows, GLA_DK), F32),
                        pltpu.VMEM((2, rows, LANE), F32),
                        pltpu.VMEM((2, nc, LANE, GLA_DK), F32),
                        pltpu.VMEM((2, nc, LANE, GLA_DK), BF16)],
        compiler_params=_params(("arbitrary", "arbitrary"),
                                2 * 5 * rows * LANE * 4 + 2 * rows * LANE * 2 + rows * LANE * (2 + 8 + 16)
                                + nc * LANE * GLA_DK * 12 + 4 * 2**20),
    )(*args)


def _scan_levels(n):
    levels = []
    size = n
    while size > SCAN_TOP_ROWS:
        radix = 4 if not levels else 8
        levels.append((size, radix))
        size //= radix
    return levels, size


def _linear_scan(a, u, lv, n, h0, rev):
    levels, top = _scan_levels(n)

    def sweep(a_l, u_l, size, radix, body):
        m = size // radix
        sb = min(m, SCAN_SLAB_ROWS)
        first = radix - 1 if rev else 0
        order = list(range(radix - 2, -1, -1)) if rev else list(range(1, radix))

        def blk(j, carry):
            sl = lambda r: pl.ds(j * (sb * radix) + r, sb, stride=radix)
            body(sl, pl.ds(pl.multiple_of(j * sb, sb), sb), first, order)
            return carry

        if m // sb == 1:
            blk(0, 0)
        else:
            lax.fori_loop(0, m // sb, blk, 0)

    src_a, src_u = a, u
    for l, (size, radix) in enumerate(levels):
        nxt = lv[l]

        def up(sl, dense, first, order, src_a=src_a, src_u=src_u, nxt=nxt):
            pa = src_a[sl(first), :]
            pu = src_u[sl(first), :]
            for r in order:
                ar = src_a[sl(r), :]
                pu = ar * pu + src_u[sl(r), :]
                pa = ar * pa
                src_a[sl(r), :] = pa
                src_u[sl(r), :] = pu
            nxt[0, dense, :] = pa
            nxt[1, dense, :] = pu

        sweep(src_a, src_u, size, radix, up)
        src_a, src_u = nxt.at[0], nxt.at[1]

    carry = h0
    top_c = lv[len(levels) - 1].at[2]
    for g in (range(top - 1, -1, -1) if rev else range(top)):
        top_c[g:g + 1, :] = carry
        carry = src_a[g:g + 1, :] * carry + src_u[g:g + 1, :]

    for l in range(len(levels) - 1, -1, -1):
        size, radix = levels[l]
        a_l, u_l = (a, u) if l == 0 else (lv[l - 1].at[0], lv[l - 1].at[1])
        c_l = lv[l].at[2]
        c_below = None if l == 0 else lv[l - 1].at[2]

        def down(sl, dense, first, order, a_l=a_l, u_l=u_l, c_l=c_l, c_below=c_below):
            cin = c_l[dense, :]
            prev = cin
            for r in [first] + order:
                h = u_l[sl(r), :] + a_l[sl(r), :] * cin
                if c_below is None:
                    u_l[sl(r), :] = h
                else:
                    c_below[sl(r), :] = prev
                prev = h

        sweep(a_l, u_l, size, radix, down)
    return carry


def _lru_kernel(*refs, has_init, seq):
    if has_init:
        (x_ref, y_ref, cw_ref, cb_ref, wr_ref, br_ref, wi_ref, bi_ref, lam_ref, h0_ref,
         o_ref, xp_s, a_s, u_s, *lv_s) = refs
        hfin_ref = None
    else:
        (x_ref, y_ref, cw_ref, cb_ref, wr_ref, br_ref, wi_ref, bi_ref, lam_ref,
         o_ref, hfin_ref, xp_s, a_s, u_s, *lv_s) = refs
    n = x_ref.shape[0]
    nseq = n // seq
    pad = SUBLANE
    xp_s[0:pad, :] = jnp.zeros((pad, LANE), F32)
    xp_s[pad + n:pad + n + pad, :] = jnp.zeros((pad, LANE), F32)
    xp_s[pad:pad + n, :] = x_ref[...]
    t = lax.broadcasted_iota(jnp.int32, (n, 1), 0)
    xc = cb_ref[...]
    for j in range(LRU_CONV):
        off = j - CONV_LEFT
        tap = xp_s[pl.ds(pad + off, n), :]
        edge = [s * seq + e for s in range(1, nseq) for e in range(0, -off)] if off < 0 else \
               [s * seq - 1 - e for s in range(1, nseq) for e in range(0, off)]
        if edge:
            tap = jnp.where(functools.reduce(jnp.logical_or, [t == e for e in edge]), 0.0, tap)
        xc = xc + tap * cw_ref[j:j + 1, :]
    xcb = xc.astype(BF16)
    for d in range(2):
        r = _sigmoid(jnp.dot(xcb, wr_ref[d], preferred_element_type=F32) + br_ref[d])
        i = _sigmoid(jnp.dot(xcb, wi_ref[d], preferred_element_type=F32) + bi_ref[d])
        log_a = -LRU_C * r * _softplus(-lam_ref[d])
        a = jnp.exp(log_a)
        a_s[d] = a
        v = -jnp.tanh(log_a) * (a * a + 1.0)
        u_s[d] = jnp.where(v > 0.0, v * lax.rsqrt(v), 0.0) * (i * xc)

    for s in range(nseq):
        rows = pl.ds(s * seq, seq)
        for d in range(2):
            h0 = h0_ref[s, d] if has_init else jnp.zeros((1, LANE), F32)
            h_last = _linear_scan(a_s.at[d, rows], u_s.at[d, rows], [lv.at[d] for lv in lv_s], seq, h0,
                                  rev=d == 1)
            if hfin_ref is not None:
                hfin_ref[s, d] = h_last
    o_ref[...] = ((u_s[0] + u_s[1]) * jax.nn.gelu(y_ref[...])).astype(o_ref.dtype)


def _lru(proj, conv_w, conv_b, w_r, b_r, w_i, b_i, lam, nb, n, init=None, layer=0):
    kb = conv_w.shape[1] // LRU_BLOCK
    bt = _seqs_per_step(nb, n)
    rows = bt * n
    vec = lambda r: pl.BlockSpec((r, LANE), lambda b, k: (0, k))
    vec2 = pl.BlockSpec((2, 1, LANE), lambda b, k: (0, 0, k))
    wspec = pl.BlockSpec((2, None, LRU_BLOCK, LRU_BLOCK), lambda b, k: (0, k, 0, 0))
    in_specs = [pl.BlockSpec((rows, LANE), lambda b, k: (b, k)),
                pl.BlockSpec((rows, LANE), lambda b, k: (b, kb + k)),
                vec(LRU_CONV), vec(1), wspec, vec2, wspec, vec2, vec2]
    args = [proj, proj, conv_w, conv_b, w_r, b_r, w_i, b_i, lam]
    o_shape = jax.ShapeDtypeStruct((nb * n, kb * LANE), BF16)
    o_spec = pl.BlockSpec((rows, LANE), lambda b, k: (b, k))
    if init is not None:
        in_specs.append(pl.BlockSpec((bt, None, 2, 1, LANE), lambda b, k: (b, layer, 0, 0, k)))
        args.append(init)
        out_shape, out_specs = o_shape, o_spec
    else:
        out_shape = (o_shape, jax.ShapeDtypeStruct((nb, 2, 1, kb * LANE), F32))
        out_specs = (o_spec, pl.BlockSpec((bt, 2, 1, LANE), lambda b, k: (b, 0, 0, k)))
    return pl.pallas_call(
        functools.partial(_lru_kernel, has_init=init is not None, seq=n),
        grid=(nb // bt, kb),
        in_specs=in_specs,
        out_specs=out_specs,
        out_shape=out_shape,
        scratch_shapes=[pltpu.VMEM((rows + 2 * SUBLANE, LANE), F32),
                        pltpu.VMEM((2, rows, LANE), F32), pltpu.VMEM((2, rows, LANE), F32)]
                       + [pltpu.VMEM((2, 3, size // radix, LANE), F32) for size, radix in _scan_levels(n)[0]],
        compiler_params=_params(("arbitrary", "arbitrary"), 2 * 2 * rows * LANE * 4 + 12 * rows * LANE * 4),
    )(*args)


def kernel(x_prompt, x_sample, cache_attn_k, cache_attn_v, state_gla, state_lru, c, c_ctx, w_ada, b_ada, norm1_g, w_in, attn_sink, gla_w_decay, gla_b_decay, gla_norm_g, lru_conv_w, lru_conv_b, lru_w_rgate, lru_b_rgate, lru_w_igate, lru_b_igate, lru_lambda, w_out, norm2_g, w_gu, w_down, final_norm_g):
    nb_c, s_len, d = x_prompt.shape
    nb_l, n_lat, _ = x_sample.shape
    depth = w_in.shape[0]
    lru_w = lru_conv_w.shape[2]
    qw, kw = ATT_HEADS * HEAD_DIM, ATT_KV_HEADS * HEAD_DIM
    gdec0 = qw + 2 * kw + 4 * GLA_HEADS * GLA_DK
    gdec1 = gdec0 + 2 * GLA_LOWRANK
    in_tile = 512
    tail_w = 2 * lru_w + 2 * GLA_LOWRANK
    tail_pad = -(-tail_w // in_tile) * in_tile
    assert gdec0 % in_tile == 0 and w_in.shape[2] == gdec0 + tail_w

    w_in_b = w_in.astype(BF16)
    w_tail_b = jnp.concatenate(
        [w_in_b[:, :, gdec1:], w_in_b[:, :, gdec0:gdec1],
         jnp.zeros((depth, d, tail_pad - tail_w), BF16)], axis=2)
    dec_col = 2 * lru_w // LANE
    w_out_b = w_out.astype(BF16)
    w_gu_b = w_gu.astype(BF16)
    w_down_b = w_down.astype(BF16)

    cvecs = jnp.concatenate([c_ctx[None], c, jnp.zeros((MOD_ROWS - 1 - nb_l, d), F32)], axis=0)
    modr = _ada(cvecs, w_ada, b_ada).reshape(depth, MOD_ROWS, 6, 1, d)
    row_ctx = _ModRows(0, nb_c * s_len)
    row_lat = _ModRows(1, n_lat)

    tables = _rope_tables(n_lat)
    cache_k = cache_attn_k.reshape(nb_l, depth, cache_attn_k.shape[2], kw)
    cache_v = cache_attn_v.reshape(nb_l, depth, cache_attn_v.shape[2], kw)
    lru_init = state_lru.reshape(nb_l, depth, 2, 1, lru_w)

    xc = x_prompt.reshape(nb_c * s_len, d)
    xl = x_sample.reshape(nb_l * n_lat, d)
    new_k, new_v, new_gla, new_lru = [], [], [], []
    for l in range(depth):
        wdec = gla_w_decay[l].reshape(2, GLA_LOWRANK, GLA_HEADS, GLA_DK).transpose(0, 2, 1, 3)
        zpad = jnp.zeros((GLA_HEADS, GLA_LOWRANK, GLA_DK), F32)
        zrest = jnp.zeros((GLA_HEADS, LANE - 2 * GLA_LOWRANK, GLA_DK), F32)
        wf = jnp.concatenate([wdec[0], zpad, zrest], axis=1).astype(BF16)
        wb = jnp.concatenate([zpad, wdec[1], zrest], axis=1).astype(BF16)
        b_dec = gla_b_decay[l].reshape(2, GLA_HEADS, 1, GLA_DK)
        ng = gla_norm_g[l].reshape(GLA_HEADS, 1, LANE)
        lru_args = (lru_conv_w[l], lru_conv_b[l].reshape(1, lru_w),
                    lru_w_rgate[l].astype(BF16), lru_b_rgate[l].reshape(2, 1, lru_w),
                    lru_w_igate[l].astype(BF16), lru_b_igate[l].reshape(2, 1, lru_w),
                    lru_lambda[l].reshape(2, 1, lru_w))
        sink = attn_sink[l]

        def ffn(x, mix, row_of):
            x = _out_proj(*mix, w_out_b, x, modr, l, row_of, 2)
            h2 = _norm_mod(x, norm2_g[l], modr, l, row_of, 4, 3)
            act = _gate_up(h2, w_gu_b, l)
            return _down_proj(act, w_down_b, x, modr, l, row_of, 5)

        h = _norm_mod(xc, norm1_g[l], modr, l, row_ctx, 1, 0)
        proj = _in_proj(h, w_in_b, l, gdec0)
        tail = _in_proj(h, w_tail_b, l, tail_pad)
        attn = _attn_ctx(proj, sink, nb_c, s_len)
        gla, gla_fin = _gla(proj, tail, dec_col, wf, wb, b_dec, ng, nb_c, s_len)
        lru, lru_fin = _lru(tail, *lru_args, nb_c, s_len)
        new_k.append(proj[:, qw:qw + kw].reshape(nb_c, s_len, ATT_KV_HEADS, HEAD_DIM))
        new_v.append(proj[:, qw + kw:qw + 2 * kw].reshape(nb_c, s_len, ATT_KV_HEADS, HEAD_DIM))
        new_gla.append(gla_fin)
        new_lru.append(lru_fin.reshape(nb_c, 2, lru_w))
        xc = ffn(xc, (attn, gla, lru), row_ctx)

        h = _norm_mod(xl, norm1_g[l], modr, l, row_lat, 1, 0)
        proj = _in_proj(h, w_in_b, l, gdec0)
        tail = _in_proj(h, w_tail_b, l, tail_pad)
        attn = _attn_lat(proj, sink, cache_k, cache_v, l, tables, nb_l, n_lat)
        gla = _gla(proj, tail, dec_col, wf, wb, b_dec, ng, nb_l, n_lat, init=state_gla, layer=l)
        lru = _lru(tail, *lru_args, nb_l, n_lat, init=lru_init, layer=l)
        xl = ffn(xl, (attn, gla, lru), row_lat)

    y_prompt = _final_norm(xc, final_norm_g).reshape(nb_c, s_len, d)
    y_sample = _final_norm(xl, final_norm_g).reshape(nb_l, n_lat, d)
    return (y_prompt, y_sample, jnp.stack(new_k, axis=1), jnp.stack(new_v, axis=1),
            jnp.stack(new_gla, axis=1), jnp.stack(new_lru, axis=1))
```

```python
import functools
from typing import NamedTuple

import jax
import jax.numpy as jnp
from jax import lax
from jax.experimental import pallas as pl
from jax.experimental.pallas import tpu as pltpu

F32 = jnp.float32
BF16 = jnp.bfloat16

EPS = 1e-6
HEAD_DIM = 128
ATT_HEADS = 16
ATT_KV_HEADS = 4
ATT_GROUPS = ATT_HEADS // ATT_KV_HEADS
WINDOW = 128
Q_BLOCK = 128
ATT_TILE_ROWS = 128
GRID_W = 64
ROPE_THETA = 10000.0
ROPE_FREQS = HEAD_DIM // 4
GLA_HEADS = 8
GLA_DK = 128
GLA_LOWRANK = 16
GLA_TAU = 16.0
GLA_CHUNK = 64
GLA_BULK_ROWS = 256
GLA_ELEMENTWISE_ROWS = 512
LRU_BLOCK = 128
LRU_CONV = 4
CONV_LEFT = 2
LRU_C = 8.0
MIXER_ROWS_PER_STEP = 1024
SCAN_TOP_ROWS = 8
SCAN_SLAB_ROWS = 64

LANE = 128
SUBLANE = 8
VMEM_BYTES_V7X = 64 * 2**20
MXU_COLS = 256
GATE_UP_SLAB_ROWS = 1024
MOD_ROWS = 16

LOG2_E = 1.4426950408889634
NT_DIMS = (((1,), (1,)), ((), ()))
TN_DIMS = (((0,), (0,)), ((), ()))


def _params(sem, est_bytes):
    limit = int(min(VMEM_BYTES_V7X - 4 * 2**20, max(est_bytes + 8 * 2**20, 32 * 2**20)))
    return pltpu.CompilerParams(dimension_semantics=sem, vmem_limit_bytes=limit)


def _seqs_per_step(nb, n):
    bt = max(1, MIXER_ROWS_PER_STEP // n)
    while nb % bt:
        bt -= 1
    return bt


def _softplus(x):
    return jnp.maximum(x, 0.0) + jnp.log1p(jnp.exp(-jnp.abs(x)))


def _log_sigmoid(x):
    return -_softplus(-x)


def _sigmoid(x):
    return 0.5 * jnp.tanh(0.5 * x) + 0.5


def _ada_kernel(c_ref, w_ref, b_ref, o_ref):
    cv = c_ref[...]
    s = (cv * jax.nn.sigmoid(cv)).astype(BF16)
    o_ref[...] = jnp.dot(s, w_ref[...].astype(BF16), preferred_element_type=F32) + b_ref[...]


def _ada(cvecs, w_ada, b_ada):
    nl, d, n6 = w_ada.shape
    tn = 512
    return pl.pallas_call(
        _ada_kernel,
        grid=(nl, n6 // tn),
        in_specs=[pl.BlockSpec((MOD_ROWS, d), lambda l, j: (0, 0)),
                  pl.BlockSpec((None, d, tn), lambda l, j: (l, 0, j)),
                  pl.BlockSpec((None, 1, tn), lambda l, j: (l, 0, j))],
        out_specs=pl.BlockSpec((None, MOD_ROWS, tn), lambda l, j: (l, 0, j)),
        out_shape=jax.ShapeDtypeStruct((nl, MOD_ROWS, n6), F32),
        compiler_params=_params(("arbitrary", "arbitrary"), 2 * d * tn * 4 + d * tn * 2),
    )(cvecs, w_ada, b_ada.reshape(nl, 1, n6))


def _norm_mod_kernel(x_ref, g_ref, sc_ref, sh_ref, o_ref):
    x = x_ref[...]
    y = x * lax.rsqrt(jnp.mean(x * x, axis=-1, keepdims=True) + EPS) * g_ref[...]
    o_ref[...] = (y * (1.0 + sc_ref[...]) + sh_ref[...]).astype(o_ref.dtype)


def _norm_kernel(x_ref, g_ref, o_ref):
    x = x_ref[...]
    y = x * lax.rsqrt(jnp.mean(x * x, axis=-1, keepdims=True) + EPS) * g_ref[...]
    o_ref[...] = y.astype(o_ref.dtype)


class _ModRows(NamedTuple):
    first: int
    segment: int

    def tile(self, m, want):
        tm = min(want, m, self.segment)
        assert m % tm == 0 and self.segment % tm == 0
        return tm

    def __call__(self, row0):
        return self.first + row0 // self.segment


def _norm_mod(x, g, modr, layer, row_of, sc_idx, sh_idx, tr=256):
    m, d = x.shape
    tr = row_of.tile(m, tr)
    mspec = lambda k: pl.BlockSpec((None, None, None, 1, d), lambda i: (layer, row_of(i * tr), k, 0, 0))
    return pl.pallas_call(
        _norm_mod_kernel,
        grid=(m // tr,),
        in_specs=[pl.BlockSpec((tr, d), lambda i: (i, 0)),
                  pl.BlockSpec((1, d), lambda i: (0, 0)),
                  mspec(sc_idx), mspec(sh_idx)],
        out_specs=pl.BlockSpec((tr, d), lambda i: (i, 0)),
        out_shape=jax.ShapeDtypeStruct((m, d), BF16),
        compiler_params=_params(("arbitrary",), 2 * tr * d * 6 + 2 * tr * d * 4),
    )(x, g.reshape(1, d), modr, modr)


def _final_norm(x, g, tr=256):
    m, d = x.shape
    return pl.pallas_call(
        _norm_kernel,
        grid=(m // tr,),
        in_specs=[pl.BlockSpec((tr, d), lambda i: (i, 0)),
                  pl.BlockSpec((1, d), lambda i: (0, 0))],
        out_specs=pl.BlockSpec((tr, d), lambda i: (i, 0)),
        out_shape=jax.ShapeDtypeStruct((m, d), F32),
        compiler_params=_params(("arbitrary",), 2 * tr * d * 8 + 2 * tr * d * 4),
    )(x, g.reshape(1, d))


def _mm_kernel(x_ref, w_ref, o_ref):
    o_ref[...] = jnp.dot(x_ref[...], w_ref[...], preferred_element_type=F32)


def _in_proj(h, w, layer, n, tm=2048, tn=512):
    m, k = h.shape
    assert n % tn == 0 and n <= w.shape[2]
    tm = min(tm, m)
    return pl.pallas_call(
        _mm_kernel,
        grid=(m // tm, n // tn),
        in_specs=[pl.BlockSpec((tm, k), lambda i, j: (i, 0)),
                  pl.BlockSpec((None, k, tn), lambda i, j: (layer, 0, j))],
        out_specs=pl.BlockSpec((tm, tn), lambda i, j: (i, j)),
        out_shape=jax.ShapeDtypeStruct((m, n), F32),
        compiler_params=_params(("arbitrary", "arbitrary"), 2 * (tm * k * 2 + k * tn * 2 + tm * tn * 4)),
    )(h, w)


def _out_proj_kernel(a_ref, b_ref, c_ref, w_ref, x_ref, g_ref, o_ref):
    ka, kb = a_ref.shape[1], b_ref.shape[1]
    for j0 in range(0, o_ref.shape[1], MXU_COLS):
        cols = slice(j0, j0 + MXU_COLS)
        acc = jnp.dot(a_ref[...], w_ref[0:ka, cols], preferred_element_type=F32)
        acc += jnp.dot(b_ref[...], w_ref[ka:ka + kb, cols], preferred_element_type=F32)
        acc += jnp.dot(c_ref[...], w_ref[ka + kb:, cols], preferred_element_type=F32)
        o_ref[:, cols] = x_ref[:, cols] + g_ref[:, cols] * acc


def _out_proj(attn, gla, lru, w, x, modr, layer, row_of, gate_idx, tm=1024, tn=1024):
    m, d = x.shape
    k = w.shape[1]
    tm = row_of.tile(m, tm)
    xs = lambda a: pl.BlockSpec((tm, a.shape[1]), lambda i, j: (i, 0))
    return pl.pallas_call(
        _out_proj_kernel,
        grid=(m // tm, d // tn),
        in_specs=[xs(attn), xs(gla), xs(lru),
                  pl.BlockSpec((None, k, tn), lambda i, j: (layer, 0, j)),
                  pl.BlockSpec((tm, tn), lambda i, j: (i, j)),
                  pl.BlockSpec((None, None, None, 1, tn),
                               lambda i, j: (layer, row_of(i * tm), gate_idx, 0, j))],
        out_specs=pl.BlockSpec((tm, tn), lambda i, j: (i, j)),
        out_shape=jax.ShapeDtypeStruct((m, d), F32),
        compiler_params=_params(("arbitrary", "arbitrary"),
                                2 * (tm * k * 2 + k * tn * 2 + 2 * tm * tn * 4) + tm * tn * 4),
    )(attn, gla, lru, w, x, modr)


def _gate_up_kernel(h_ref, wg_ref, wu_ref, o_ref):
    tm = h_ref.shape[0]
    slab = min(tm, GATE_UP_SLAB_ROWS)
    for r0 in range(0, tm, slab):
        h = h_ref[r0:r0 + slab, :]
        a = jnp.dot(h, wg_ref[...], preferred_element_type=F32)
        u = jnp.dot(h, wu_ref[...], preferred_element_type=F32)
        o_ref[r0:r0 + slab, :] = (a * _sigmoid(a) * u).astype(o_ref.dtype)


def _gate_up(h, w, layer, tm=2048, tn=256):
    m, k = h.shape
    f = w.shape[2] // 2
    tm = min(tm, m)
    nj = f // tn
    return pl.pallas_call(
        _gate_up_kernel,
        grid=(m // tm, nj),
        in_specs=[pl.BlockSpec((tm, k), lambda i, j: (i, 0)),
                  pl.BlockSpec((None, k, tn), lambda i, j: (layer, 0, j)),
                  pl.BlockSpec((None, k, tn), lambda i, j: (layer, 0, nj + j))],
        out_specs=pl.BlockSpec((tm, tn), lambda i, j: (i, j)),
        out_shape=jax.ShapeDtypeStruct((m, f), BF16),
        compiler_params=_params(("arbitrary", "arbitrary"),
                                2 * (tm * k * 2 + 2 * k * tn * 2 + tm * tn * 2) + 3 * tm * tn * 4),
    )(h, w, w)


def _down_kernel(a_ref, w_ref, x_ref, g_ref, o_ref):
    for j0 in range(0, o_ref.shape[1], MXU_COLS):
        cols = slice(j0, j0 + MXU_COLS)
        acc = jnp.dot(a_ref[...], w_ref[:, cols], preferred_element_type=F32)
        o_ref[:, cols] = x_ref[:, cols] + g_ref[:, cols] * acc


def _down_proj(act, w, x, modr, layer, row_of, gate_idx, tm=512, tn=512):
    m, d = x.shape
    k = w.shape[1]
    tm = row_of.tile(m, tm)
    return pl.pallas_call(
        _down_kernel,
        grid=(m // tm, d // tn),
        in_specs=[pl.BlockSpec((tm, k), lambda i, j: (i, 0)),
                  pl.BlockSpec((None, k, tn), lambda i, j: (layer, 0, j)),
                  pl.BlockSpec((tm, tn), lambda i, j: (i, j)),
                  pl.BlockSpec((None, None, None, 1, tn),
                               lambda i, j: (layer, row_of(i * tm), gate_idx, 0, j))],
        out_specs=pl.BlockSpec((tm, tn), lambda i, j: (i, j)),
        out_shape=jax.ShapeDtypeStruct((m, d), F32),
        compiler_params=_params(("arbitrary", "arbitrary"),
                                2 * (tm * k * 2 + k * tn * 2 + 2 * tm * tn * 4) + tm * tn * 4),
    )(act, w, x, modr)


def _softmax_pv(s_parts, sk, scale, value_of):
    m = functools.reduce(jnp.maximum, s_parts)
    m2 = jnp.maximum(jnp.max(m, axis=-1, keepdims=True) * scale, sk) * LOG2_E
    p_parts = [jnp.exp2(s * (scale * LOG2_E) - m2) for s in s_parts]
    denom = (jnp.sum(functools.reduce(jnp.add, p_parts), axis=-1, keepdims=True)
             + jnp.exp2(sk * LOG2_E - m2))
    o = None
    for j, p in enumerate(p_parts):
        ov = jnp.dot(p.astype(BF16), value_of(j), preferred_element_type=F32)
        o = ov if o is None else o + ov
    return o, 1.0 / denom


def _attn_ctx_kernel(sink_ref, q_ref, k_ref, v_ref, o_ref):
    scale = HEAD_DIM ** -0.5
    head = lambda ref, h: ref[:, h * HEAD_DIM:(h + 1) * HEAD_DIM]
    kt = [head(k_ref, kh).T.astype(BF16) for kh in range(ATT_KV_HEADS)]
    vv = [head(v_ref, kh).astype(BF16) for kh in range(ATT_KV_HEADS)]

    def scores(h):
        s = jnp.dot(head(q_ref, h).astype(BF16), kt[h // ATT_GROUPS], preferred_element_type=F32)
        return [s[:, j:j + LANE] for j in range(0, s.shape[1], LANE)]

    s_next = scores(0)
    for h in range(ATT_HEADS):
        s_parts = s_next
        if h + 1 < ATT_HEADS:
            s_next = scores(h + 1)
        v = vv[h // ATT_GROUPS]
        o, r = _softmax_pv(s_parts, sink_ref[h], scale, lambda j: v[j * LANE:(j + 1) * LANE])
        o_ref[:, h * HEAD_DIM:(h + 1) * HEAD_DIM] = (o * r).astype(o_ref.dtype)


def _attn_ctx(proj, sink, nb, s_len):
    qw = ATT_HEADS * HEAD_DIM
    kw = ATT_KV_HEADS * HEAD_DIM
    return pl.pallas_call(
        _attn_ctx_kernel,
        grid=(nb,),
        in_specs=[pl.BlockSpec(memory_space=pltpu.SMEM),
                  pl.BlockSpec((s_len, qw), lambda b: (b, 0)),
                  pl.BlockSpec((s_len, kw), lambda b: (b, qw // kw)),
                  pl.BlockSpec((s_len, kw), lambda b: (b, qw // kw + 1))],
        out_specs=pl.BlockSpec((s_len, qw), lambda b: (b, 0)),
        out_shape=jax.ShapeDtypeStruct((nb * s_len, qw), BF16),
        compiler_params=_params(("arbitrary",), 2 * s_len * (qw + 2 * kw) * 4 + 2 * s_len * qw * 2
                                + 8 * ATT_GROUPS * s_len * s_len * 4),
    )(sink, proj, proj, proj)


def _attn_lat_kernel(sink_ref, q_ref, k_ref, v_ref, kc_ref, vc_ref, cos_ref, sa_ref, sb_ref,
                     o_ref, kt_s, vb_s, qr_s):
    n = q_ref.shape[0]
    nqb = n // Q_BLOCK
    kh = pl.program_id(1)
    scale = HEAD_DIM ** -0.5
    wblocks = (Q_BLOCK + 2 * WINDOW) // Q_BLOCK

    def rope(x, rows):
        return (x * cos_ref[rows, :] + pltpu.roll(x, HEAD_DIM - ROPE_FREQS, 1) * sa_ref[rows, :]
                + pltpu.roll(x, ROPE_FREQS, 1) * sb_ref[rows, :])

    def prep(j, carry):
        rows = pl.ds(pl.multiple_of(j * Q_BLOCK, Q_BLOCK), Q_BLOCK)
        kt_s[j] = rope(k_ref[rows, :], rows).T.astype(BF16)
        for g in range(ATT_GROUPS):
            lanes = slice(g * HEAD_DIM, (g + 1) * HEAD_DIM)
            qr_s[rows, lanes] = rope(q_ref[rows, lanes], rows).astype(BF16)
        return carry

    lax.fori_loop(0, nqb, prep, 0)
    vb_s[...] = v_ref[...].astype(BF16)
    kct = kc_ref[...].T.astype(BF16)
    vc = vc_ref[...].astype(BF16)
    qi = lax.broadcasted_iota(jnp.int32, (Q_BLOCK, Q_BLOCK), 0)
    ki = lax.broadcasted_iota(jnp.int32, (Q_BLOCK, Q_BLOCK), 1)
    rel = qi - ki

    def block(i, carry):
        qs = pl.multiple_of(i * Q_BLOCK, Q_BLOCK)
        kb = jnp.clip(i - WINDOW // Q_BLOCK, 0, nqb - wblocks)
        biases = []
        for j in range(wblocks):
            off = (i - kb - j) * Q_BLOCK
            biases.append(jnp.where(jnp.abs(rel + off) <= WINDOW, 0.0, -jnp.inf).astype(F32))

        tr = ATT_TILE_ROWS
        tiles = [(g, r0) for g in range(ATT_GROUPS) for r0 in range(0, Q_BLOCK, tr)]

        def scores(g, r0):
            q = qr_s[pl.ds(qs + r0, tr), g * HEAD_DIM:(g + 1) * HEAD_DIM]
            parts = [jnp.dot(q, kt_s[kb + j], preferred_element_type=F32) + biases[j][r0:r0 + tr]
                     for j in range(wblocks)]
            s2 = jnp.dot(q, kct, preferred_element_type=F32)
            return parts + [s2[:, j * LANE:(j + 1) * LANE] for j in range(s2.shape[1] // LANE)]

        s_next = scores(*tiles[0])
        for t, (g, r0) in enumerate(tiles):
            s_parts = s_next
            if t + 1 < len(tiles):
                s_next = scores(*tiles[t + 1])
            o, r = _softmax_pv(s_parts, sink_ref[kh * ATT_GROUPS + g], scale, lambda j: (
                vb_s[pl.ds(pl.multiple_of((kb + j) * Q_BLOCK, Q_BLOCK), Q_BLOCK), :] if j < wblocks
                else vc[(j - wblocks) * LANE:(j - wblocks + 1) * LANE, :]))
            o_ref[pl.ds(qs + r0, tr), g * HEAD_DIM:(g + 1) * HEAD_DIM] = (o * r).astype(o_ref.dtype)
        return carry

    lax.fori_loop(0, nqb, block, 0, unroll=2)


def _attn_lat(proj, sink, cache_k, cache_v, layer, tables, nb, n):
    qw = ATT_HEADS * HEAD_DIM
    gw = ATT_GROUPS * HEAD_DIM
    past = cache_k.shape[2]
    kcol = qw // HEAD_DIM
    vcol = kcol + ATT_KV_HEADS
    cspec = pl.BlockSpec((None, None, past, HEAD_DIM), lambda b, h: (b, layer, 0, h))
    tspec = pl.BlockSpec((n, HEAD_DIM), lambda b, h: (0, 0))
    return pl.pallas_call(
        _attn_lat_kernel,
        grid=(nb, ATT_KV_HEADS),
        in_specs=[pl.BlockSpec(memory_space=pltpu.SMEM),
                  pl.BlockSpec((n, gw), lambda b, h: (b, h)),
                  pl.BlockSpec((n, HEAD_DIM), lambda b, h: (b, kcol + h)),
                  pl.BlockSpec((n, HEAD_DIM), lambda b, h: (b, vcol + h)),
                  cspec, cspec, tspec, tspec, tspec],
        out_specs=pl.BlockSpec((n, gw), lambda b, h: (b, h)),
        out_shape=jax.ShapeDtypeStruct((nb * n, qw), BF16),
        scratch_shapes=[pltpu.VMEM((n // Q_BLOCK, HEAD_DIM, Q_BLOCK), BF16),
                        pltpu.VMEM((n, HEAD_DIM), BF16), pltpu.VMEM((n, gw), BF16)],
        compiler_params=_params(("arbitrary", "arbitrary"),
                                2 * n * (gw + 2 * HEAD_DIM) * 4 + 2 * n * gw * 2 + 6 * n * HEAD_DIM * 4
                                + 4 * n * HEAD_DIM + n * gw * 2),
    )(sink, proj, proj, proj, cache_k, cache_v, *tables)


def _rope_tables(n):
    pos = jnp.arange(n)
    row = (pos // GRID_W).astype(F32)
    col = (pos % GRID_W).astype(F32)
    inv = ROPE_THETA ** (-jnp.arange(ROPE_FREQS, dtype=F32) / ROPE_FREQS)
    ang_r = row[:, None] * inv[None, :]
    ang_c = col[:, None] * inv[None, :]
    cr, sr, cc, sc = jnp.cos(ang_r), jnp.sin(ang_r), jnp.cos(ang_c), jnp.sin(ang_c)
    z = jnp.zeros_like(cr)
    cos = jnp.concatenate([cr, cr, cc, cc], axis=1)
    sa = jnp.concatenate([-sr, z, -sc, z], axis=1)
    sb = jnp.concatenate([z, sr, z, sc], axis=1)
    return cos, sa, sb


def _gla_kernel(*refs, has_init, seq):
    if has_init:
        (q_ref, k_ref, v_ref, g_ref, dec_ref, wf_ref, wb_ref, bd_ref, ng_ref, s0_ref,
         o_ref, qin_s, tot_s, o_s, kv_s, sin_s) = refs
        sfin_ref = None
    else:
        (q_ref, k_ref, v_ref, g_ref, dec_ref, wf_ref, wb_ref, bd_ref, ng_ref,
         o_ref, sfin_ref, qin_s, tot_s, o_s, kv_s, sin_s) = refs
    n = q_ref.shape[0]
    c = GLA_CHUNK
    nc = n // c
    blk = min(seq, GLA_BULK_ROWS)
    qscale = GLA_DK ** -0.5
    shift = c.bit_length() - 1
    ri = lax.broadcasted_iota(jnp.int32, (blk, blk), 0)
    ci = lax.broadcasted_iota(jnp.int32, (blk, blk), 1)
    same = lax.shift_right_logical(ri, shift) == lax.shift_right_logical(ci, shift)
    keep = (same & (ri >= ci), same & (ri <= ci))
    sums = tuple(jnp.concatenate([keep[d].astype(BF16), same.astype(BF16)], axis=0) for d in range(2))
    wdec = jnp.concatenate([wf_ref[...], wb_ref[...]], axis=1)
    bdec = jnp.concatenate([bd_ref[0], bd_ref[1]], axis=1)

    def split3(x):
        hi = x.astype(BF16)
        r1 = x - hi.astype(F32)
        mid = r1.astype(BF16)
        return jnp.concatenate([hi, mid, (r1 - mid.astype(F32)).astype(BF16)], axis=1)

    def fold3(y):
        y = (y[:, :LANE] + y[:, LANE:2 * LANE]) + y[:, 2 * LANE:]
        return y[:blk], y[blk:]

    def chain(rows, chunk0, d, z, qc, kc, vc):
        st = {}

        def log_decay():
            st["p"] = split3(_log_sigmoid(z[:, d * GLA_DK:(d + 1) * GLA_DK]) * (1.0 / GLA_TAU))

        def chunk_sums():
            st["y"] = jnp.dot(sums[d], st["p"], preferred_element_type=F32)

        def decayed():
            cum, tot = fold3(st["y"])
            tot_s[d, rows, :] = tot
            st["q"] = (qc * jnp.exp(cum)).astype(BF16)
            st["k"] = (kc * jnp.exp(-cum)).astype(BF16)
            qin_s[d, rows, :] = st["q"]
            st["kd"] = (kc * jnp.exp(tot - cum)).astype(BF16)

        def scores():
            st["a"] = lax.dot_general(st["q"], st["k"], NT_DIMS, preferred_element_type=F32)

        def mask():
            st["att"] = jnp.where(keep[d], st["a"], 0.0).astype(BF16)

        def within():
            o_s[d, rows, :] = jnp.dot(st["att"], vc, preferred_element_type=F32)
            for u in range(blk // c):
                cr = slice(u * c, (u + 1) * c)
                kv_s[d, chunk0 + u] = lax.dot_general(vc[cr], st["kd"][cr], TN_DIMS,
                                                      preferred_element_type=F32)

        return [log_decay, chunk_sums, decayed, scores, mask, within]

    per_step = 2 if (n // blk) % 2 == 0 else 1

    def bulk(i, carry):
        chains = []
        for b in range(per_step):
            block = i * per_step + b
            rows = pl.ds(pl.multiple_of(block * blk, blk), blk)
            z = jnp.dot(dec_ref[rows, :].astype(BF16), wdec, preferred_element_type=F32) + bdec
            vc = v_ref[rows, :].astype(BF16)
            chains += [chain(rows, block * (blk // c), d, z, q_ref[rows, :] * qscale, k_ref[rows, :], vc)
                       for d in range(2)]
        nstage = len(chains[0])
        for t in range(nstage + len(chains) - 1):
            for lag, stages in enumerate(chains):
                if 0 <= t - lag < nstage:
                    stages[t - lag]()
        return carry

    lax.fori_loop(0, n // (blk * per_step), bulk, 0)

    ncs = seq // c
    for s in range(n // seq):
        def scan(j, carry, base=s * ncs):
            sf, sb = carry
            jf = base + j
            jb = base + ncs - 1 - j
            sin_s[0, jf] = sf.astype(BF16)
            sin_s[1, jb] = sb.astype(BF16)
            sf = jnp.exp(tot_s[0, pl.ds(jf * c, 1), :]) * sf + kv_s[0, jf]
            sb = jnp.exp(tot_s[1, pl.ds(jb * c, 1), :]) * sb + kv_s[1, jb]
            return sf, sb

        if has_init:
            init = (s0_ref[s, 0].T, s0_ref[s, 1].T)
        else:
            init = (jnp.zeros((LANE, GLA_DK), F32), jnp.zeros((LANE, GLA_DK), F32))
        sf, sb = lax.fori_loop(0, ncs, scan, init)
        if sfin_ref is not None:
            sfin_ref[s, 0] = sf.T
            sfin_ref[s, 1] = sb.T

    group = 4 if nc % 4 == 0 else 1

    def carried(g):
        terms = []
        for j in range(g * group, (g + 1) * group):
            rows = slice(j * c, (j + 1) * c)
            terms.append((rows, [(o_s[d, rows, :], lax.dot_general(qin_s[d, rows, :], sin_s[d, j], NT_DIMS,
                                                                   preferred_element_type=F32))
                                 for d in range(2)]))
        return terms

    nxt = carried(0)
    for g in range(nc // group):
        cur = nxt
        if g + 1 < nc // group:
            nxt = carried(g + 1)
        for rows, ((of, cf), (ob, cb)) in cur:
            o_s[0, rows, :] = (of + cf) + (ob + cb)

    ew = min(n, GLA_ELEMENTWISE_ROWS)

    def finish(i, carry):
        rows = pl.ds(pl.multiple_of(i * ew, ew), ew)
        o = o_s[0, rows, :]
        o = o * lax.rsqrt(jnp.mean(o * o, axis=-1, keepdims=True) + EPS) * ng_ref[...]
        g = g_ref[rows, :]
        o_ref[rows, :] = (o * (g * _sigmoid(g))).astype(o_ref.dtype)
        return carry

    lax.fori_loop(0, n // ew, finish, 0)


def _gla(proj, proj_tail, dec_col, wf, wb, b_dec, norm_g, nb, n, init=None, layer=0):
    hh = GLA_HEADS
    q0 = (ATT_HEADS + 2 * ATT_KV_HEADS) * HEAD_DIM // LANE
    bt = _seqs_per_step(nb, n)
    rows = bt * n
    nc = rows // GLA_CHUNK
    col = lambda base: pl.BlockSpec((rows, LANE), lambda b, h: (b, base + h))
    in_specs = [col(q0), col(q0 + hh), col(q0 + 2 * hh), col(q0 + 3 * hh),
                pl.BlockSpec((rows, LANE), lambda b, h: (b, dec_col)),
                pl.BlockSpec((None, LANE, LANE), lambda b, h: (h, 0, 0)),
                pl.BlockSpec((None, LANE, LANE), lambda b, h: (h, 0, 0)),
                pl.BlockSpec((2, None, 1, LANE), lambda b, h: (0, h, 0, 0)),
                pl.BlockSpec((None, 1, LANE), lambda b, h: (h, 0, 0))]
    args = [proj, proj, proj, proj, proj_tail, wf, wb, b_dec, norm_g]
    o_shape = jax.ShapeDtypeStruct((nb * n, hh * LANE), BF16)
    o_spec = pl.BlockSpec((rows, LANE), lambda b, h: (b, h))
    if init is not None:
        in_specs.append(pl.BlockSpec((bt, None, 2, None, GLA_DK, LANE),
                                     lambda b, h: (b, layer, 0, h, 0, 0)))
        args.append(init)
        out_shape, out_specs = o_shape, o_spec
    else:
        out_shape = (o_shape, jax.ShapeDtypeStruct((nb, 2, hh, GLA_DK, LANE), F32))
        out_specs = (o_spec, pl.BlockSpec((bt, 2, None, GLA_DK, LANE), lambda b, h: (b, 0, h, 0, 0)))
    return pl.pallas_call(
        functools.partial(_gla_kernel, has_init=init is not None, seq=n),
        grid=(nb // bt, hh),
        in_specs=in_specs,
        out_specs=out_specs,
        out_shape=out_shape,
        scratch_shapes=[pltpu.VMEM((2, rows, GLA_DK), BF16),
                        pltpu.VMEM((2, rows, GLA_DK), F32),
                        pltpu.VMEM((2, rows, LANE), F32),
                        pltpu.VMEM((2, nc, LANE, GLA_DK), F32),
                        pltpu.VMEM((2, nc, LANE, GLA_DK), BF16)],
        compiler_params=_params(("arbitrary", "arbitrary"),
                                2 * 5 * rows * LANE * 4 + 2 * rows * LANE * 2 + rows * LANE * (2 + 8 + 16)
                                + nc * LANE * GLA_DK * 12 + 4 * 2**20),
    )(*args)


def _scan_levels(n):
    levels = []
    size = n
    while size > SCAN_TOP_ROWS:
        radix = 4 if not levels else 8
        levels.append((size, radix))
        size //= radix
    return levels, size


def _linear_scan(a, u, lv, n, h0, rev):
    levels, top = _scan_levels(n)

    def sweep(a_l, u_l, size, radix, body):
        m = size // radix
        sb = min(m, SCAN_SLAB_ROWS)
        first = radix - 1 if rev else 0
        order = list(range(radix - 2, -1, -1)) if rev else list(range(1, radix))

        def blk(j, carry):
            sl = lambda r: pl.ds(j * (sb * radix) + r, sb, stride=radix)
            body(sl, pl.ds(pl.multiple_of(j * sb, sb), sb), first, order)
            return carry

        if m // sb == 1:
            blk(0, 0)
        else:
            lax.fori_loop(0, m // sb, blk, 0)

    src_a, src_u = a, u
    for l, (size, radix) in enumerate(levels):
        nxt = lv[l]

        def up(sl, dense, first, order, src_a=src_a, src_u=src_u, nxt=nxt):
            pa = src_a[sl(first), :]
            pu = src_u[sl(first), :]
            for r in order:
                ar = src_a[sl(r), :]
                pu = ar * pu + src_u[sl(r), :]
                pa = ar * pa
                src_a[sl(r), :] = pa
                src_u[sl(r), :] = pu
            nxt[0, dense, :] = pa
            nxt[1, dense, :] = pu

        sweep(src_a, src_u, size, radix, up)
        src_a, src_u = nxt.at[0], nxt.at[1]

    carry = h0
    top_c = lv[len(levels) - 1].at[2]
    for g in (range(top - 1, -1, -1) if rev else range(top)):
        top_c[g:g + 1, :] = carry
        carry = src_a[g:g + 1, :] * carry + src_u[g:g + 1, :]

    for l in range(len(levels) - 1, -1, -1):
        size, radix = levels[l]
        a_l, u_l = (a, u) if l == 0 else (lv[l - 1].at[0], lv[l - 1].at[1])
        c_l = lv[l].at[2]
        c_below = None if l == 0 else lv[l - 1].at[2]

        def down(sl, dense, first, order, a_l=a_l, u_l=u_l, c_l=c_l, c_below=c_below):
            cin = c_l[dense, :]
            prev = cin
            for r in [first] + order:
                h = u_l[sl(r), :] + a_l[sl(r), :] * cin
                if c_below is None:
                    u_l[sl(r), :] = h
                else:
                    c_below[sl(r), :] = prev
                prev = h

        sweep(a_l, u_l, size, radix, down)
    return carry


def _lru_kernel(*refs, has_init, seq):
    if has_init:
        (x_ref, y_ref, cw_ref, cb_ref, wr_ref, br_ref, wi_ref, bi_ref, lam_ref, h0_ref,
         o_ref, xp_s, a_s, u_s, *lv_s) = refs
        hfin_ref = None
    else:
        (x_ref, y_ref, cw_ref, cb_ref, wr_ref, br_ref, wi_ref, bi_ref, lam_ref,
         o_ref, hfin_ref, xp_s, a_s, u_s, *lv_s) = refs
    n = x_ref.shape[0]
    nseq = n // seq
    pad = SUBLANE
    xp_s[0:pad, :] = jnp.zeros((pad, LANE), F32)
    xp_s[pad + n:pad + n + pad, :] = jnp.zeros((pad, LANE), F32)
    xp_s[pad:pad + n, :] = x_ref[...]
    t = lax.broadcasted_iota(jnp.int32, (n, 1), 0)
    xc = cb_ref[...]
    for j in range(LRU_CONV):
        off = j - CONV_LEFT
        tap = xp_s[pl.ds(pad + off, n), :]
        edge = [s * seq + e for s in range(1, nseq) for e in range(0, -off)] if off < 0 else \
               [s * seq - 1 - e for s in range(1, nseq) for e in range(0, off)]
        if edge:
            tap = jnp.where(functools.reduce(jnp.logical_or, [t == e for e in edge]), 0.0, tap)
        xc = xc + tap * cw_ref[j:j + 1, :]
    xcb = xc.astype(BF16)
    for d in range(2):
        r = _sigmoid(jnp.dot(xcb, wr_ref[d], preferred_element_type=F32) + br_ref[d])
        i = _sigmoid(jnp.dot(xcb, wi_ref[d], preferred_element_type=F32) + bi_ref[d])
        log_a = -LRU_C * r * _softplus(-lam_ref[d])
        a = jnp.exp(log_a)
        a_s[d] = a
        v = -jnp.tanh(log_a) * (a * a + 1.0)
        u_s[d] = jnp.where(v > 0.0, v * lax.rsqrt(v), 0.0) * (i * xc)

    for s in range(nseq):
        rows = pl.ds(s * seq, seq)
        for d in range(2):
            h0 = h0_ref[s, d] if has_init else jnp.zeros((1, LANE), F32)
            h_last = _linear_scan(a_s.at[d, rows], u_s.at[d, rows], [lv.at[d] for lv in lv_s], seq, h0,
                                  rev=d == 1)
            if hfin_ref is not None:
                hfin_ref[s, d] = h_last
    o_ref[...] = ((u_s[0] + u_s[1]) * jax.nn.gelu(y_ref[...])).astype(o_ref.dtype)


def _lru(proj, conv_w, conv_b, w_r, b_r, w_i, b_i, lam, nb, n, init=None, layer=0):
    kb = conv_w.shape[1] // LRU_BLOCK
    bt = _seqs_per_step(nb, n)
    rows = bt * n
    vec = lambda r: pl.BlockSpec((r, LANE), lambda b, k: (0, k))
    vec2 = pl.BlockSpec((2, 1, LANE), lambda b, k: (0, 0, k))
    wspec = pl.BlockSpec((2, None, LRU_BLOCK, LRU_BLOCK), lambda b, k: (0, k, 0, 0))
    in_specs = [pl.BlockSpec((rows, LANE), lambda b, k: (b, k)),
                pl.BlockSpec((rows, LANE), lambda b, k: (b, kb + k)),
                vec(LRU_CONV), vec(1), wspec, vec2, wspec, vec2, vec2]
    args = [proj, proj, conv_w, conv_b, w_r, b_r, w_i, b_i, lam]
    o_shape = jax.ShapeDtypeStruct((nb * n, kb * LANE), BF16)
    o_spec = pl.BlockSpec((rows, LANE), lambda b, k: (b, k))
    if init is not None:
        in_specs.append(pl.BlockSpec((bt, None, 2, 1, LANE), lambda b, k: (b, layer, 0, 0, k)))
        args.append(init)
        out_shape, out_specs = o_shape, o_spec
    else:
        out_shape = (o_shape, jax.ShapeDtypeStruct((nb, 2, 1, kb * LANE), F32))
        out_specs = (o_spec, pl.BlockSpec((bt, 2, 1, LANE), lambda b, k: (b, 0, 0, k)))
    return pl.pallas_call(
        functools.partial(_lru_kernel, has_init=init is not None, seq=n),
        grid=(nb // bt, kb),
        in_specs=in_specs,
        out_specs=out_specs,
        out_shape=out_shape,
        scratch_shapes=[pltpu.VMEM((rows + 2 * SUBLANE, LANE), F32),
                        pltpu.VMEM((2, rows, LANE), F32), pltpu.VMEM((2, rows, LANE), F32)]
                       + [pltpu.VMEM((2, 3, size // radix, LANE), F32) for size, radix in _scan_levels(n)[0]],
        compiler_params=_params(("arbitrary", "arbitrary"), 2 * 2 * rows * LANE * 4 + 12 * rows * LANE * 4),
    )(*args)


def kernel(x_prompt, x_sample, cache_attn_k, cache_attn_v, state_gla, state_lru, c, c_ctx, w_ada, b_ada, norm1_g, w_in, attn_sink, gla_w_decay, gla_b_decay, gla_norm_g, lru_conv_w, lru_conv_b, lru_w_rgate, lru_b_rgate, lru_w_igate, lru_b_igate, lru_lambda, w_out, norm2_g, w_gu, w_down, final_norm_g):
    nb_c, s_len, d = x_prompt.shape
    nb_l, n_lat, _ = x_sample.shape
    depth = w_in.shape[0]
    lru_w = lru_conv_w.shape[2]
    qw, kw = ATT_HEADS * HEAD_DIM, ATT_KV_HEADS * HEAD_DIM
    gdec0 = qw + 2 * kw + 4 * GLA_HEADS * GLA_DK
    gdec1 = gdec0 + 2 * GLA_LOWRANK
    in_tile = 512
    tail_w = 2 * lru_w + 2 * GLA_LOWRANK
    tail_pad = -(-tail_w // in_tile) * in_tile
    assert gdec0 % in_tile == 0 and w_in.shape[2] == gdec0 + tail_w

    w_in_b = w_in.astype(BF16)
    w_tail_b = jnp.concatenate(
        [w_in_b[:, :, gdec1:], w_in_b[:, :, gdec0:gdec1],
         jnp.zeros((depth, d, tail_pad - tail_w), BF16)], axis=2)
    dec_col = 2 * lru_w // LANE
    w_out_b = w_out.astype(BF16)
    w_gu_b = w_gu.astype(BF16)
    w_down_b = w_down.astype(BF16)

    cvecs = jnp.concatenate([c_ctx[None], c, jnp.zeros((MOD_ROWS - 1 - nb_l, d), F32)], axis=0)
    modr = _ada(cvecs, w_ada, b_ada).reshape(depth, MOD_ROWS, 6, 1, d)
    row_ctx = _ModRows(0, nb_c * s_len)
    row_lat = _ModRows(1, n_lat)

    tables = _rope_tables(n_lat)
    cache_k = cache_attn_k.reshape(nb_l, depth, cache_attn_k.shape[2], kw)
    cache_v = cache_attn_v.reshape(nb_l, depth, cache_attn_v.shape[2], kw)
    lru_init = state_lru.reshape(nb_l, depth, 2, 1, lru_w)

    xc = x_prompt.reshape(nb_c * s_len, d)
    xl = x_sample.reshape(nb_l * n_lat, d)
    new_k, new_v, new_gla, new_lru = [], [], [], []
    for l in range(depth):
        wdec = gla_w_decay[l].reshape(2, GLA_LOWRANK, GLA_HEADS, GLA_DK).transpose(0, 2, 1, 3)
        zpad = jnp.zeros((GLA_HEADS, GLA_LOWRANK, GLA_DK), F32)
        zrest = jnp.zeros((GLA_HEADS, LANE - 2 * GLA_LOWRANK, GLA_DK), F32)
        wf = jnp.concatenate([wdec[0], zpad, zrest], axis=1).astype(BF16)
        wb = jnp.concatenate([zpad, wdec[1], zrest], axis=1).astype(BF16)
        b_dec = gla_b_decay[l].reshape(2, GLA_HEADS, 1, GLA_DK)
        ng = gla_norm_g[l].reshape(GLA_HEADS, 1, LANE)
        lru_args = (lru_conv_w[l], lru_conv_b[l].reshape(1, lru_w),
                    lru_w_rgate[l].astype(BF16), lru_b_rgate[l].reshape(2, 1, lru_w),
                    lru_w_igate[l].astype(BF16), lru_b_igate[l].reshape(2, 1, lru_w),
                    lru_lambda[l].reshape(2, 1, lru_w))
        sink = attn_sink[l]

        def ffn(x, mix, row_of):
            x = _out_proj(*mix, w_out_b, x, modr, l, row_of, 2)
            h2 = _norm_mod(x, norm2_g[l], modr, l, row_of, 4, 3)
            act = _gate_up(h2, w_gu_b, l)
            return _down_proj(act, w_down_b, x, modr, l, row_of, 5)

        h = _norm_mod(xc, norm1_g[l], modr, l, row_ctx, 1, 0)
        proj = _in_proj(h, w_in_b, l, gdec0)
        tail = _in_proj(h, w_tail_b, l, tail_pad)
        attn = _attn_ctx(proj, sink, nb_c, s_len)
        gla, gla_fin = _gla(proj, tail, dec_col, wf, wb, b_dec, ng, nb_c, s_len)
        lru, lru_fin = _lru(tail, *lru_args, nb_c, s_len)
        new_k.append(proj[:, qw:qw + kw].reshape(nb_c, s_len, ATT_KV_HEADS, HEAD_DIM))
        new_v.append(proj[:, qw + kw:qw + 2 * kw].reshape(nb_c, s_len, ATT_KV_HEADS, HEAD_DIM))
        new_gla.append(gla_fin)
        new_lru.append(lru_fin.reshape(nb_c, 2, lru_w))
        xc = ffn(xc, (attn, gla, lru), row_ctx)

        h = _norm_mod(xl, norm1_g[l], modr, l, row_lat, 1, 0)
        proj = _in_proj(h, w_in_b, l, gdec0)
        tail = _in_proj(h, w_tail_b, l, tail_pad)
        attn = _attn_lat(proj, sink, cache_k, cache_v, l, tables, nb_l, n_lat)
        gla = _gla(proj, tail, dec_col, wf, wb, b_dec, ng, nb_l, n_lat, init=state_gla, layer=l)
        lru = _lru(tail, *lru_args, nb_l, n_lat, init=lru_init, layer=l)
        xl = ffn(xl, (attn, gla, lru), row_lat)

    y_prompt = _final_norm(xc, final_norm_g).reshape(nb_c, s_len, d)
    y_sample = _final_norm(xl, final_norm_g).reshape(nb_l, n_lat, d)
    return (y_prompt, y_sample, jnp.stack(new_k, axis=1), jnp.stack(new_v, axis=1),
            jnp.stack(new_gla, axis=1), jnp.stack(new_lru, axis=1))
```

```python
import functools
from typing import NamedTuple

import jax
import jax.numpy as jnp
from jax import lax
from jax.experimental import pallas as pl
from jax.experimental.pallas import tpu as pltpu

F32 = jnp.float32
BF16 = jnp.bfloat16

EPS = 1e-6
HEAD_DIM = 128
ATT_HEADS = 16
ATT_KV_HEADS = 4
ATT_GROUPS = ATT_HEADS // ATT_KV_HEADS
WINDOW = 128
Q_BLOCK = 128
ATT_TILE_ROWS = 128
GRID_W = 64
ROPE_THETA = 10000.0
ROPE_FREQS = HEAD_DIM // 4
GLA_HEADS = 8
GLA_DK = 128
GLA_LOWRANK = 16
GLA_TAU = 16.0
GLA_CHUNK = 64
GLA_BULK_ROWS = 256
GLA_ELEMENTWISE_ROWS = 512
LRU_BLOCK = 128
LRU_CONV = 4
CONV_LEFT = 2
LRU_C = 8.0
MIXER_ROWS_PER_STEP = 1024
SCAN_TOP_ROWS = 8
SCAN_SLAB_ROWS = 64

LANE = 128
SUBLANE = 8
VMEM_BYTES_V7X = 64 * 2**20
MXU_COLS = 256
GATE_UP_SLAB_ROWS = 1024
MOD_ROWS = 16

LOG2_E = 1.4426950408889634
NT_DIMS = (((1,), (1,)), ((), ()))
TN_DIMS = (((0,), (0,)), ((), ()))


def _params(sem, est_bytes):
    limit = int(min(VMEM_BYTES_V7X - 4 * 2**20, max(est_bytes + 8 * 2**20, 32 * 2**20)))
    return pltpu.CompilerParams(dimension_semantics=sem, vmem_limit_bytes=limit)


def _seqs_per_step(nb, n):
    bt = max(1, MIXER_ROWS_PER_STEP // n)
    while nb % bt:
        bt -= 1
    return bt


def _softplus(x):
    return jnp.maximum(x, 0.0) + jnp.log1p(jnp.exp(-jnp.abs(x)))


def _log_sigmoid(x):
    return -_softplus(-x)


def _sigmoid(x):
    return 0.5 * jnp.tanh(0.5 * x) + 0.5


def _ada_kernel(c_ref, w_ref, b_ref, o_ref):
    cv = c_ref[...]
    s = (cv * jax.nn.sigmoid(cv)).astype(BF16)
    o_ref[...] = jnp.dot(s, w_ref[...].astype(BF16), preferred_element_type=F32) + b_ref[...]


def _ada(cvecs, w_ada, b_ada):
    nl, d, n6 = w_ada.shape
    tn = 512
    return pl.pallas_call(
        _ada_kernel,
        grid=(nl, n6 // tn),
        in_specs=[pl.BlockSpec((MOD_ROWS, d), lambda l, j: (0, 0)),
                  pl.BlockSpec((None, d, tn), lambda l, j: (l, 0, j)),
                  pl.BlockSpec((None, 1, tn), lambda l, j: (l, 0, j))],
        out_specs=pl.BlockSpec((None, MOD_ROWS, tn), lambda l, j: (l, 0, j)),
        out_shape=jax.ShapeDtypeStruct((nl, MOD_ROWS, n6), F32),
        compiler_params=_params(("arbitrary", "arbitrary"), 2 * d * tn * 4 + d * tn * 2),
    )(cvecs, w_ada, b_ada.reshape(nl, 1, n6))


def _norm_mod_kernel(x_ref, g_ref, sc_ref, sh_ref, o_ref):
    x = x_ref[...]
    y = x * lax.rsqrt(jnp.mean(x * x, axis=-1, keepdims=True) + EPS) * g_ref[...]
    o_ref[...] = (y * (1.0 + sc_ref[...]) + sh_ref[...]).astype(o_ref.dtype)


def _norm_kernel(x_ref, g_ref, o_ref):
    x = x_ref[...]
    y = x * lax.rsqrt(jnp.mean(x * x, axis=-1, keepdims=True) + EPS) * g_ref[...]
    o_ref[...] = y.astype(o_ref.dtype)


class _ModRows(NamedTuple):
    first: int
    segment: int

    def tile(self, m, want):
        tm = min(want, m, self.segment)
        assert m % tm == 0 and self.segment % tm == 0
        return tm

    def __call__(self, row0):
        return self.first + row0 // self.segment


def _norm_mod(x, g, modr, layer, row_of, sc_idx, sh_idx, tr=512):
    m, d = x.shape
    tr = row_of.tile(m, tr)
    mspec = lambda k: pl.BlockSpec((None, None, None, 1, d), lambda i: (layer, row_of(i * tr), k, 0, 0))
    return pl.pallas_call(
        _norm_mod_kernel,
        grid=(m // tr,),
        in_specs=[pl.BlockSpec((tr, d), lambda i: (i, 0)),
                  pl.BlockSpec((1, d), lambda i: (0, 0)),
                  mspec(sc_idx), mspec(sh_idx)],
        out_specs=pl.BlockSpec((tr, d), lambda i: (i, 0)),
        out_shape=jax.ShapeDtypeStruct((m, d), BF16),
        compiler_params=_params(("arbitrary",), 2 * tr * d * 6 + 2 * tr * d * 4),
    )(x, g.reshape(1, d), modr, modr)


def _final_norm(x, g, tr=512):
    m, d = x.shape
    tr = min(tr, m)
    return pl.pallas_call(
        _norm_kernel,
        grid=(m // tr,),
        in_specs=[pl.BlockSpec((tr, d), lambda i: (i, 0)),
                  pl.BlockSpec((1, d), lambda i: (0, 0))],
        out_specs=pl.BlockSpec((tr, d), lambda i: (i, 0)),
        out_shape=jax.ShapeDtypeStruct((m, d), F32),
        compiler_params=_params(("arbitrary",), 2 * tr * d * 8 + 2 * tr * d * 4),
    )(x, g.reshape(1, d))


def _mm_kernel(x_ref, w_ref, o_ref):
    for j0 in range(0, o_ref.shape[1], MXU_COLS):
        cols = slice(j0, min(j0 + MXU_COLS, o_ref.shape[1]))
        o_ref[:, cols] = jnp.dot(x_ref[...], w_ref[:, cols], preferred_element_type=F32)


def _in_proj(h, w, layer, n, tm=2048, tn=512):
    m, k = h.shape
    assert n <= w.shape[2] and n % LANE == 0
    if n % tn:
        assert n == w.shape[2]
        tm, tn = 512, n
    tm = min(tm, m)
    return pl.pallas_call(
        _mm_kernel,
        grid=(m // tm, n // tn),
        in_specs=[pl.BlockSpec((tm, k), lambda i, j: (i, 0)),
                  pl.BlockSpec((None, k, tn), lambda i, j: (layer, 0, j))],
        out_specs=pl.BlockSpec((tm, tn), lambda i, j: (i, j)),
        out_shape=jax.ShapeDtypeStruct((m, n), F32),
        compiler_params=_params(("arbitrary", "arbitrary"), 2 * (tm * k * 2 + k * tn * 2 + tm * tn * 4)),
    )(h, w)


def _out_proj_kernel(a_ref, b_ref, c_ref, w_ref, x_ref, g_ref, o_ref):
    ka, kb = a_ref.shape[1], b_ref.shape[1]
    for j0 in range(0, o_ref.shape[1], MXU_COLS):
        cols = slice(j0, j0 + MXU_COLS)
        acc = jnp.dot(a_ref[...], w_ref[0:ka, cols], preferred_element_type=F32)
        acc += jnp.dot(b_ref[...], w_ref[ka:ka + kb, cols], preferred_element_type=F32)
        acc += jnp.dot(c_ref[...], w_ref[ka + kb:, cols], preferred_element_type=F32)
        o_ref[:, cols] = x_ref[:, cols] + g_ref[:, cols] * acc


def _out_proj(attn, gla, lru, w, x, modr, layer, row_of, gate_idx, tm=1024, tn=1024):
    m, d = x.shape
    k = w.shape[1]
    tm = row_of.tile(m, tm)
    xs = lambda a: pl.BlockSpec((tm, a.shape[1]), lambda i, j: (i, 0))
    return pl.pallas_call(
        _out_proj_kernel,
        grid=(m // tm, d // tn),
        in_specs=[xs(attn), xs(gla), xs(lru),
                  pl.BlockSpec((None, k, tn), lambda i, j: (layer, 0, j)),
                  pl.BlockSpec((tm, tn), lambda i, j: (i, j)),
                  pl.BlockSpec((None, None, None, 1, tn),
                               lambda i, j: (layer, row_of(i * tm), gate_idx, 0, j))],
        out_specs=pl.BlockSpec((tm, tn), lambda i, j: (i, j)),
        out_shape=jax.ShapeDtypeStruct((m, d), F32),
        compiler_params=_params(("arbitrary", "arbitrary"),
                                2 * (tm * k * 2 + k * tn * 2 + 2 * tm * tn * 4) + tm * tn * 4),
    )(attn, gla, lru, w, x, modr)


def _gate_up_kernel(h_ref, wg_ref, wu_ref, o_ref):
    tm = h_ref.shape[0]
    slab = min(tm, GATE_UP_SLAB_ROWS)
    for r0 in range(0, tm, slab):
        h = h_ref[r0:r0 + slab, :]
        a = jnp.dot(h, wg_ref[...], preferred_element_type=F32)
        u = jnp.dot(h, wu_ref[...], preferred_element_type=F32)
        o_ref[r0:r0 + slab, :] = (a * _sigmoid(a) * u).astype(o_ref.dtype)


def _gate_up(h, w, layer, tm=2048, tn=256):
    m, k = h.shape
    f = w.shape[2] // 2
    tm = min(tm, m)
    nj = f // tn
    return pl.pallas_call(
        _gate_up_kernel,
        grid=(m // tm, nj),
        in_specs=[pl.BlockSpec((tm, k), lambda i, j: (i, 0)),
                  pl.BlockSpec((None, k, tn), lambda i, j: (layer, 0, j)),
                  pl.BlockSpec((None, k, tn), lambda i, j: (layer, 0, nj + j))],
        out_specs=pl.BlockSpec((tm, tn), lambda i, j: (i, j)),
        out_shape=jax.ShapeDtypeStruct((m, f), BF16),
        compiler_params=_params(("arbitrary", "arbitrary"),
                                2 * (tm * k * 2 + 2 * k * tn * 2 + tm * tn * 2) + 3 * tm * tn * 4),
    )(h, w, w)


def _down_kernel(a_ref, w_ref, x_ref, g_ref, o_ref):
    for j0 in range(0, o_ref.shape[1], MXU_COLS):
        cols = slice(j0, j0 + MXU_COLS)
        acc = jnp.dot(a_ref[...], w_ref[:, cols], preferred_element_type=F32)
        o_ref[:, cols] = x_ref[:, cols] + g_ref[:, cols] * acc


def _down_proj(act, w, x, modr, layer, row_of, gate_idx, tm=512, tn=512):
    m, d = x.shape
    k = w.shape[1]
    tm = row_of.tile(m, tm)
    return pl.pallas_call(
        _down_kernel,
        grid=(m // tm, d // tn),
        in_specs=[pl.BlockSpec((tm, k), lambda i, j: (i, 0)),
                  pl.BlockSpec((None, k, tn), lambda i, j: (layer, 0, j)),
                  pl.BlockSpec((tm, tn), lambda i, j: (i, j)),
                  pl.BlockSpec((None, None, None, 1, tn),
                               lambda i, j: (layer, row_of(i * tm), gate_idx, 0, j))],
        out_specs=pl.BlockSpec((tm, tn), lambda i, j: (i, j)),
        out_shape=jax.ShapeDtypeStruct((m, d), F32),
        compiler_params=_params(("arbitrary", "arbitrary"),
                                2 * (tm * k * 2 + k * tn * 2 + 2 * tm * tn * 4) + tm * tn * 4),
    )(act, w, x, modr)


def _softmax_pv(s_parts, sk, scale, value_of):
    m = functools.reduce(jnp.maximum, s_parts)
    m2 = jnp.maximum(jnp.max(m, axis=-1, keepdims=True) * scale, sk) * LOG2_E
    p_parts = [jnp.exp2(s * (scale * LOG2_E) - m2) for s in s_parts]
    denom = (jnp.sum(functools.reduce(jnp.add, p_parts), axis=-1, keepdims=True)
             + jnp.exp2(sk * LOG2_E - m2))
    o = None
    for j, p in enumerate(p_parts):
        ov = jnp.dot(p.astype(BF16), value_of(j), preferred_element_type=F32)
        o = ov if o is None else o + ov
    return o, 1.0 / denom


def _attn_ctx_kernel(sink_ref, q_ref, k_ref, v_ref, o_ref):
    scale = HEAD_DIM ** -0.5
    head = lambda ref, h: ref[:, h * HEAD_DIM:(h + 1) * HEAD_DIM]
    kt = [head(k_ref, kh).T.astype(BF16) for kh in range(ATT_KV_HEADS)]
    vv = [head(v_ref, kh).astype(BF16) for kh in range(ATT_KV_HEADS)]

    def scores(h):
        s = jnp.dot(head(q_ref, h).astype(BF16), kt[h // ATT_GROUPS], preferred_element_type=F32)
        return [s[:, j:j + LANE] for j in range(0, s.shape[1], LANE)]

    s_next = scores(0)
    for h in range(ATT_HEADS):
        s_parts = s_next
        if h + 1 < ATT_HEADS:
            s_next = scores(h + 1)
        v = vv[h // ATT_GROUPS]
        o, r = _softmax_pv(s_parts, sink_ref[h], scale, lambda j: v[j * LANE:(j + 1) * LANE])
        o_ref[:, h * HEAD_DIM:(h + 1) * HEAD_DIM] = (o * r).astype(o_ref.dtype)


def _attn_ctx(proj, sink, nb, s_len):
    qw = ATT_HEADS * HEAD_DIM
    kw = ATT_KV_HEADS * HEAD_DIM
    return pl.pallas_call(
        _attn_ctx_kernel,
        grid=(nb,),
        in_specs=[pl.BlockSpec(memory_space=pltpu.SMEM),
                  pl.BlockSpec((s_len, qw), lambda b: (b, 0)),
                  pl.BlockSpec((s_len, kw), lambda b: (b, qw // kw)),
                  pl.BlockSpec((s_len, kw), lambda b: (b, qw // kw + 1))],
        out_specs=pl.BlockSpec((s_len, qw), lambda b: (b, 0)),
        out_shape=jax.ShapeDtypeStruct((nb * s_len, qw), BF16),
        compiler_params=_params(("arbitrary",), 2 * s_len * (qw + 2 * kw) * 4 + 2 * s_len * qw * 2
                                + 8 * ATT_GROUPS * s_len * s_len * 4),
    )(sink, proj, proj, proj)


def _attn_lat_kernel(sink_ref, q_ref, k_ref, v_ref, kc_ref, vc_ref, cos_ref, sa_ref, sb_ref,
                     o_ref, kt_s, vb_s, qr_s):
    n = q_ref.shape[0]
    nqb = n // Q_BLOCK
    kh = pl.program_id(1)
    scale = HEAD_DIM ** -0.5
    wblocks = (Q_BLOCK + 2 * WINDOW) // Q_BLOCK

    def rope(x, rows):
        return (x * cos_ref[rows, :] + pltpu.roll(x, HEAD_DIM - ROPE_FREQS, 1) * sa_ref[rows, :]
                + pltpu.roll(x, ROPE_FREQS, 1) * sb_ref[rows, :])

    def prep(j, carry):
        rows = pl.ds(pl.multiple_of(j * Q_BLOCK, Q_BLOCK), Q_BLOCK)
        kt_s[j] = rope(k_ref[rows, :], rows).T.astype(BF16)
        for g in range(ATT_GROUPS):
            lanes = slice(g * HEAD_DIM, (g + 1) * HEAD_DIM)
            qr_s[rows, lanes] = rope(q_ref[rows, lanes], rows).astype(BF16)
        return carry

    lax.fori_loop(0, nqb, prep, 0)
    vb_s[...] = v_ref[...].astype(BF16)
    kct = kc_ref[...].T.astype(BF16)
    vc = vc_ref[...].astype(BF16)
    qi = lax.broadcasted_iota(jnp.int32, (Q_BLOCK, Q_BLOCK), 0)
    ki = lax.broadcasted_iota(jnp.int32, (Q_BLOCK, Q_BLOCK), 1)
    rel = qi - ki

    def block(i, carry):
        qs = pl.multiple_of(i * Q_BLOCK, Q_BLOCK)
        kb = jnp.clip(i - WINDOW // Q_BLOCK, 0, nqb - wblocks)
        biases = []
        for j in range(wblocks):
            off = (i - kb - j) * Q_BLOCK
            biases.append(jnp.where(jnp.abs(rel + off) <= WINDOW, 0.0, -jnp.inf).astype(F32))

        tr = ATT_TILE_ROWS
        tiles = [(g, r0) for g in range(ATT_GROUPS) for r0 in range(0, Q_BLOCK, tr)]

        def scores(g, r0):
            q = qr_s[pl.ds(qs + r0, tr), g * HEAD_DIM:(g + 1) * HEAD_DIM]
            parts = [jnp.dot(q, kt_s[kb + j], preferred_element_type=F32) + biases[j][r0:r0 + tr]
                     for j in range(wblocks)]
            s2 = jnp.dot(q, kct, preferred_element_type=F32)
            return parts + [s2[:, j * LANE:(j + 1) * LANE] for j in range(s2.shape[1] // LANE)]

        s_next = scores(*tiles[0])
        for t, (g, r0) in enumerate(tiles):
            s_parts = s_next
            if t + 1 < len(tiles):
                s_next = scores(*tiles[t + 1])
            o, r = _softmax_pv(s_parts, sink_ref[kh * ATT_GROUPS + g], scale, lambda j: (
                vb_s[pl.ds(pl.multiple_of((kb + j) * Q_BLOCK, Q_BLOCK), Q_BLOCK), :] if j < wblocks
                else vc[(j - wblocks) * LANE:(j - wblocks + 1) * LANE, :]))
            o_ref[pl.ds(qs + r0, tr), g * HEAD_DIM:(g + 1) * HEAD_DIM] = (o * r).astype(o_ref.dtype)
        return carry

    lax.fori_loop(0, nqb, block, 0, unroll=2)


def _attn_lat(proj, sink, cache_k, cache_v, layer, tables, nb, n):
    qw = ATT_HEADS * HEAD_DIM
    gw = ATT_GROUPS * HEAD_DIM
    past = cache_k.shape[2]
    kcol = qw // HEAD_DIM
    vcol = kcol + ATT_KV_HEADS
    cspec = pl.BlockSpec((None, None, past, HEAD_DIM), lambda b, h: (b, layer, 0, h))
    tspec = pl.BlockSpec((n, HEAD_DIM), lambda b, h: (0, 0))
    return pl.pallas_call(
        _attn_lat_kernel,
        grid=(nb, ATT_KV_HEADS),
        in_specs=[pl.BlockSpec(memory_space=pltpu.SMEM),
                  pl.BlockSpec((n, gw), lambda b, h: (b, h)),
                  pl.BlockSpec((n, HEAD_DIM), lambda b, h: (b, kcol + h)),
                  pl.BlockSpec((n, HEAD_DIM), lambda b, h: (b, vcol + h)),
                  cspec, cspec, tspec, tspec, tspec],
        out_specs=pl.BlockSpec((n, gw), lambda b, h: (b, h)),
        out_shape=jax.ShapeDtypeStruct((nb * n, qw), BF16),
        scratch_shapes=[pltpu.VMEM((n // Q_BLOCK, HEAD_DIM, Q_BLOCK), BF16),
                        pltpu.VMEM((n, HEAD_DIM), BF16), pltpu.VMEM((n, gw), BF16)],
        compiler_params=_params(("arbitrary", "arbitrary"),
                                2 * n * (gw + 2 * HEAD_DIM) * 4 + 2 * n * gw * 2 + 6 * n * HEAD_DIM * 4
                                + 4 * n * HEAD_DIM + n * gw * 2),
    )(sink, proj, proj, proj, cache_k, cache_v, *tables)


def _rope_tables(n):
    pos = jnp.arange(n)
    row = (pos // GRID_W).astype(F32)
    col = (pos % GRID_W).astype(F32)
    inv = ROPE_THETA ** (-jnp.arange(ROPE_FREQS, dtype=F32) / ROPE_FREQS)
    ang_r = row[:, None] * inv[None, :]
    ang_c = col[:, None] * inv[None, :]
    cr, sr, cc, sc = jnp.cos(ang_r), jnp.sin(ang_r), jnp.cos(ang_c), jnp.sin(ang_c)
    z = jnp.zeros_like(cr)
    cos = jnp.concatenate([cr, cr, cc, cc], axis=1)
    sa = jnp.concatenate([-sr, z, -sc, z], axis=1)
    sb = jnp.concatenate([z, sr, z, sc], axis=1)
    return cos, sa, sb


def _gla_kernel(*refs, has_init, seq):
    if has_init:
        (q_ref, k_ref, v_ref, g_ref, dec_ref, wf_ref, wb_ref, bd_ref, ng_ref, s0_ref,
         o_ref, qin_s, tot_s, o_s, kv_s, sin_s) = refs
        sfin_ref = None
    else:
        (q_ref, k_ref, v_ref, g_ref, dec_ref, wf_ref, wb_ref, bd_ref, ng_ref,
         o_ref, sfin_ref, qin_s, tot_s, o_s, kv_s, sin_s) = refs
    n = q_ref.shape[0]
    c = GLA_CHUNK
    nc = n // c
    blk = min(seq, GLA_BULK_ROWS)
    qscale = GLA_DK ** -0.5
    shift = c.bit_length() - 1
    ri = lax.broadcasted_iota(jnp.int32, (blk, blk), 0)
    ci = lax.broadcasted_iota(jnp.int32, (blk, blk), 1)
    same = lax.shift_right_logical(ri, shift) == lax.shift_right_logical(ci, shift)
    keep = (same & (ri >= ci), same & (ri <= ci))
    sums = tuple(jnp.concatenate([keep[d].astype(BF16), same.astype(BF16)], axis=0) for d in range(2))
    wdec = jnp.concatenate([wf_ref[...], wb_ref[...]], axis=1)
    bdec = jnp.concatenate([bd_ref[0], bd_ref[1]], axis=1)

    def split3(x):
        hi = x.astype(BF16)
        r1 = x - hi.astype(F32)
        mid = r1.astype(BF16)
        return jnp.concatenate([hi, mid, (r1 - mid.astype(F32)).astype(BF16)], axis=1)

    def fold3(y):
        y = (y[:, :LANE] + y[:, LANE:2 * LANE]) + y[:, 2 * LANE:]
        return y[:blk], y[blk:]

    def chain(rows, chunk0, d, z, qc, kc, vc):
        st = {}

        def log_decay():
            st["p"] = split3(_log_sigmoid(z[:, d * GLA_DK:(d + 1) * GLA_DK]) * (1.0 / GLA_TAU))

        def chunk_sums():
            st["y"] = jnp.dot(sums[d], st["p"], preferred_element_type=F32)

        def decayed():
            cum, tot = fold3(st["y"])
            tot_s[d, rows, :] = tot
            st["q"] = (qc * jnp.exp(cum)).astype(BF16)
            st["k"] = (kc * jnp.exp(-cum)).astype(BF16)
            qin_s[d, rows, :] = st["q"]
            st["kd"] = (kc * jnp.exp(tot - cum)).astype(BF16)

        def scores():
            st["a"] = lax.dot_general(st["q"], st["k"], NT_DIMS, preferred_element_type=F32)

        def mask():
            st["att"] = jnp.where(keep[d], st["a"], 0.0).astype(BF16)

        def within():
            o_s[d, rows, :] = jnp.dot(st["att"], vc, preferred_element_type=F32)
            for u in range(blk // c):
                cr = slice(u * c, (u + 1) * c)
                kv_s[d, chunk0 + u] = lax.dot_general(vc[cr], st["kd"][cr], TN_DIMS,
                                                      preferred_element_type=F32)

        return [log_decay, chunk_sums, decayed, scores, mask, within]

    per_step = 2 if (n // blk) % 2 == 0 else 1

    def bulk(i, carry):
        chains = []
        for b in range(per_step):
            block = i * per_step + b
            rows = pl.ds(pl.multiple_of(block * blk, blk), blk)
            z = jnp.dot(dec_ref[rows, :].astype(BF16), wdec, preferred_element_type=F32) + bdec
            vc = v_ref[rows, :].astype(BF16)
            chains += [chain(rows, block * (blk // c), d, z, q_ref[rows, :] * qscale, k_ref[rows, :], vc)
                       for d in range(2)]
        nstage = len(chains[0])
        for t in range(nstage + len(chains) - 1):
            for lag, stages in enumerate(chains):
                if 0 <= t - lag < nstage:
                    stages[t - lag]()
        return carry

    lax.fori_loop(0, n // (blk * per_step), bulk, 0)

    ncs = seq // c
    for s in range(n // seq):
        def scan(j, carry, base=s * ncs):
            sf, sb = carry
            jf = base + j
            jb = base + ncs - 1 - j
            sin_s[0, jf] = sf.astype(BF16)
            sin_s[1, jb] = sb.astype(BF16)
            sf = jnp.exp(tot_s[0, pl.ds(jf * c, 1), :]) * sf + kv_s[0, jf]
            sb = jnp.exp(tot_s[1, pl.ds(jb * c, 1), :]) * sb + kv_s[1, jb]
            return sf, sb

        if has_init:
            init = (s0_ref[s, 0].T, s0_ref[s, 1].T)
        else:
            init = (jnp.zeros((LANE, GLA_DK), F32), jnp.zeros((LANE, GLA_DK), F32))
        sf, sb = lax.fori_loop(0, ncs, scan, init)
        if sfin_ref is not None:
            sfin_ref[s, 0] = sf.T
            sfin_ref[s, 1] = sb.T

    group = 4 if nc % 4 == 0 else 1

    def carried(g):
        terms = []
        for j in range(g * group, (g + 1) * group):
            rows = slice(j * c, (j + 1) * c)
            terms.append((rows, [(o_s[d, rows, :], lax.dot_general(qin_s[d, rows, :], sin_s[d, j], NT_DIMS,
                                                                   preferred_element_type=F32))
                                 for d in range(2)]))
        return terms

    nxt = carried(0)
    for g in range(nc // group):
        cur = nxt
        if g + 1 < nc // group:
            nxt = carried(g + 1)
        for rows, ((of, cf), (ob, cb)) in cur:
            o_s[0, rows, :] = (of + cf) + (ob + cb)

    ew = min(n, GLA_ELEMENTWISE_ROWS)

    def finish(i, carry):
        rows = pl.ds(pl.multiple_of(i * ew, ew), ew)
        o = o_s[0, rows, :]
        o = o * lax.rsqrt(jnp.mean(o * o, axis=-1, keepdims=True) + EPS) * ng_ref[...]
        g = g_ref[rows, :]
        o_ref[rows, :] = (o * (g * _sigmoid(g))).astype(o_ref.dtype)
        return carry

    lax.fori_loop(0, n // ew, finish, 0)


def _gla(proj, proj_tail, dec_col, wf, wb, b_dec, norm_g, nb, n, init=None, layer=0):
    hh = GLA_HEADS
    q0 = (ATT_HEADS + 2 * ATT_KV_HEADS) * HEAD_DIM // LANE
    bt = _seqs_per_step(nb, n)
    rows = bt * n
    nc = rows // GLA_CHUNK
    col = lambda base: pl.BlockSpec((rows, LANE), lambda b, h: (b, base + h))
    in_specs = [col(q0), col(q0 + hh), col(q0 + 2 * hh), col(q0 + 3 * hh),
                pl.BlockSpec((rows, LANE), lambda b, h: (b, dec_col)),
                pl.BlockSpec((None, LANE, LANE), lambda b, h: (h, 0, 0)),
                pl.BlockSpec((None, LANE, LANE), lambda b, h: (h, 0, 0)),
                pl.BlockSpec((2, None, 1, LANE), lambda b, h: (0, h, 0, 0)),
                pl.BlockSpec((None, 1, LANE), lambda b, h: (h, 0, 0))]
    args = [proj, proj, proj, proj, proj_tail, wf, wb, b_dec, norm_g]
    o_shape = jax.ShapeDtypeStruct((nb * n, hh * LANE), BF16)
    o_spec = pl.BlockSpec((rows, LANE), lambda b, h: (b, h))
    if init is not None:
        in_specs.append(pl.BlockSpec((bt, None, 2, None, GLA_DK, LANE),
                                     lambda b, h: (b, layer, 0, h, 0, 0)))
        args.append(init)
        out_shape, out_specs = o_shape, o_spec
    else:
        out_shape = (o_shape, jax.ShapeDtypeStruct((nb, 2, hh, GLA_DK, LANE), F32))
        out_specs = (o_spec, pl.BlockSpec((bt, 2, None, GLA_DK, LANE), lambda b, h: (b, 0, h, 0, 0)))
    return pl.pallas_call(
        functools.partial(_gla_kernel, has_init=init is not None, seq=n),
        grid=(nb // bt, hh),
        in_specs=in_specs,
        out_specs=out_specs,
        out_shape=out_shape,
        scratch_shapes=[pltpu.VMEM((2, rows, GLA_DK), BF16),
                        pltpu.VMEM((2, rows, GLA_DK), F32),
                        pltpu.VMEM((2, rows, LANE), F32),
                        pltpu.VMEM((2, nc, LANE, GLA_DK), F32),
                        pltpu.VMEM((2, nc, LANE, GLA_DK), BF16)],
        compiler_params=_params(("arbitrary", "arbitrary"),
                                2 * 5 * rows * LANE * 4 + 2 * rows * LANE * 2 + rows * LANE * (2 + 8 + 16)
                                + nc * LANE * GLA_DK * 12 + 4 * 2**20),
    )(*args)


def _scan_levels(n):
    levels = []
    size = n
    while size > SCAN_TOP_ROWS:
        radix = 4 if not levels else 8
        levels.append((size, radix))
        size //= radix
    return levels, size


def _linear_scan(a, u, lv, n, h0, rev):
    levels, top = _scan_levels(n)

    def sweep(a_l, u_l, size, radix, body):
        m = size // radix
        sb = min(m, SCAN_SLAB_ROWS)
        first = radix - 1 if rev else 0
        order = list(range(radix - 2, -1, -1)) if rev else list(range(1, radix))

        def blk(j, carry):
            sl = lambda r: pl.ds(j * (sb * radix) + r, sb, stride=radix)
            body(sl, pl.ds(pl.multiple_of(j * sb, sb), sb), first, order)
            return carry

        if m // sb == 1:
            blk(0, 0)
        else:
            lax.fori_loop(0, m // sb, blk, 0)

    src_a, src_u = a, u
    for l, (size, radix) in enumerate(levels):
        nxt = lv[l]

        def up(sl, dense, first, order, src_a=src_a, src_u=src_u, nxt=nxt):
            pa = src_a[sl(first), :]
            pu = src_u[sl(first), :]
            for r in order:
                ar = src_a[sl(r), :]
                pu = ar * pu + src_u[sl(r), :]
                pa = ar * pa
                src_a[sl(r), :] = pa
                src_u[sl(r), :] = pu
            nxt[0, dense, :] = pa
            nxt[1, dense, :] = pu

        sweep(src_a, src_u, size, radix, up)
        src_a, src_u = nxt.at[0], nxt.at[1]

    carry = h0
    top_c = lv[len(levels) - 1].at[2]
    for g in (range(top - 1, -1, -1) if rev else range(top)):
        top_c[g:g + 1, :] = carry
        carry = src_a[g:g + 1, :] * carry + src_u[g:g + 1, :]

    for l in range(len(levels) - 1, -1, -1):
        size, radix = levels[l]
        a_l, u_l = (a, u) if l == 0 else (lv[l - 1].at[0], lv[l - 1].at[1])
        c_l = lv[l].at[2]
        c_below = None if l == 0 else lv[l - 1].at[2]

        def down(sl, dense, first, order, a_l=a_l, u_l=u_l, c_l=c_l, c_below=c_below):
            cin = c_l[dense, :]
            prev = cin
            for r in [first] + order:
                h = u_l[sl(r), :] + a_l[sl(r), :] * cin
                if c_below is None:
                    u_l[sl(r), :] = h
                else:
                    c_below[sl(r), :] = prev
                prev = h

        sweep(a_l, u_l, size, radix, down)
    return carry


def _lru_kernel(*refs, has_init, seq):
    if has_init:
        (x_ref, y_ref, cw_ref, cb_ref, wr_ref, br_ref, wi_ref, bi_ref, lam_ref, h0_ref,
         o_ref, xp_s, a_s, u_s, *lv_s) = refs
        hfin_ref = None
    else:
        (x_ref, y_ref, cw_ref, cb_ref, wr_ref, br_ref, wi_ref, bi_ref, lam_ref,
         o_ref, hfin_ref, xp_s, a_s, u_s, *lv_s) = refs
    n = x_ref.shape[0]
    nseq = n // seq
    pad = SUBLANE
    xp_s[0:pad, :] = jnp.zeros((pad, LANE), F32)
    xp_s[pad + n:pad + n + pad, :] = jnp.zeros((pad, LANE), F32)
    xp_s[pad:pad + n, :] = x_ref[...]
    t = lax.broadcasted_iota(jnp.int32, (n, 1), 0)
    xc = cb_ref[...]
    for j in range(LRU_CONV):
        off = j - CONV_LEFT
        tap = xp_s[pl.ds(pad + off, n), :]
        edge = [s * seq + e for s in range(1, nseq) for e in range(0, -off)] if off < 0 else \
               [s * seq - 1 - e for s in range(1, nseq) for e in range(0, off)]
        if edge:
            tap = jnp.where(functools.reduce(jnp.logical_or, [t == e for e in edge]), 0.0, tap)
        xc = xc + tap * cw_ref[j:j + 1, :]
    xcb = xc.astype(BF16)
    for d in range(2):
        r = _sigmoid(jnp.dot(xcb, wr_ref[d], preferred_element_type=F32) + br_ref[d])
        i = _sigmoid(jnp.dot(xcb, wi_ref[d], preferred_element_type=F32) + bi_ref[d])
        log_a = -LRU_C * r * _softplus(-lam_ref[d])
        a = jnp.exp(log_a)
        a_s[d] = a
        v = -jnp.tanh(log_a) * (a * a + 1.0)
        u_s[d] = jnp.where(v > 0.0, v * lax.rsqrt(v), 0.0) * (i * xc)

    for s in range(nseq):
        rows = pl.ds(s * seq, seq)
        for d in range(2):
            h0 = h0_ref[s, d] if has_init else jnp.zeros((1, LANE), F32)
            h_last = _linear_scan(a_s.at[d, rows], u_s.at[d, rows], [lv.at[d] for lv in lv_s], seq, h0,
                                  rev=d == 1)
            if hfin_ref is not None:
                hfin_ref[s, d] = h_last
    o_ref[...] = ((u_s[0] + u_s[1]) * jax.nn.gelu(y_ref[...])).astype(o_ref.dtype)


def _lru(proj, conv_w, conv_b, w_r, b_r, w_i, b_i, lam, nb, n, init=None, layer=0):
    kb = conv_w.shape[1] // LRU_BLOCK
    bt = _seqs_per_step(nb, n)
    rows = bt * n
    vec = lambda r: pl.BlockSpec((r, LANE), lambda b, k: (0, k))
    vec2 = pl.BlockSpec((2, 1, LANE), lambda b, k: (0, 0, k))
    wspec = pl.BlockSpec((2, None, LRU_BLOCK, LRU_BLOCK), lambda b, k: (0, k, 0, 0))
    in_specs = [pl.BlockSpec((rows, LANE), lambda b, k: (b, k)),
                pl.BlockSpec((rows, LANE), lambda b, k: (b, kb + k)),
                vec(LRU_CONV), vec(1), wspec, vec2, wspec, vec2, vec2]
    args = [proj, proj, conv_w, conv_b, w_r, b_r, w_i, b_i, lam]
    o_shape = jax.ShapeDtypeStruct((nb * n, kb * LANE), BF16)
    o_spec = pl.BlockSpec((rows, LANE), lambda b, k: (b, k))
    if init is not None:
        in_specs.append(pl.BlockSpec((bt, None, 2, 1, LANE), lambda b, k: (b, layer, 0, 0, k)))
        args.append(init)
        out_shape, out_specs = o_shape, o_spec
    else:
        out_shape = (o_shape, jax.ShapeDtypeStruct((nb, 2, 1, kb * LANE), F32))
        out_specs = (o_spec, pl.BlockSpec((bt, 2, 1, LANE), lambda b, k: (b, 0, 0, k)))
    return pl.pallas_call(
        functools.partial(_lru_kernel, has_init=init is not None, seq=n),
        grid=(nb // bt, kb),
        in_specs=in_specs,
        out_specs=out_specs,
        out_shape=out_shape,
        scratch_shapes=[pltpu.VMEM((rows + 2 * SUBLANE, LANE), F32),
                        pltpu.VMEM((2, rows, LANE), F32), pltpu.VMEM((2, rows, LANE), F32)]
                       + [pltpu.VMEM((2, 3, size // radix, LANE), F32) for size, radix in _scan_levels(n)[0]],
        compiler_params=_params(("arbitrary", "arbitrary"), 2 * 2 * rows * LANE * 4 + 12 * rows * LANE * 4),
    )(*args)


def kernel(x_prompt, x_sample, cache_attn_k, cache_attn_v, state_gla, state_lru, c, c_ctx, w_ada, b_ada, norm1_g, w_in, attn_sink, gla_w_decay, gla_b_decay, gla_norm_g, lru_conv_w, lru_conv_b, lru_w_rgate, lru_b_rgate, lru_w_igate, lru_b_igate, lru_lambda, w_out, norm2_g, w_gu, w_down, final_norm_g):
    nb_c, s_len, d = x_prompt.shape
    nb_l, n_lat, _ = x_sample.shape
    depth = w_in.shape[0]
    lru_w = lru_conv_w.shape[2]
    qw, kw = ATT_HEADS * HEAD_DIM, ATT_KV_HEADS * HEAD_DIM
    gdec0 = qw + 2 * kw + 4 * GLA_HEADS * GLA_DK
    gdec1 = gdec0 + 2 * GLA_LOWRANK
    in_tile = 512
    tail_w = 2 * lru_w + 2 * GLA_LOWRANK
    tail_pad = -(-tail_w // LANE) * LANE
    assert gdec0 % in_tile == 0 and w_in.shape[2] == gdec0 + tail_w

    w_in_b = w_in.astype(BF16)
    w_tail_b = jnp.concatenate(
        [w_in_b[:, :, gdec1:], w_in_b[:, :, gdec0:gdec1],
         jnp.zeros((depth, d, tail_pad - tail_w), BF16)], axis=2)
    dec_col = 2 * lru_w // LANE
    w_out_b = w_out.astype(BF16)
    w_gu_b = w_gu.astype(BF16)
    w_down_b = w_down.astype(BF16)

    cvecs = jnp.concatenate([c_ctx[None], c, jnp.zeros((MOD_ROWS - 1 - nb_l, d), F32)], axis=0)
    modr = _ada(cvecs, w_ada, b_ada).reshape(depth, MOD_ROWS, 6, 1, d)
    row_ctx = _ModRows(0, nb_c * s_len)
    row_lat = _ModRows(1, n_lat)

    tables = _rope_tables(n_lat)
    cache_k = cache_attn_k.reshape(nb_l, depth, cache_attn_k.shape[2], kw)
    cache_v = cache_attn_v.reshape(nb_l, depth, cache_attn_v.shape[2], kw)
    lru_init = state_lru.reshape(nb_l, depth, 2, 1, lru_w)

    xc = x_prompt.reshape(nb_c * s_len, d)
    xl = x_sample.reshape(nb_l * n_lat, d)
    new_k, new_v, new_gla, new_lru = [], [], [], []
    for l in range(depth):
        wdec = gla_w_decay[l].reshape(2, GLA_LOWRANK, GLA_HEADS, GLA_DK).transpose(0, 2, 1, 3)
        zpad = jnp.zeros((GLA_HEADS, GLA_LOWRANK, GLA_DK), F32)
        zrest = jnp.zeros((GLA_HEADS, LANE - 2 * GLA_LOWRANK, GLA_DK), F32)
        wf = jnp.concatenate([wdec[0], zpad, zrest], axis=1).astype(BF16)
        wb = jnp.concatenate([zpad, wdec[1], zrest], axis=1).astype(BF16)
        b_dec = gla_b_decay[l].reshape(2, GLA_HEADS, 1, GLA_DK)
        ng = gla_norm_g[l].reshape(GLA_HEADS, 1, LANE)
        lru_args = (lru_conv_w[l], lru_conv_b[l].reshape(1, lru_w),
                    lru_w_rgate[l].astype(BF16), lru_b_rgate[l].reshape(2, 1, lru_w),
                    lru_w_igate[l].astype(BF16), lru_b_igate[l].reshape(2, 1, lru_w),
                    lru_lambda[l].reshape(2, 1, lru_w))
        sink = attn_sink[l]

        def ffn(x, mix, row_of):
            x = _out_proj(*mix, w_out_b, x, modr, l, row_of, 2)
            h2 = _norm_mod(x, norm2_g[l], modr, l, row_of, 4, 3)
            act = _gate_up(h2, w_gu_b, l)
            return _down_proj(act, w_down_b, x, modr, l, row_of, 5)

        h = _norm_mod(xc, norm1_g[l], modr, l, row_ctx, 1, 0)
        proj = _in_proj(h, w_in_b, l, gdec0)
        tail = _in_proj(h, w_tail_b, l, tail_pad)
        attn = _attn_ctx(proj, sink, nb_c, s_len)
        gla, gla_fin = _gla(proj, tail, dec_col, wf, wb, b_dec, ng, nb_c, s_len)
        lru, lru_fin = _lru(tail, *lru_args, nb_c, s_len)
        new_k.append(proj[:, qw:qw + kw].reshape(nb_c, s_len, ATT_KV_HEADS, HEAD_DIM))
        new_v.append(proj[:, qw + kw:qw + 2 * kw].reshape(nb_c, s_len, ATT_KV_HEADS, HEAD_DIM))
        new_gla.append(gla_fin)
        new_lru.append(lru_fin.reshape(nb_c, 2, lru_w))
        xc = ffn(xc, (attn, gla, lru), row_ctx)

        h = _norm_mod(xl, norm1_g[l], modr, l, row_lat, 1, 0)
        proj = _in_proj(h, w_in_b, l, gdec0)
        tail = _in_proj(h, w_tail_b, l, tail_pad)
        attn = _attn_lat(proj, sink, cache_k, cache_v, l, tables, nb_l, n_lat)
        gla = _gla(proj, tail, dec_col, wf, wb, b_dec, ng, nb_l, n_lat, init=state_gla, layer=l)
        lru = _lru(tail, *lru_args, nb_l, n_lat, init=lru_init, layer=l)
        xl = ffn(xl, (attn, gla, lru), row_lat)

    y_prompt = _final_norm(xc, final_norm_g).reshape(nb_c, s_len, d)
    y_sample = _final_norm(xl, final_norm_g).reshape(nb_l, n_lat, d)
    return (y_prompt, y_sample, jnp.stack(new_k, axis=1), jnp.stack(new_v, axis=1),
            jnp.stack(new_gla, axis=1), jnp.stack(new_lru, axis=1))
```

```python
import functools
from typing import NamedTuple

import jax
import jax.numpy as jnp
from jax import lax
from jax.experimental import pallas as pl
from jax.experimental.pallas import tpu as pltpu

F32 = jnp.float32
BF16 = jnp.bfloat16

EPS = 1e-6
HEAD_DIM = 128
ATT_HEADS = 16
ATT_KV_HEADS = 4
ATT_GROUPS = ATT_HEADS // ATT_KV_HEADS
WINDOW = 128
Q_BLOCK = 128
ATT_TILE_ROWS = 128
GRID_W = 64
ROPE_THETA = 10000.0
ROPE_FREQS = HEAD_DIM // 4
GLA_HEADS = 8
GLA_DK = 128
GLA_LOWRANK = 16
GLA_TAU = 16.0
GLA_CHUNK = 64
GLA_BULK_ROWS = 256
GLA_ELEMENTWISE_ROWS = 512
LRU_BLOCK = 128
LRU_CONV = 4
CONV_LEFT = 2
LRU_C = 8.0
MIXER_ROWS_PER_STEP = 1024
SCAN_TOP_ROWS = 8
SCAN_SLAB_ROWS = 64

LANE = 128
SUBLANE = 8
VMEM_BYTES_V7X = 64 * 2**20
MXU_COLS = 256
GATE_UP_SLAB_ROWS = 1024
MOD_ROWS = 16

LOG2_E = 1.4426950408889634
NT_DIMS = (((1,), (1,)), ((), ()))
TN_DIMS = (((0,), (0,)), ((), ()))


def _params(sem, est_bytes):
    limit = int(min(VMEM_BYTES_V7X - 4 * 2**20, max(est_bytes + 8 * 2**20, 32 * 2**20)))
    return pltpu.CompilerParams(dimension_semantics=sem, vmem_limit_bytes=limit)


def _seqs_per_step(nb, n):
    bt = max(1, MIXER_ROWS_PER_STEP // n)
    while nb % bt:
        bt -= 1
    return bt


def _softplus(x):
    return jnp.maximum(x, 0.0) + jnp.log1p(jnp.exp(-jnp.abs(x)))


def _log_sigmoid(x):
    return -_softplus(-x)


def _sigmoid(x):
    return 0.5 * jnp.tanh(0.5 * x) + 0.5


def _ada_kernel(c_ref, w_ref, b_ref, o_ref):
    cv = c_ref[...]
    s = (cv * jax.nn.sigmoid(cv)).astype(BF16)
    o_ref[...] = jnp.dot(s, w_ref[...].astype(BF16), preferred_element_type=F32) + b_ref[...]


def _ada(cvecs, w_ada, b_ada):
    nl, d, n6 = w_ada.shape
    tn = 512
    return pl.pallas_call(
        _ada_kernel,
        grid=(nl, n6 // tn),
        in_specs=[pl.BlockSpec((MOD_ROWS, d), lambda l, j: (0, 0)),
                  pl.BlockSpec((None, d, tn), lambda l, j: (l, 0, j)),
                  pl.BlockSpec((None, 1, tn), lambda l, j: (l, 0, j))],
        out_specs=pl.BlockSpec((None, MOD_ROWS, tn), lambda l, j: (l, 0, j)),
        out_shape=jax.ShapeDtypeStruct((nl, MOD_ROWS, n6), F32),
        compiler_params=_params(("arbitrary", "arbitrary"), 2 * d * tn * 4 + d * tn * 2),
    )(cvecs, w_ada, b_ada.reshape(nl, 1, n6))


def _norm_mod_kernel(x_ref, g_ref, sc_ref, sh_ref, o_ref):
    x = x_ref[...]
    y = x * lax.rsqrt(jnp.mean(x * x, axis=-1, keepdims=True) + EPS) * g_ref[...]
    o_ref[...] = (y * (1.0 + sc_ref[...]) + sh_ref[...]).astype(o_ref.dtype)


def _norm_kernel(x_ref, g_ref, o_ref):
    x = x_ref[...]
    y = x * lax.rsqrt(jnp.mean(x * x, axis=-1, keepdims=True) + EPS) * g_ref[...]
    o_ref[...] = y.astype(o_ref.dtype)


class _ModRows(NamedTuple):
    first: int
    segment: int

    def tile(self, m, want):
        tm = min(want, m, self.segment)
        assert m % tm == 0 and self.segment % tm == 0
        return tm

    def __call__(self, row0):
        return self.first + row0 // self.segment


def _norm_mod(x, g, modr, layer, row_of, sc_idx, sh_idx, tr=512):
    m, d = x.shape
    tr = row_of.tile(m, tr)
    mspec = lambda k: pl.BlockSpec((None, None, None, 1, d), lambda i: (layer, row_of(i * tr), k, 0, 0))
    return pl.pallas_call(
        _norm_mod_kernel,
        grid=(m // tr,),
        in_specs=[pl.BlockSpec((tr, d), lambda i: (i, 0)),
                  pl.BlockSpec((1, d), lambda i: (0, 0)),
                  mspec(sc_idx), mspec(sh_idx)],
        out_specs=pl.BlockSpec((tr, d), lambda i: (i, 0)),
        out_shape=jax.ShapeDtypeStruct((m, d), BF16),
        compiler_params=_params(("arbitrary",), 2 * tr * d * 6 + 2 * tr * d * 4),
    )(x, g.reshape(1, d), modr, modr)


def _final_norm(x, g, tr=512):
    m, d = x.shape
    tr = min(tr, m)
    return pl.pallas_call(
        _norm_kernel,
        grid=(m // tr,),
        in_specs=[pl.BlockSpec((tr, d), lambda i: (i, 0)),
                  pl.BlockSpec((1, d), lambda i: (0, 0))],
        out_specs=pl.BlockSpec((tr, d), lambda i: (i, 0)),
        out_shape=jax.ShapeDtypeStruct((m, d), F32),
        compiler_params=_params(("arbitrary",), 2 * tr * d * 8 + 2 * tr * d * 4),
    )(x, g.reshape(1, d))


def _mm_kernel(x_ref, w_ref, o_ref):
    for j0 in range(0, o_ref.shape[1], MXU_COLS):
        cols = slice(j0, min(j0 + MXU_COLS, o_ref.shape[1]))
        o_ref[:, cols] = jnp.dot(x_ref[...], w_ref[:, cols], preferred_element_type=F32)


def _in_proj(h, w, layer, n, tm=2048, tn=512):
    m, k = h.shape
    assert n <= w.shape[2] and n % LANE == 0
    if n % tn:
        assert n == w.shape[2]
        tm, tn = 512, n
    tm = min(tm, m)
    return pl.pallas_call(
        _mm_kernel,
        grid=(m // tm, n // tn),
        in_specs=[pl.BlockSpec((tm, k), lambda i, j: (i, 0)),
                  pl.BlockSpec((None, k, tn), lambda i, j: (layer, 0, j))],
        out_specs=pl.BlockSpec((tm, tn), lambda i, j: (i, j)),
        out_shape=jax.ShapeDtypeStruct((m, n), F32),
        compiler_params=_params(("arbitrary", "arbitrary"), 2 * (tm * k * 2 + k * tn * 2 + tm * tn * 4)),
    )(h, w)


def _out_proj_kernel(a_ref, b_ref, c_ref, w_ref, x_ref, g_ref, o_ref):
    ka, kb = a_ref.shape[1], b_ref.shape[1]
    for j0 in range(0, o_ref.shape[1], MXU_COLS):
        cols = slice(j0, j0 + MXU_COLS)
        acc = jnp.dot(a_ref[...], w_ref[0:ka, cols], preferred_element_type=F32)
        acc += jnp.dot(b_ref[...], w_ref[ka:ka + kb, cols], preferred_element_type=F32)
        acc += jnp.dot(c_ref[...], w_ref[ka + kb:, cols], preferred_element_type=F32)
        o_ref[:, cols] = x_ref[:, cols] + g_ref[:, cols] * acc


def _out_proj(attn, gla, lru, w, x, modr, layer, row_of, gate_idx, tm=1024, tn=1024):
    m, d = x.shape
    k = w.shape[1]
    tm = row_of.tile(m, tm)
    xs = lambda a: pl.BlockSpec((tm, a.shape[1]), lambda i, j: (i, 0))
    return pl.pallas_call(
        _out_proj_kernel,
        grid=(m // tm, d // tn),
        in_specs=[xs(attn), xs(gla), xs(lru),
                  pl.BlockSpec((None, k, tn), lambda i, j: (layer, 0, j)),
                  pl.BlockSpec((tm, tn), lambda i, j: (i, j)),
                  pl.BlockSpec((None, None, None, 1, tn),
                               lambda i, j: (layer, row_of(i * tm), gate_idx, 0, j))],
        out_specs=pl.BlockSpec((tm, tn), lambda i, j: (i, j)),
        out_shape=jax.ShapeDtypeStruct((m, d), F32),
        compiler_params=_params(("arbitrary", "arbitrary"),
                                2 * (tm * k * 2 + k * tn * 2 + 2 * tm * tn * 4) + tm * tn * 4),
    )(attn, gla, lru, w, x, modr)


def _gate_up_kernel(h_ref, wg_ref, wu_ref, o_ref):
    tm = h_ref.shape[0]
    slab = min(tm, GATE_UP_SLAB_ROWS)
    for r0 in range(0, tm, slab):
        h = h_ref[r0:r0 + slab, :]
        a = jnp.dot(h, wg_ref[...], preferred_element_type=F32)
        u = jnp.dot(h, wu_ref[...], preferred_element_type=F32)
        o_ref[r0:r0 + slab, :] = (a * _sigmoid(a) * u).astype(o_ref.dtype)


def _gate_up(h, w, layer, tm=2048, tn=256):
    m, k = h.shape
    f = w.shape[2] // 2
    tm = min(tm, m)
    nj = f // tn
    return pl.pallas_call(
        _gate_up_kernel,
        grid=(m // tm, nj),
        in_specs=[pl.BlockSpec((tm, k), lambda i, j: (i, 0)),
                  pl.BlockSpec((None, k, tn), lambda i, j: (layer, 0, j)),
                  pl.BlockSpec((None, k, tn), lambda i, j: (layer, 0, nj + j))],
        out_specs=pl.BlockSpec((tm, tn), lambda i, j: (i, j)),
        out_shape=jax.ShapeDtypeStruct((m, f), BF16),
        compiler_params=_params(("arbitrary", "arbitrary"),
                                2 * (tm * k * 2 + 2 * k * tn * 2 + tm * tn * 2) + 3 * tm * tn * 4),
    )(h, w, w)


def _down_kernel(a_ref, w_ref, x_ref, g_ref, o_ref):
    for j0 in range(0, o_ref.shape[1], MXU_COLS):
        cols = slice(j0, j0 + MXU_COLS)
        acc = jnp.dot(a_ref[...], w_ref[:, cols], preferred_element_type=F32)
        o_ref[:, cols] = x_ref[:, cols] + g_ref[:, cols] * acc


def _down_proj(act, w, x, modr, layer, row_of, gate_idx, tm=512, tn=512):
    m, d = x.shape
    k = w.shape[1]
    tm = row_of.tile(m, tm)
    return pl.pallas_call(
        _down_kernel,
        grid=(m // tm, d // tn),
        in_specs=[pl.BlockSpec((tm, k), lambda i, j: (i, 0)),
                  pl.BlockSpec((None, k, tn), lambda i, j: (layer, 0, j)),
                  pl.BlockSpec((tm, tn), lambda i, j: (i, j)),
                  pl.BlockSpec((None, None, None, 1, tn),
                               lambda i, j: (layer, row_of(i * tm), gate_idx, 0, j))],
        out_specs=pl.BlockSpec((tm, tn), lambda i, j: (i, j)),
        out_shape=jax.ShapeDtypeStruct((m, d), F32),
        compiler_params=_params(("arbitrary", "arbitrary"),
                                2 * (tm * k * 2 + k * tn * 2 + 2 * tm * tn * 4) + tm * tn * 4),
    )(act, w, x, modr)


def _softmax_pv(s_parts, sk, scale, value_of):
    m = functools.reduce(jnp.maximum, s_parts)
    m2 = jnp.maximum(jnp.max(m, axis=-1, keepdims=True) * scale, sk) * LOG2_E
    p_parts = [jnp.exp2(s * (scale * LOG2_E) - m2) for s in s_parts]
    denom = (jnp.sum(functools.reduce(jnp.add, p_parts), axis=-1, keepdims=True)
             + jnp.exp2(sk * LOG2_E - m2))
    o = None
    for j, p in enumerate(p_parts):
        ov = jnp.dot(p.astype(BF16), value_of(j), preferred_element_type=F32)
        o = ov if o is None else o + ov
    return o, 1.0 / denom


def _attn_ctx_kernel(sink_ref, q_ref, k_ref, v_ref, o_ref):
    scale = HEAD_DIM ** -0.5
    head = lambda ref, h: ref[:, h * HEAD_DIM:(h + 1) * HEAD_DIM]
    kt = [head(k_ref, kh).T.astype(BF16) for kh in range(ATT_KV_HEADS)]
    vv = [head(v_ref, kh).astype(BF16) for kh in range(ATT_KV_HEADS)]

    def scores(h):
        s = jnp.dot(head(q_ref, h).astype(BF16), kt[h // ATT_GROUPS], preferred_element_type=F32)
        return [s[:, j:j + LANE] for j in range(0, s.shape[1], LANE)]

    s_next = scores(0)
    for h in range(ATT_HEADS):
        s_parts = s_next
        if h + 1 < ATT_HEADS:
            s_next = scores(h + 1)
        v = vv[h // ATT_GROUPS]
        o, r = _softmax_pv(s_parts, sink_ref[h], scale, lambda j: v[j * LANE:(j + 1) * LANE])
        o_ref[:, h * HEAD_DIM:(h + 1) * HEAD_DIM] = (o * r).astype(o_ref.dtype)


def _attn_ctx(proj, sink, nb, s_len):
    qw = ATT_HEADS * HEAD_DIM
    kw = ATT_KV_HEADS * HEAD_DIM
    return pl.pallas_call(
        _attn_ctx_kernel,
        grid=(nb,),
        in_specs=[pl.BlockSpec(memory_space=pltpu.SMEM),
                  pl.BlockSpec((s_len, qw), lambda b: (b, 0)),
                  pl.BlockSpec((s_len, kw), lambda b: (b, qw // kw)),
                  pl.BlockSpec((s_len, kw), lambda b: (b, qw // kw + 1))],
        out_specs=pl.BlockSpec((s_len, qw), lambda b: (b, 0)),
        out_shape=jax.ShapeDtypeStruct((nb * s_len, qw), BF16),
        compiler_params=_params(("arbitrary",), 2 * s_len * (qw + 2 * kw) * 4 + 2 * s_len * qw * 2
                                + 8 * ATT_GROUPS * s_len * s_len * 4),
    )(sink, proj, proj, proj)


def _attn_lat_kernel(sink_ref, q_ref, k_ref, v_ref, kc_ref, vc_ref, cos_ref, sa_ref, sb_ref,
                     o_ref, kt_s, vb_s, qr_s):
    n = q_ref.shape[0]
    nqb = n // Q_BLOCK
    kh = pl.program_id(1)
    scale = HEAD_DIM ** -0.5
    wblocks = (Q_BLOCK + 2 * WINDOW) // Q_BLOCK

    def rope(x, rows):
        return (x * cos_ref[rows, :] + pltpu.roll(x, HEAD_DIM - ROPE_FREQS, 1) * sa_ref[rows, :]
                + pltpu.roll(x, ROPE_FREQS, 1) * sb_ref[rows, :])

    def prep(j, carry):
        rows = pl.ds(pl.multiple_of(j * Q_BLOCK, Q_BLOCK), Q_BLOCK)
        kt_s[j] = rope(k_ref[rows, :], rows).T.astype(BF16)
        for g in range(ATT_GROUPS):
            lanes = slice(g * HEAD_DIM, (g + 1) * HEAD_DIM)
            qr_s[rows, lanes] = rope(q_ref[rows, lanes], rows).astype(BF16)
        return carry

    lax.fori_loop(0, nqb, prep, 0)
    vb_s[...] = v_ref[...].astype(BF16)
    kct = kc_ref[...].T.astype(BF16)
    vc = vc_ref[...].astype(BF16)
    qi = lax.broadcasted_iota(jnp.int32, (Q_BLOCK, Q_BLOCK), 0)
    ki = lax.broadcasted_iota(jnp.int32, (Q_BLOCK, Q_BLOCK), 1)
    rel = qi - ki

    tr = ATT_TILE_ROWS
    per_step = 2 if nqb % 2 == 0 else 1

    def blocks(ii, carry):
        tiles = []
        for bb in range(per_step):
            i = ii * per_step + bb
            qs = pl.multiple_of(i * Q_BLOCK, Q_BLOCK)
            kb = jnp.clip(i - WINDOW // Q_BLOCK, 0, nqb - wblocks)
            biases = []
            for j in range(wblocks):
                off = (i - kb - j) * Q_BLOCK
                biases.append(jnp.where(jnp.abs(rel + off) <= WINDOW, 0.0, -jnp.inf).astype(F32))
            tiles += [(qs, kb, biases, g, r0) for g in range(ATT_GROUPS) for r0 in range(0, Q_BLOCK, tr)]

        def scores(qs, kb, biases, g, r0):
            q = qr_s[pl.ds(qs + r0, tr), g * HEAD_DIM:(g + 1) * HEAD_DIM]
            parts = [jnp.dot(q, kt_s[kb + j], preferred_element_type=F32) + biases[j][r0:r0 + tr]
                     for j in range(wblocks)]
            s2 = jnp.dot(q, kct, preferred_element_type=F32)
            return parts + [s2[:, j * LANE:(j + 1) * LANE] for j in range(s2.shape[1] // LANE)]

        s_next = scores(*tiles[0])
        for t, (qs, kb, _, g, r0) in enumerate(tiles):
            s_parts = s_next
            if t + 1 < len(tiles):
                s_next = scores(*tiles[t + 1])
            o, r = _softmax_pv(s_parts, sink_ref[kh * ATT_GROUPS + g], scale, lambda j: (
                vb_s[pl.ds(pl.multiple_of((kb + j) * Q_BLOCK, Q_BLOCK), Q_BLOCK), :] if j < wblocks
                else vc[(j - wblocks) * LANE:(j - wblocks + 1) * LANE, :]))
            o_ref[pl.ds(qs + r0, tr), g * HEAD_DIM:(g + 1) * HEAD_DIM] = (o * r).astype(o_ref.dtype)
        return carry

    lax.fori_loop(0, nqb // per_step, blocks, 0)


def _attn_lat(proj, sink, cache_k, cache_v, layer, tables, nb, n):
    qw = ATT_HEADS * HEAD_DIM
    gw = ATT_GROUPS * HEAD_DIM
    past = cache_k.shape[2]
    kcol = qw // HEAD_DIM
    vcol = kcol + ATT_KV_HEADS
    cspec = pl.BlockSpec((None, None, past, HEAD_DIM), lambda b, h: (b, layer, 0, h))
    tspec = pl.BlockSpec((n, HEAD_DIM), lambda b, h: (0, 0))
    return pl.pallas_call(
        _attn_lat_kernel,
        grid=(nb, ATT_KV_HEADS),
        in_specs=[pl.BlockSpec(memory_space=pltpu.SMEM),
                  pl.BlockSpec((n, gw), lambda b, h: (b, h)),
                  pl.BlockSpec((n, HEAD_DIM), lambda b, h: (b, kcol + h)),
                  pl.BlockSpec((n, HEAD_DIM), lambda b, h: (b, vcol + h)),
                  cspec, cspec, tspec, tspec, tspec],
        out_specs=pl.BlockSpec((n, gw), lambda b, h: (b, h)),
        out_shape=jax.ShapeDtypeStruct((nb * n, qw), BF16),
        scratch_shapes=[pltpu.VMEM((n // Q_BLOCK, HEAD_DIM, Q_BLOCK), BF16),
                        pltpu.VMEM((n, HEAD_DIM), BF16), pltpu.VMEM((n, gw), BF16)],
        compiler_params=_params(("arbitrary", "arbitrary"),
                                2 * n * (gw + 2 * HEAD_DIM) * 4 + 2 * n * gw * 2 + 6 * n * HEAD_DIM * 4
                                + 4 * n * HEAD_DIM + n * gw * 2),
    )(sink, proj, proj, proj, cache_k, cache_v, *tables)


def _rope_tables(n):
    pos = jnp.arange(n)
    row = (pos // GRID_W).astype(F32)
    col = (pos % GRID_W).astype(F32)
    inv = ROPE_THETA ** (-jnp.arange(ROPE_FREQS, dtype=F32) / ROPE_FREQS)
    ang_r = row[:, None] * inv[None, :]
    ang_c = col[:, None] * inv[None, :]
    cr, sr, cc, sc = jnp.cos(ang_r), jnp.sin(ang_r), jnp.cos(ang_c), jnp.sin(ang_c)
    z = jnp.zeros_like(cr)
    cos = jnp.concatenate([cr, cr, cc, cc], axis=1)
    sa = jnp.concatenate([-sr, z, -sc, z], axis=1)
    sb = jnp.concatenate([z, sr, z, sc], axis=1)
    return cos, sa, sb


def _gla_kernel(*refs, has_init, seq):
    if has_init:
        (q_ref, k_ref, v_ref, g_ref, dec_ref, wf_ref, wb_ref, bd_ref, ng_ref, s0_ref,
         o_ref, qin_s, tot_s, o_s, kv_s, sin_s) = refs
        sfin_ref = None
    else:
        (q_ref, k_ref, v_ref, g_ref, dec_ref, wf_ref, wb_ref, bd_ref, ng_ref,
         o_ref, sfin_ref, qin_s, tot_s, o_s, kv_s, sin_s) = refs
    n = q_ref.shape[0]
    c = GLA_CHUNK
    nc = n // c
    blk = min(seq, GLA_BULK_ROWS)
    qscale = GLA_DK ** -0.5
    shift = c.bit_length() - 1
    ri = lax.broadcasted_iota(jnp.int32, (blk, blk), 0)
    ci = lax.broadcasted_iota(jnp.int32, (blk, blk), 1)
    same = lax.shift_right_logical(ri, shift) == lax.shift_right_logical(ci, shift)
    keep = (same & (ri >= ci), same & (ri <= ci))
    sums = tuple(jnp.concatenate([keep[d].astype(BF16), same.astype(BF16)], axis=0) for d in range(2))
    wdec = jnp.concatenate([wf_ref[...], wb_ref[...]], axis=1)
    bdec = jnp.concatenate([bd_ref[0], bd_ref[1]], axis=1)

    def split3(x):
        hi = x.astype(BF16)
        r1 = x - hi.astype(F32)
        mid = r1.astype(BF16)
        return jnp.concatenate([hi, mid, (r1 - mid.astype(F32)).astype(BF16)], axis=1)

    def fold3(y):
        y = (y[:, :LANE] + y[:, LANE:2 * LANE]) + y[:, 2 * LANE:]
        return y[:blk], y[blk:]

    def chain(rows, chunk0, d, z, qc, kc, vc):
        st = {}

        def log_decay():
            st["p"] = split3(_log_sigmoid(z[:, d * GLA_DK:(d + 1) * GLA_DK]) * (1.0 / GLA_TAU))

        def chunk_sums():
            st["y"] = jnp.dot(sums[d], st["p"], preferred_element_type=F32)

        def decayed():
            cum, tot = fold3(st["y"])
            tot_s[d, rows, :] = tot
            st["q"] = (qc * jnp.exp(cum)).astype(BF16)
            st["k"] = (kc * jnp.exp(-cum)).astype(BF16)
            qin_s[d, rows, :] = st["q"]
            st["kd"] = (kc * jnp.exp(tot - cum)).astype(BF16)

        def scores():
            st["a"] = lax.dot_general(st["q"], st["k"], NT_DIMS, preferred_element_type=F32)

        def mask():
            st["att"] = jnp.where(keep[d], st["a"], 0.0).astype(BF16)

        def within():
            o_s[d, rows, :] = jnp.dot(st["att"], vc, preferred_element_type=F32)
            for u in range(blk // c):
                cr = slice(u * c, (u + 1) * c)
                kv_s[d, chunk0 + u] = lax.dot_general(vc[cr], st["kd"][cr], TN_DIMS,
                                                      preferred_element_type=F32)

        return [log_decay, chunk_sums, decayed, scores, mask, within]

    per_step = 2 if (n // blk) % 2 == 0 else 1

    def bulk(i, carry):
        chains = []
        for b in range(per_step):
            block = i * per_step + b
            rows = pl.ds(pl.multiple_of(block * blk, blk), blk)
            z = jnp.dot(dec_ref[rows, :].astype(BF16), wdec, preferred_element_type=F32) + bdec
            vc = v_ref[rows, :].astype(BF16)
            chains += [chain(rows, block * (blk // c), d, z, q_ref[rows, :] * qscale, k_ref[rows, :], vc)
                       for d in range(2)]
        nstage = len(chains[0])
        for t in range(nstage + len(chains) - 1):
            for lag, stages in enumerate(chains):
                if 0 <= t - lag < nstage:
                    stages[t - lag]()
        return carry

    lax.fori_loop(0, n // (blk * per_step), bulk, 0)

    ncs = seq // c
    for s in range(n // seq):
        def scan(j, carry, base=s * ncs):
            sf, sb = carry
            jf = base + j
            jb = base + ncs - 1 - j
            sin_s[0, jf] = sf.astype(BF16)
            sin_s[1, jb] = sb.astype(BF16)
            sf = jnp.exp(tot_s[0, pl.ds(jf * c, 1), :]) * sf + kv_s[0, jf]
            sb = jnp.exp(tot_s[1, pl.ds(jb * c, 1), :]) * sb + kv_s[1, jb]
            return sf, sb

        if has_init:
            init = (s0_ref[s, 0].T, s0_ref[s, 1].T)
        else:
            init = (jnp.zeros((LANE, GLA_DK), F32), jnp.zeros((LANE, GLA_DK), F32))
        sf, sb = lax.fori_loop(0, ncs, scan, init)
        if sfin_ref is not None:
            sfin_ref[s, 0] = sf.T
            sfin_ref[s, 1] = sb.T

    group = 4 if nc % 4 == 0 else 1

    def carried(g):
        terms = []
        for j in range(g * group, (g + 1) * group):
            rows = slice(j * c, (j + 1) * c)
            terms.append((rows, [(o_s[d, rows, :], lax.dot_general(qin_s[d, rows, :], sin_s[d, j], NT_DIMS,
                                                                   preferred_element_type=F32))
                                 for d in range(2)]))
        return terms

    nxt = carried(0)
    for g in range(nc // group):
        cur = nxt
        if g + 1 < nc // group:
            nxt = carried(g + 1)
        for rows, ((of, cf), (ob, cb)) in cur:
            o_s[0, rows, :] = (of + cf) + (ob + cb)

    ew = min(n, GLA_ELEMENTWISE_ROWS)

    def finish(i, carry):
        rows = pl.ds(pl.multiple_of(i * ew, ew), ew)
        o = o_s[0, rows, :]
        o = o * lax.rsqrt(jnp.mean(o * o, axis=-1, keepdims=True) + EPS) * ng_ref[...]
        g = g_ref[rows, :]
        o_ref[rows, :] = (o * (g * _sigmoid(g))).astype(o_ref.dtype)
        return carry

    lax.fori_loop(0, n // ew, finish, 0)


def _gla(proj, proj_tail, dec_col, wf, wb, b_dec, norm_g, nb, n, init=None, layer=0):
    hh = GLA_HEADS
    q0 = (ATT_HEADS + 2 * ATT_KV_HEADS) * HEAD_DIM // LANE
    bt = _seqs_per_step(nb, n)
    rows = bt * n
    nc = rows // GLA_CHUNK
    col = lambda base: pl.BlockSpec((rows, LANE), lambda b, h: (b, base + h))
    in_specs = [col(q0), col(q0 + hh), col(q0 + 2 * hh), col(q0 + 3 * hh),
                pl.BlockSpec((rows, LANE), lambda b, h: (b, dec_col)),
                pl.BlockSpec((None, LANE, LANE), lambda b, h: (h, 0, 0)),
                pl.BlockSpec((None, LANE, LANE), lambda b, h: (h, 0, 0)),
                pl.BlockSpec((2, None, 1, LANE), lambda b, h: (0, h, 0, 0)),
                pl.BlockSpec((None, 1, LANE), lambda b, h: (h, 0, 0))]
    args = [proj, proj, proj, proj, proj_tail, wf, wb, b_dec, norm_g]
    o_shape = jax.ShapeDtypeStruct((nb * n, hh * LANE), BF16)
    o_spec = pl.BlockSpec((rows, LANE), lambda b, h: (b, h))
    if init is not None:
        in_specs.append(pl.BlockSpec((bt, None, 2, None, GLA_DK, LANE),
                                     lambda b, h: (b, layer, 0, h, 0, 0)))
        args.append(init)
        out_shape, out_specs = o_shape, o_spec
    else:
        out_shape = (o_shape, jax.ShapeDtypeStruct((nb, 2, hh, GLA_DK, LANE), F32))
        out_specs = (o_spec, pl.BlockSpec((bt, 2, None, GLA_DK, LANE), lambda b, h: (b, 0, h, 0, 0)))
    return pl.pallas_call(
        functools.partial(_gla_kernel, has_init=init is not None, seq=n),
        grid=(nb // bt, hh),
        in_specs=in_specs,
        out_specs=out_specs,
        out_shape=out_shape,
        scratch_shapes=[pltpu.VMEM((2, rows, GLA_DK), BF16),
                        pltpu.VMEM((2, rows, GLA_DK), F32),
                        pltpu.VMEM((2, rows, LANE), F32),
                        pltpu.VMEM((2, nc, LANE, GLA_DK), F32),
                        pltpu.VMEM((2, nc, LANE, GLA_DK), BF16)],
        compiler_params=_params(("arbitrary", "arbitrary"),
                                2 * 5 * rows * LANE * 4 + 2 * rows * LANE * 2 + rows * LANE * (2 + 8 + 16)
                                + nc * LANE * GLA_DK * 12 + 4 * 2**20),
    )(*args)


def _scan_levels(n):
    levels = []
    size = n
    while size > SCAN_TOP_ROWS:
        radix = 4 if not levels else 8
        levels.append((size, radix))
        size //= radix
    return levels, size


def _linear_scan(a, u, lv, n, h0, rev):
    levels, top = _scan_levels(n)

    def sweep(a_l, u_l, size, radix, body):
        m = size // radix
        sb = min(m, SCAN_SLAB_ROWS)
        first = radix - 1 if rev else 0
        order = list(range(radix - 2, -1, -1)) if rev else list(range(1, radix))

        def blk(j, carry):
            sl = lambda r: pl.ds(j * (sb * radix) + r, sb, stride=radix)
            body(sl, pl.ds(pl.multiple_of(j * sb, sb), sb), first, order)
            return carry

        if m // sb == 1:
            blk(0, 0)
        else:
            lax.fori_loop(0, m // sb, blk, 0)

    src_a, src_u = a, u
    for l, (size, radix) in enumerate(levels):
        nxt = lv[l]

        def up(sl, dense, first, order, src_a=src_a, src_u=src_u, nxt=nxt):
            pa = src_a[sl(first), :]
            pu = src_u[sl(first), :]
            for r in order:
                ar = src_a[sl(r), :]
                pu = ar * pu + src_u[sl(r), :]
                pa = ar * pa
                src_a[sl(r), :] = pa
                src_u[sl(r), :] = pu
            nxt[0, dense, :] = pa
            nxt[1, dense, :] = pu

        sweep(src_a, src_u, size, radix, up)
        src_a, src_u = nxt.at[0], nxt.at[1]

    carry = h0
    top_c = lv[len(levels) - 1].at[2]
    for g in (range(top - 1, -1, -1) if rev else range(top)):
        top_c[g:g + 1, :] = carry
        carry = src_a[g:g + 1, :] * carry + src_u[g:g + 1, :]

    for l in range(len(levels) - 1, -1, -1):
        size, radix = levels[l]
        a_l, u_l = (a, u) if l == 0 else (lv[l - 1].at[0], lv[l - 1].at[1])
        c_l = lv[l].at[2]
        c_below = None if l == 0 else lv[l - 1].at[2]

        def down(sl, dense, first, order, a_l=a_l, u_l=u_l, c_l=c_l, c_below=c_below):
            cin = c_l[dense, :]
            prev = cin
            for r in [first] + order:
                h = u_l[sl(r), :] + a_l[sl(r), :] * cin
                if c_below is None:
                    u_l[sl(r), :] = h
                else:
                    c_below[sl(r), :] = prev
                prev = h

        sweep(a_l, u_l, size, radix, down)
    return carry


def _lru_kernel(*refs, has_init, seq):
    if has_init:
        (x_ref, y_ref, cw_ref, cb_ref, wr_ref, br_ref, wi_ref, bi_ref, lam_ref, h0_ref,
         o_ref, xp_s, a_s, u_s, *lv_s) = refs
        hfin_ref = None
    else:
        (x_ref, y_ref, cw_ref, cb_ref, wr_ref, br_ref, wi_ref, bi_ref, lam_ref,
         o_ref, hfin_ref, xp_s, a_s, u_s, *lv_s) = refs
    n = x_ref.shape[0]
    nseq = n // seq
    pad = SUBLANE
    xp_s[0:pad, :] = jnp.zeros((pad, LANE), F32)
    xp_s[pad + n:pad + n + pad, :] = jnp.zeros((pad, LANE), F32)
    xp_s[pad:pad + n, :] = x_ref[...]
    t = lax.broadcasted_iota(jnp.int32, (n, 1), 0)
    xc = cb_ref[...]
    for j in range(LRU_CONV):
        off = j - CONV_LEFT
        tap = xp_s[pl.ds(pad + off, n), :]
        edge = [s * seq + e for s in range(1, nseq) for e in range(0, -off)] if off < 0 else \
               [s * seq - 1 - e for s in range(1, nseq) for e in range(0, off)]
        if edge:
            tap = jnp.where(functools.reduce(jnp.logical_or, [t == e for e in edge]), 0.0, tap)
        xc = xc + tap * cw_ref[j:j + 1, :]
    xcb = xc.astype(BF16)
    for d in range(2):
        r = _sigmoid(jnp.dot(xcb, wr_ref[d], preferred_element_type=F32) + br_ref[d])
        i = _sigmoid(jnp.dot(xcb, wi_ref[d], preferred_element_type=F32) + bi_ref[d])
        log_a = -LRU_C * r * _softplus(-lam_ref[d])
        a = jnp.exp(log_a)
        a_s[d] = a
        v = -jnp.tanh(log_a) * (a * a + 1.0)
        u_s[d] = jnp.where(v > 0.0, v * lax.rsqrt(v), 0.0) * (i * xc)

    for s in range(nseq):
        rows = pl.ds(s * seq, seq)
        for d in range(2):
            h0 = h0_ref[s, d] if has_init else jnp.zeros((1, LANE), F32)
            h_last = _linear_scan(a_s.at[d, rows], u_s.at[d, rows], [lv.at[d] for lv in lv_s], seq, h0,
                                  rev=d == 1)
            if hfin_ref is not None:
                hfin_ref[s, d] = h_last
    o_ref[...] = ((u_s[0] + u_s[1]) * jax.nn.gelu(y_ref[...])).astype(o_ref.dtype)


def _lru(proj, conv_w, conv_b, w_r, b_r, w_i, b_i, lam, nb, n, init=None, layer=0):
    kb = conv_w.shape[1] // LRU_BLOCK
    bt = _seqs_per_step(nb, n)
    rows = bt * n
    vec = lambda r: pl.BlockSpec((r, LANE), lambda b, k: (0, k))
    vec2 = pl.BlockSpec((2, 1, LANE), lambda b, k: (0, 0, k))
    wspec = pl.BlockSpec((2, None, LRU_BLOCK, LRU_BLOCK), lambda b, k: (0, k, 0, 0))
    in_specs = [pl.BlockSpec((rows, LANE), lambda b, k: (b, k)),
                pl.BlockSpec((rows, LANE), lambda b, k: (b, kb + k)),
                vec(LRU_CONV), vec(1), wspec, vec2, wspec, vec2, vec2]
    args = [proj, proj, conv_w, conv_b, w_r, b_r, w_i, b_i, lam]
    o_shape = jax.ShapeDtypeStruct((nb * n, kb * LANE), BF16)
    o_spec = pl.BlockSpec((rows, LANE), lambda b, k: (b, k))
    if init is not None:
        in_specs.append(pl.BlockSpec((bt, None, 2, 1, LANE), lambda b, k: (b, layer, 0, 0, k)))
        args.append(init)
        out_shape, out_specs = o_shape, o_spec
    else:
        out_shape = (o_shape, jax.ShapeDtypeStruct((nb, 2, 1, kb * LANE), F32))
        out_specs = (o_spec, pl.BlockSpec((bt, 2, 1, LANE), lambda b, k: (b, 0, 0, k)))
    return pl.pallas_call(
        functools.partial(_lru_kernel, has_init=init is not None, seq=n),
        grid=(nb // bt, kb),
        in_specs=in_specs,
        out_specs=out_specs,
        out_shape=out_shape,
        scratch_shapes=[pltpu.VMEM((rows + 2 * SUBLANE, LANE), F32),
                        pltpu.VMEM((2, rows, LANE), F32), pltpu.VMEM((2, rows, LANE), F32)]
                       + [pltpu.VMEM((2, 3, size // radix, LANE), F32) for size, radix in _scan_levels(n)[0]],
        compiler_params=_params(("arbitrary", "arbitrary"), 2 * 2 * rows * LANE * 4 + 12 * rows * LANE * 4),
    )(*args)


def kernel(x_prompt, x_sample, cache_attn_k, cache_attn_v, state_gla, state_lru, c, c_ctx, w_ada, b_ada, norm1_g, w_in, attn_sink, gla_w_decay, gla_b_decay, gla_norm_g, lru_conv_w, lru_conv_b, lru_w_rgate, lru_b_rgate, lru_w_igate, lru_b_igate, lru_lambda, w_out, norm2_g, w_gu, w_down, final_norm_g):
    nb_c, s_len, d = x_prompt.shape
    nb_l, n_lat, _ = x_sample.shape
    depth = w_in.shape[0]
    lru_w = lru_conv_w.shape[2]
    qw, kw = ATT_HEADS * HEAD_DIM, ATT_KV_HEADS * HEAD_DIM
    gdec0 = qw + 2 * kw + 4 * GLA_HEADS * GLA_DK
    gdec1 = gdec0 + 2 * GLA_LOWRANK
    in_tile = 512
    tail_w = 2 * lru_w + 2 * GLA_LOWRANK
    tail_pad = -(-tail_w // LANE) * LANE
    assert gdec0 % in_tile == 0 and w_in.shape[2] == gdec0 + tail_w

    w_in_b = w_in.astype(BF16)
    w_tail_b = jnp.concatenate(
        [w_in_b[:, :, gdec1:], w_in_b[:, :, gdec0:gdec1],
         jnp.zeros((depth, d, tail_pad - tail_w), BF16)], axis=2)
    dec_col = 2 * lru_w // LANE
    w_out_b = w_out.astype(BF16)
    w_gu_b = w_gu.astype(BF16)
    w_down_b = w_down.astype(BF16)

    cvecs = jnp.concatenate([c_ctx[None], c, jnp.zeros((MOD_ROWS - 1 - nb_l, d), F32)], axis=0)
    modr = _ada(cvecs, w_ada, b_ada).reshape(depth, MOD_ROWS, 6, 1, d)
    row_ctx = _ModRows(0, nb_c * s_len)
    row_lat = _ModRows(1, n_lat)

    tables = _rope_tables(n_lat)
    cache_k = cache_attn_k.reshape(nb_l, depth, cache_attn_k.shape[2], kw)
    cache_v = cache_attn_v.reshape(nb_l, depth, cache_attn_v.shape[2], kw)
    lru_init = state_lru.reshape(nb_l, depth, 2, 1, lru_w)

    xc = x_prompt.reshape(nb_c * s_len, d)
    xl = x_sample.reshape(nb_l * n_lat, d)
    new_k, new_v, new_gla, new_lru = [], [], [], []
    for l in range(depth):
        wdec = gla_w_decay[l].reshape(2, GLA_LOWRANK, GLA_HEADS, GLA_DK).transpose(0, 2, 1, 3)
        zpad = jnp.zeros((GLA_HEADS, GLA_LOWRANK, GLA_DK), F32)
        zrest = jnp.zeros((GLA_HEADS, LANE - 2 * GLA_LOWRANK, GLA_DK), F32)
        wf = jnp.concatenate([wdec[0], zpad, zrest], axis=1).astype(BF16)
        wb = jnp.concatenate([zpad, wdec[1], zrest], axis=1).astype(BF16)
        b_dec = gla_b_decay[l].reshape(2, GLA_HEADS, 1, GLA_DK)
        ng = gla_norm_g[l].reshape(GLA_HEADS, 1, LANE)
        lru_args = (lru_conv_w[l], lru_conv_b[l].reshape(1, lru_w),
                    lru_w_rgate[l].astype(BF16), lru_b_rgate[l].reshape(2, 1, lru_w),
                    lru_w_igate[l].astype(BF16), lru_b_igate[l].reshape(2, 1, lru_w),
                    lru_lambda[l].reshape(2, 1, lru_w))
        sink = attn_sink[l]

        def ffn(x, mix, row_of):
            x = _out_proj(*mix, w_out_b, x, modr, l, row_of, 2)
            h2 = _norm_mod(x, norm2_g[l], modr, l, row_of, 4, 3)
            act = _gate_up(h2, w_gu_b, l)
            return _down_proj(act, w_down_b, x, modr, l, row_of, 5)

        h = _norm_mod(xc, norm1_g[l], modr, l, row_ctx, 1, 0)
        proj = _in_proj(h, w_in_b, l, gdec0)
        tail = _in_proj(h, w_tail_b, l, tail_pad)
        attn = _attn_ctx(proj, sink, nb_c, s_len)
        gla, gla_fin = _gla(proj, tail, dec_col, wf, wb, b_dec, ng, nb_c, s_len)
        lru, lru_fin = _lru(tail, *lru_args, nb_c, s_len)
        new_k.append(proj[:, qw:qw + kw].reshape(nb_c, s_len, ATT_KV_HEADS, HEAD_DIM))
        new_v.append(proj[:, qw + kw:qw + 2 * kw].reshape(nb_c, s_len, ATT_KV_HEADS, HEAD_DIM))
        new_gla.append(gla_fin)
        new_lru.append(lru_fin.reshape(nb_c, 2, lru_w))
        xc = ffn(xc, (attn, gla, lru), row_ctx)

        h = _norm_mod(xl, norm1_g[l], modr, l, row_lat, 1, 0)
        proj = _in_proj(h, w_in_b, l, gdec0)
        tail = _in_proj(h, w_tail_b, l, tail_pad)
        attn = _attn_lat(proj, sink, cache_k, cache_v, l, tables, nb_l, n_lat)
        gla = _gla(proj, tail, dec_col, wf, wb, b_dec, ng, nb_l, n_lat, init=state_gla, layer=l)
        lru = _lru(tail, *lru_args, nb_l, n_lat, init=lru_init, layer=l)
        xl = ffn(xl, (attn, gla, lru), row_lat)

    y_prompt = _final_norm(xc, final_norm_g).reshape(nb_c, s_len, d)
    y_sample = _final_norm(xl, final_norm_g).reshape(nb_l, n_lat, d)
    return (y_prompt, y_sample, jnp.stack(new_k, axis=1), jnp.stack(new_v, axis=1),
            jnp.stack(new_gla, axis=1), jnp.stack(new_lru, axis=1))
```

```python
import functools
from typing import NamedTuple

import jax
import jax.numpy as jnp
from jax import lax
from jax.experimental import pallas as pl
from jax.experimental.pallas import tpu as pltpu

F32 = jnp.float32
BF16 = jnp.bfloat16

EPS = 1e-6
HEAD_DIM = 128
ATT_HEADS = 16
ATT_KV_HEADS = 4
ATT_GROUPS = ATT_HEADS // ATT_KV_HEADS
WINDOW = 128
Q_BLOCK = 128
ATT_TILE_ROWS = 128
GRID_W = 64
ROPE_THETA = 10000.0
ROPE_FREQS = HEAD_DIM // 4
GLA_HEADS = 8
GLA_DK = 128
GLA_LOWRANK = 16
GLA_TAU = 16.0
GLA_CHUNK = 64
GLA_BULK_ROWS = 256
GLA_ELEMENTWISE_ROWS = 512
LRU_BLOCK = 128
LRU_CONV = 4
CONV_LEFT = 2
LRU_C = 8.0
MIXER_ROWS_PER_STEP = 1024
SCAN_TOP_ROWS = 8
SCAN_SLAB_ROWS = 64

LANE = 128
SUBLANE = 8
VMEM_BYTES_V7X = 64 * 2**20
MXU_COLS = 256
GATE_UP_SLAB_ROWS = 1024
MOD_ROWS = 16

LOG2_E = 1.4426950408889634
NT_DIMS = (((1,), (1,)), ((), ()))
TN_DIMS = (((0,), (0,)), ((), ()))


def _params(sem, est_bytes):
    limit = int(min(VMEM_BYTES_V7X - 4 * 2**20, max(est_bytes + 8 * 2**20, 32 * 2**20)))
    return pltpu.CompilerParams(dimension_semantics=sem, vmem_limit_bytes=limit)


def _seqs_per_step(nb, n):
    bt = max(1, MIXER_ROWS_PER_STEP // n)
    while nb % bt:
        bt -= 1
    return bt


def _softplus(x):
    return jnp.maximum(x, 0.0) + jnp.log1p(jnp.exp(-jnp.abs(x)))


def _log_sigmoid(x):
    return -_softplus(-x)


def _sigmoid(x):
    return 0.5 * jnp.tanh(0.5 * x) + 0.5


def _ada_kernel(c_ref, w_ref, b_ref, o_ref):
    cv = c_ref[...]
    s = (cv * jax.nn.sigmoid(cv)).astype(BF16)
    o_ref[...] = jnp.dot(s, w_ref[...].astype(BF16), preferred_element_type=F32) + b_ref[...]


def _ada(cvecs, w_ada, b_ada):
    nl, d, n6 = w_ada.shape
    tn = 512
    return pl.pallas_call(
        _ada_kernel,
        grid=(nl, n6 // tn),
        in_specs=[pl.BlockSpec((MOD_ROWS, d), lambda l, j: (0, 0)),
                  pl.BlockSpec((None, d, tn), lambda l, j: (l, 0, j)),
                  pl.BlockSpec((None, 1, tn), lambda l, j: (l, 0, j))],
        out_specs=pl.BlockSpec((None, MOD_ROWS, tn), lambda l, j: (l, 0, j)),
        out_shape=jax.ShapeDtypeStruct((nl, MOD_ROWS, n6), F32),
        compiler_params=_params(("arbitrary", "arbitrary"), 2 * d * tn * 4 + d * tn * 2),
    )(cvecs, w_ada, b_ada.reshape(nl, 1, n6))


def _norm_mod_kernel(x_ref, g_ref, sc_ref, sh_ref, o_ref):
    x = x_ref[...]
    y = x * lax.rsqrt(jnp.mean(x * x, axis=-1, keepdims=True) + EPS) * g_ref[...]
    o_ref[...] = (y * (1.0 + sc_ref[...]) + sh_ref[...]).astype(o_ref.dtype)


def _norm_kernel(x_ref, g_ref, o_ref):
    x = x_ref[...]
    y = x * lax.rsqrt(jnp.mean(x * x, axis=-1, keepdims=True) + EPS) * g_ref[...]
    o_ref[...] = y.astype(o_ref.dtype)


class _ModRows(NamedTuple):
    first: int
    segment: int

    def tile(self, m, want):
        tm = min(want, m, self.segment)
        assert m % tm == 0 and self.segment % tm == 0
        return tm

    def __call__(self, row0):
        return self.first + row0 // self.segment


def _norm_mod(x, g, modr, layer, row_of, sc_idx, sh_idx, tr=512):
    m, d = x.shape
    tr = row_of.tile(m, tr)
    mspec = lambda k: pl.BlockSpec((None, None, None, 1, d), lambda i: (layer, row_of(i * tr), k, 0, 0))
    return pl.pallas_call(
        _norm_mod_kernel,
        grid=(m // tr,),
        in_specs=[pl.BlockSpec((tr, d), lambda i: (i, 0)),
                  pl.BlockSpec((1, d), lambda i: (0, 0)),
                  mspec(sc_idx), mspec(sh_idx)],
        out_specs=pl.BlockSpec((tr, d), lambda i: (i, 0)),
        out_shape=jax.ShapeDtypeStruct((m, d), BF16),
        compiler_params=_params(("arbitrary",), 2 * tr * d * 6 + 2 * tr * d * 4),
    )(x, g.reshape(1, d), modr, modr)


def _final_norm(x, g, tr=512):
    m, d = x.shape
    tr = min(tr, m)
    return pl.pallas_call(
        _norm_kernel,
        grid=(m // tr,),
        in_specs=[pl.BlockSpec((tr, d), lambda i: (i, 0)),
                  pl.BlockSpec((1, d), lambda i: (0, 0))],
        out_specs=pl.BlockSpec((tr, d), lambda i: (i, 0)),
        out_shape=jax.ShapeDtypeStruct((m, d), F32),
        compiler_params=_params(("arbitrary",), 2 * tr * d * 8 + 2 * tr * d * 4),
    )(x, g.reshape(1, d))


def _mm_kernel(x_ref, w_ref, o_ref):
    for j0 in range(0, o_ref.shape[1], MXU_COLS):
        cols = slice(j0, min(j0 + MXU_COLS, o_ref.shape[1]))
        o_ref[:, cols] = jnp.dot(x_ref[...], w_ref[:, cols], preferred_element_type=F32)


def _in_proj(h, w, layer, n, tm=2048, tn=512):
    m, k = h.shape
    assert n <= w.shape[2] and n % LANE == 0
    if n % tn:
        assert n == w.shape[2]
        tm, tn = 512, n
    tm = min(tm, m)
    return pl.pallas_call(
        _mm_kernel,
        grid=(m // tm, n // tn),
        in_specs=[pl.BlockSpec((tm, k), lambda i, j: (i, 0)),
                  pl.BlockSpec((None, k, tn), lambda i, j: (layer, 0, j))],
        out_specs=pl.BlockSpec((tm, tn), lambda i, j: (i, j)),
        out_shape=jax.ShapeDtypeStruct((m, n), F32),
        compiler_params=_params(("arbitrary", "arbitrary"), 2 * (tm * k * 2 + k * tn * 2 + tm * tn * 4)),
    )(h, w)


def _out_proj_kernel(a_ref, b_ref, c_ref, w_ref, x_ref, g_ref, o_ref):
    ka, kb = a_ref.shape[1], b_ref.shape[1]
    for j0 in range(0, o_ref.shape[1], MXU_COLS):
        cols = slice(j0, j0 + MXU_COLS)
        acc = jnp.dot(a_ref[...], w_ref[0:ka, cols], preferred_element_type=F32)
        acc += jnp.dot(b_ref[...], w_ref[ka:ka + kb, cols], preferred_element_type=F32)
        acc += jnp.dot(c_ref[...], w_ref[ka + kb:, cols], preferred_element_type=F32)
        o_ref[:, cols] = x_ref[:, cols] + g_ref[:, cols] * acc


def _out_proj(attn, gla, lru, w, x, modr, layer, row_of, gate_idx, tm=1024, tn=1024):
    m, d = x.shape
    k = w.shape[1]
    tm = row_of.tile(m, tm)
    xs = lambda a: pl.BlockSpec((tm, a.shape[1]), lambda i, j: (i, 0))
    return pl.pallas_call(
        _out_proj_kernel,
        grid=(m // tm, d // tn),
        in_specs=[xs(attn), xs(gla), xs(lru),
                  pl.BlockSpec((None, k, tn), lambda i, j: (layer, 0, j)),
                  pl.BlockSpec((tm, tn), lambda i, j: (i, j)),
                  pl.BlockSpec((None, None, None, 1, tn),
                               lambda i, j: (layer, row_of(i * tm), gate_idx, 0, j))],
        out_specs=pl.BlockSpec((tm, tn), lambda i, j: (i, j)),
        out_shape=jax.ShapeDtypeStruct((m, d), F32),
        compiler_params=_params(("arbitrary", "arbitrary"),
                                2 * (tm * k * 2 + k * tn * 2 + 2 * tm * tn * 4) + tm * tn * 4),
    )(attn, gla, lru, w, x, modr)


def _gate_up_kernel(h_ref, wg_ref, wu_ref, o_ref):
    tm = h_ref.shape[0]
    slab = min(tm, GATE_UP_SLAB_ROWS)
    for r0 in range(0, tm, slab):
        h = h_ref[r0:r0 + slab, :]
        a = jnp.dot(h, wg_ref[...], preferred_element_type=F32)
        u = jnp.dot(h, wu_ref[...], preferred_element_type=F32)
        o_ref[r0:r0 + slab, :] = (a * _sigmoid(a) * u).astype(o_ref.dtype)


def _gate_up(h, w, layer, tm=2048, tn=256):
    m, k = h.shape
    f = w.shape[2] // 2
    tm = min(tm, m)
    nj = f // tn
    return pl.pallas_call(
        _gate_up_kernel,
        grid=(m // tm, nj),
        in_specs=[pl.BlockSpec((tm, k), lambda i, j: (i, 0)),
                  pl.BlockSpec((None, k, tn), lambda i, j: (layer, 0, j)),
                  pl.BlockSpec((None, k, tn), lambda i, j: (layer, 0, nj + j))],
        out_specs=pl.BlockSpec((tm, tn), lambda i, j: (i, j)),
        out_shape=jax.ShapeDtypeStruct((m, f), BF16),
        compiler_params=_params(("arbitrary", "arbitrary"),
                                2 * (tm * k * 2 + 2 * k * tn * 2 + tm * tn * 2) + 3 * tm * tn * 4),
    )(h, w, w)


def _down_kernel(a_ref, w_ref, x_ref, g_ref, o_ref):
    for j0 in range(0, o_ref.shape[1], MXU_COLS):
        cols = slice(j0, j0 + MXU_COLS)
        acc = jnp.dot(a_ref[...], w_ref[:, cols], preferred_element_type=F32)
        o_ref[:, cols] = x_ref[:, cols] + g_ref[:, cols] * acc


def _down_proj(act, w, x, modr, layer, row_of, gate_idx, tm=512, tn=512):
    m, d = x.shape
    k = w.shape[1]
    tm = row_of.tile(m, tm)
    return pl.pallas_call(
        _down_kernel,
        grid=(m // tm, d // tn),
        in_specs=[pl.BlockSpec((tm, k), lambda i, j: (i, 0)),
                  pl.BlockSpec((None, k, tn), lambda i, j: (layer, 0, j)),
                  pl.BlockSpec((tm, tn), lambda i, j: (i, j)),
                  pl.BlockSpec((None, None, None, 1, tn),
                               lambda i, j: (layer, row_of(i * tm), gate_idx, 0, j))],
        out_specs=pl.BlockSpec((tm, tn), lambda i, j: (i, j)),
        out_shape=jax.ShapeDtypeStruct((m, d), F32),
        compiler_params=_params(("arbitrary", "arbitrary"),
                                2 * (tm * k * 2 + k * tn * 2 + 2 * tm * tn * 4) + tm * tn * 4),
    )(act, w, x, modr)


def _softmax_pv(s_parts, sk, scale, value_of):
    m = functools.reduce(jnp.maximum, s_parts)
    m2 = jnp.maximum(jnp.max(m, axis=-1, keepdims=True) * scale, sk) * LOG2_E
    p_parts = [jnp.exp2(s * (scale * LOG2_E) - m2) for s in s_parts]
    denom = (jnp.sum(functools.reduce(jnp.add, p_parts), axis=-1, keepdims=True)
             + jnp.exp2(sk * LOG2_E - m2))
    o = None
    for j, p in enumerate(p_parts):
        ov = jnp.dot(p.astype(BF16), value_of(j), preferred_element_type=F32)
        o = ov if o is None else o + ov
    return o, 1.0 / denom


def _attn_ctx_kernel(sink_ref, q_ref, k_ref, v_ref, o_ref):
    scale = HEAD_DIM ** -0.5
    head = lambda ref, h: ref[:, h * HEAD_DIM:(h + 1) * HEAD_DIM]
    kt = [head(k_ref, kh).T.astype(BF16) for kh in range(ATT_KV_HEADS)]
    vv = [head(v_ref, kh).astype(BF16) for kh in range(ATT_KV_HEADS)]

    def scores(h):
        s = jnp.dot(head(q_ref, h).astype(BF16), kt[h // ATT_GROUPS], preferred_element_type=F32)
        return [s[:, j:j + LANE] for j in range(0, s.shape[1], LANE)]

    s_next = scores(0)
    for h in range(ATT_HEADS):
        s_parts = s_next
        if h + 1 < ATT_HEADS:
            s_next = scores(h + 1)
        v = vv[h // ATT_GROUPS]
        o, r = _softmax_pv(s_parts, sink_ref[h], scale, lambda j: v[j * LANE:(j + 1) * LANE])
        o_ref[:, h * HEAD_DIM:(h + 1) * HEAD_DIM] = (o * r).astype(o_ref.dtype)


def _attn_ctx(proj, sink, nb, s_len):
    qw = ATT_HEADS * HEAD_DIM
    kw = ATT_KV_HEADS * HEAD_DIM
    return pl.pallas_call(
        _attn_ctx_kernel,
        grid=(nb,),
        in_specs=[pl.BlockSpec(memory_space=pltpu.SMEM),
                  pl.BlockSpec((s_len, qw), lambda b: (b, 0)),
                  pl.BlockSpec((s_len, kw), lambda b: (b, qw // kw)),
                  pl.BlockSpec((s_len, kw), lambda b: (b, qw // kw + 1))],
        out_specs=pl.BlockSpec((s_len, qw), lambda b: (b, 0)),
        out_shape=jax.ShapeDtypeStruct((nb * s_len, qw), BF16),
        compiler_params=_params(("arbitrary",), 2 * s_len * (qw + 2 * kw) * 4 + 2 * s_len * qw * 2
                                + 8 * ATT_GROUPS * s_len * s_len * 4),
    )(sink, proj, proj, proj)


def _attn_lat_kernel(sink_ref, q_ref, k_ref, v_ref, kc_ref, vc_ref, cos_ref, sa_ref, sb_ref,
                     o_ref, kt_s, vb_s, qr_s):
    n = q_ref.shape[0]
    nqb = n // Q_BLOCK
    kh = pl.program_id(1)
    scale = HEAD_DIM ** -0.5
    wblocks = (Q_BLOCK + 2 * WINDOW) // Q_BLOCK

    def rope(x, rows):
        return (x * cos_ref[rows, :] + pltpu.roll(x, HEAD_DIM - ROPE_FREQS, 1) * sa_ref[rows, :]
                + pltpu.roll(x, ROPE_FREQS, 1) * sb_ref[rows, :])

    def prep(j, carry):
        rows = pl.ds(pl.multiple_of(j * Q_BLOCK, Q_BLOCK), Q_BLOCK)
        kt_s[j] = rope(k_ref[rows, :], rows).T.astype(BF16)
        for g in range(ATT_GROUPS):
            lanes = slice(g * HEAD_DIM, (g + 1) * HEAD_DIM)
            qr_s[rows, lanes] = rope(q_ref[rows, lanes], rows).astype(BF16)
        return carry

    lax.fori_loop(0, nqb, prep, 0)
    vb_s[...] = v_ref[...].astype(BF16)
    kct = kc_ref[...].T.astype(BF16)
    vc = vc_ref[...].astype(BF16)
    qi = lax.broadcasted_iota(jnp.int32, (Q_BLOCK, Q_BLOCK), 0)
    ki = lax.broadcasted_iota(jnp.int32, (Q_BLOCK, Q_BLOCK), 1)
    rel = qi - ki

    tr = ATT_TILE_ROWS
    per_step = max(s for s in (4, 2, 1) if nqb % s == 0)

    def blocks(ii, carry):
        tiles = []
        for bb in range(per_step):
            i = ii * per_step + bb
            qs = pl.multiple_of(i * Q_BLOCK, Q_BLOCK)
            kb = jnp.clip(i - WINDOW // Q_BLOCK, 0, nqb - wblocks)
            biases = []
            for j in range(wblocks):
                off = (i - kb - j) * Q_BLOCK
                biases.append(jnp.where(jnp.abs(rel + off) <= WINDOW, 0.0, -jnp.inf).astype(F32))
            tiles += [(qs, kb, biases, g, r0) for g in range(ATT_GROUPS) for r0 in range(0, Q_BLOCK, tr)]

        def scores(qs, kb, biases, g, r0):
            q = qr_s[pl.ds(qs + r0, tr), g * HEAD_DIM:(g + 1) * HEAD_DIM]
            parts = [jnp.dot(q, kt_s[kb + j], preferred_element_type=F32) + biases[j][r0:r0 + tr]
                     for j in range(wblocks)]
            s2 = jnp.dot(q, kct, preferred_element_type=F32)
            return parts + [s2[:, j * LANE:(j + 1) * LANE] for j in range(s2.shape[1] // LANE)]

        s_next = scores(*tiles[0])
        for t, (qs, kb, _, g, r0) in enumerate(tiles):
            s_parts = s_next
            if t + 1 < len(tiles):
                s_next = scores(*tiles[t + 1])
            o, r = _softmax_pv(s_parts, sink_ref[kh * ATT_GROUPS + g], scale, lambda j: (
                vb_s[pl.ds(pl.multiple_of((kb + j) * Q_BLOCK, Q_BLOCK), Q_BLOCK), :] if j < wblocks
                else vc[(j - wblocks) * LANE:(j - wblocks + 1) * LANE, :]))
            o_ref[pl.ds(qs + r0, tr), g * HEAD_DIM:(g + 1) * HEAD_DIM] = (o * r).astype(o_ref.dtype)
        return carry

    lax.fori_loop(0, nqb // per_step, blocks, 0)


def _attn_lat(proj, sink, cache_k, cache_v, layer, tables, nb, n):
    qw = ATT_HEADS * HEAD_DIM
    gw = ATT_GROUPS * HEAD_DIM
    past = cache_k.shape[2]
    kcol = qw // HEAD_DIM
    vcol = kcol + ATT_KV_HEADS
    cspec = pl.BlockSpec((None, None, past, HEAD_DIM), lambda b, h: (b, layer, 0, h))
    tspec = pl.BlockSpec((n, HEAD_DIM), lambda b, h: (0, 0))
    return pl.pallas_call(
        _attn_lat_kernel,
        grid=(nb, ATT_KV_HEADS),
        in_specs=[pl.BlockSpec(memory_space=pltpu.SMEM),
                  pl.BlockSpec((n, gw), lambda b, h: (b, h)),
                  pl.BlockSpec((n, HEAD_DIM), lambda b, h: (b, kcol + h)),
                  pl.BlockSpec((n, HEAD_DIM), lambda b, h: (b, vcol + h)),
                  cspec, cspec, tspec, tspec, tspec],
        out_specs=pl.BlockSpec((n, gw), lambda b, h: (b, h)),
        out_shape=jax.ShapeDtypeStruct((nb * n, qw), BF16),
        scratch_shapes=[pltpu.VMEM((n // Q_BLOCK, HEAD_DIM, Q_BLOCK), BF16),
                        pltpu.VMEM((n, HEAD_DIM), BF16), pltpu.VMEM((n, gw), BF16)],
        compiler_params=_params(("arbitrary", "arbitrary"),
                                2 * n * (gw + 2 * HEAD_DIM) * 4 + 2 * n * gw * 2 + 6 * n * HEAD_DIM * 4
                                + 4 * n * HEAD_DIM + n * gw * 2),
    )(sink, proj, proj, proj, cache_k, cache_v, *tables)


def _rope_tables(n):
    pos = jnp.arange(n)
    row = (pos // GRID_W).astype(F32)
    col = (pos % GRID_W).astype(F32)
    inv = ROPE_THETA ** (-jnp.arange(ROPE_FREQS, dtype=F32) / ROPE_FREQS)
    ang_r = row[:, None] * inv[None, :]
    ang_c = col[:, None] * inv[None, :]
    cr, sr, cc, sc = jnp.cos(ang_r), jnp.sin(ang_r), jnp.cos(ang_c), jnp.sin(ang_c)
    z = jnp.zeros_like(cr)
    cos = jnp.concatenate([cr, cr, cc, cc], axis=1)
    sa = jnp.concatenate([-sr, z, -sc, z], axis=1)
    sb = jnp.concatenate([z, sr, z, sc], axis=1)
    return cos, sa, sb


def _gla_kernel(*refs, has_init, seq):
    if has_init:
        (q_ref, k_ref, v_ref, g_ref, dec_ref, wf_ref, wb_ref, bd_ref, ng_ref, s0_ref,
         o_ref, qin_s, tot_s, o_s, kv_s, sin_s) = refs
        sfin_ref = None
    else:
        (q_ref, k_ref, v_ref, g_ref, dec_ref, wf_ref, wb_ref, bd_ref, ng_ref,
         o_ref, sfin_ref, qin_s, tot_s, o_s, kv_s, sin_s) = refs
    n = q_ref.shape[0]
    c = GLA_CHUNK
    nc = n // c
    blk = min(seq, GLA_BULK_ROWS)
    qscale = GLA_DK ** -0.5
    shift = c.bit_length() - 1
    ri = lax.broadcasted_iota(jnp.int32, (blk, blk), 0)
    ci = lax.broadcasted_iota(jnp.int32, (blk, blk), 1)
    same = lax.shift_right_logical(ri, shift) == lax.shift_right_logical(ci, shift)
    keep = (same & (ri >= ci), same & (ri <= ci))
    sums = tuple(jnp.concatenate([keep[d].astype(BF16), same.astype(BF16)], axis=0) for d in range(2))
    wdec = jnp.concatenate([wf_ref[...], wb_ref[...]], axis=1)
    bdec = jnp.concatenate([bd_ref[0], bd_ref[1]], axis=1)

    def split3(x):
        hi = x.astype(BF16)
        r1 = x - hi.astype(F32)
        mid = r1.astype(BF16)
        return jnp.concatenate([hi, mid, (r1 - mid.astype(F32)).astype(BF16)], axis=1)

    def fold3(y):
        y = (y[:, :LANE] + y[:, LANE:2 * LANE]) + y[:, 2 * LANE:]
        return y[:blk], y[blk:]

    def chain(rows, chunk0, d, z, qc, kc, vc):
        st = {}

        def log_decay():
            st["p"] = split3(_log_sigmoid(z[:, d * GLA_DK:(d + 1) * GLA_DK]) * (1.0 / GLA_TAU))

        def chunk_sums():
            st["y"] = jnp.dot(sums[d], st["p"], preferred_element_type=F32)

        def decayed():
            cum, tot = fold3(st["y"])
            tot_s[d, rows, :] = tot
            st["q"] = (qc * jnp.exp(cum)).astype(BF16)
            st["k"] = (kc * jnp.exp(-cum)).astype(BF16)
            qin_s[d, rows, :] = st["q"]
            st["kd"] = (kc * jnp.exp(tot - cum)).astype(BF16)

        def scores():
            st["a"] = lax.dot_general(st["q"], st["k"], NT_DIMS, preferred_element_type=F32)

        def mask():
            st["att"] = jnp.where(keep[d], st["a"], 0.0).astype(BF16)

        def within():
            o_s[d, rows, :] = jnp.dot(st["att"], vc, preferred_element_type=F32)
            for u in range(blk // c):
                cr = slice(u * c, (u + 1) * c)
                kv_s[d, chunk0 + u] = lax.dot_general(vc[cr], st["kd"][cr], TN_DIMS,
                                                      preferred_element_type=F32)

        return [log_decay, chunk_sums, decayed, scores, mask, within]

    per_step = 2 if (n // blk) % 2 == 0 else 1

    def bulk(i, carry):
        chains = []
        for b in range(per_step):
            block = i * per_step + b
            rows = pl.ds(pl.multiple_of(block * blk, blk), blk)
            z = jnp.dot(dec_ref[rows, :].astype(BF16), wdec, preferred_element_type=F32) + bdec
            vc = v_ref[rows, :].astype(BF16)
            chains += [chain(rows, block * (blk // c), d, z, q_ref[rows, :] * qscale, k_ref[rows, :], vc)
                       for d in range(2)]
        nstage = len(chains[0])
        for t in range(nstage + len(chains) - 1):
            for lag, stages in enumerate(chains):
                if 0 <= t - lag < nstage:
                    stages[t - lag]()
        return carry

    lax.fori_loop(0, n // (blk * per_step), bulk, 0)

    ncs = seq // c
    for s in range(n // seq):
        def scan(j, carry, base=s * ncs):
            sf, sb = carry
            jf = base + j
            jb = base + ncs - 1 - j
            sin_s[0, jf] = sf.astype(BF16)
            sin_s[1, jb] = sb.astype(BF16)
            sf = jnp.exp(tot_s[0, pl.ds(jf * c, 1), :]) * sf + kv_s[0, jf]
            sb = jnp.exp(tot_s[1, pl.ds(jb * c, 1), :]) * sb + kv_s[1, jb]
            return sf, sb

        if has_init:
            init = (s0_ref[s, 0].T, s0_ref[s, 1].T)
        else:
            init = (jnp.zeros((LANE, GLA_DK), F32), jnp.zeros((LANE, GLA_DK), F32))
        sf, sb = lax.fori_loop(0, ncs, scan, init)
        if sfin_ref is not None:
            sfin_ref[s, 0] = sf.T
            sfin_ref[s, 1] = sb.T

    group = 4 if nc % 4 == 0 else 1

    def carried(g):
        terms = []
        for j in range(g * group, (g + 1) * group):
            rows = slice(j * c, (j + 1) * c)
            terms.append((rows, [(o_s[d, rows, :], lax.dot_general(qin_s[d, rows, :], sin_s[d, j], NT_DIMS,
                                                                   preferred_element_type=F32))
                                 for d in range(2)]))
        return terms

    nxt = carried(0)
    for g in range(nc // group):
        cur = nxt
        if g + 1 < nc // group:
            nxt = carried(g + 1)
        for rows, ((of, cf), (ob, cb)) in cur:
            o_s[0, rows, :] = (of + cf) + (ob + cb)

    ew = min(n, GLA_ELEMENTWISE_ROWS)

    def finish(i, carry):
        rows = pl.ds(pl.multiple_of(i * ew, ew), ew)
        o = o_s[0, rows, :]
        o = o * lax.rsqrt(jnp.mean(o * o, axis=-1, keepdims=True) + EPS) * ng_ref[...]
        g = g_ref[rows, :]
        o_ref[rows, :] = (o * (g * _sigmoid(g))).astype(o_ref.dtype)
        return carry

    lax.fori_loop(0, n // ew, finish, 0)


def _gla(proj, proj_tail, dec_col, wf, wb, b_dec, norm_g, nb, n, init=None, layer=0):
    hh = GLA_HEADS
    q0 = (ATT_HEADS + 2 * ATT_KV_HEADS) * HEAD_DIM // LANE
    bt = _seqs_per_step(nb, n)
    rows = bt * n
    nc = rows // GLA_CHUNK
    col = lambda base: pl.BlockSpec((rows, LANE), lambda b, h: (b, base + h))
    in_specs = [col(q0), col(q0 + hh), col(q0 + 2 * hh), col(q0 + 3 * hh),
                pl.BlockSpec((rows, LANE), lambda b, h: (b, dec_col)),
                pl.BlockSpec((None, LANE, LANE), lambda b, h: (h, 0, 0)),
                pl.BlockSpec((None, LANE, LANE), lambda b, h: (h, 0, 0)),
                pl.BlockSpec((2, None, 1, LANE), lambda b, h: (0, h, 0, 0)),
                pl.BlockSpec((None, 1, LANE), lambda b, h: (h, 0, 0))]
    args = [proj, proj, proj, proj, proj_tail, wf, wb, b_dec, norm_g]
    o_shape = jax.ShapeDtypeStruct((nb * n, hh * LANE), BF16)
    o_spec = pl.BlockSpec((rows, LANE), lambda b, h: (b, h))
    if init is not None:
        in_specs.append(pl.BlockSpec((bt, None, 2, None, GLA_DK, LANE),
                                     lambda b, h: (b, layer, 0, h, 0, 0)))
        args.append(init)
        out_shape, out_specs = o_shape, o_spec
    else:
        out_shape = (o_shape, jax.ShapeDtypeStruct((nb, 2, hh, GLA_DK, LANE), F32))
        out_specs = (o_spec, pl.BlockSpec((bt, 2, None, GLA_DK, LANE), lambda b, h: (b, 0, h, 0, 0)))
    return pl.pallas_call(
        functools.partial(_gla_kernel, has_init=init is not None, seq=n),
        grid=(nb // bt, hh),
        in_specs=in_specs,
        out_specs=out_specs,
        out_shape=out_shape,
        scratch_shapes=[pltpu.VMEM((2, rows, GLA_DK), BF16),
                        pltpu.VMEM((2, rows, GLA_DK), F32),
                        pltpu.VMEM((2, rows, LANE), F32),
                        pltpu.VMEM((2, nc, LANE, GLA_DK), F32),
                        pltpu.VMEM((2, nc, LANE, GLA_DK), BF16)],
        compiler_params=_params(("arbitrary", "arbitrary"),
                                2 * 5 * rows * LANE * 4 + 2 * rows * LANE * 2 + rows * LANE * (2 + 8 + 16)
                                + nc * LANE * GLA_DK * 12 + 4 * 2**20),
    )(*args)


def _scan_levels(n):
    levels = []
    size = n
    while size > SCAN_TOP_ROWS:
        radix = 4 if not levels else 8
        levels.append((size, radix))
        size //= radix
    return levels, size


def _linear_scan(a, u, lv, n, h0, rev):
    levels, top = _scan_levels(n)

    def sweep(a_l, u_l, size, radix, body):
        m = size // radix
        sb = min(m, SCAN_SLAB_ROWS)
        first = radix - 1 if rev else 0
        order = list(range(radix - 2, -1, -1)) if rev else list(range(1, radix))

        def blk(j, carry):
            sl = lambda r: pl.ds(j * (sb * radix) + r, sb, stride=radix)
            body(sl, pl.ds(pl.multiple_of(j * sb, sb), sb), first, order)
            return carry

        if m // sb == 1:
            blk(0, 0)
        else:
            lax.fori_loop(0, m // sb, blk, 0)

    src_a, src_u = a, u
    for l, (size, radix) in enumerate(levels):
        nxt = lv[l]

        def up(sl, dense, first, order, src_a=src_a, src_u=src_u, nxt=nxt):
            pa = src_a[sl(first), :]
            pu = src_u[sl(first), :]
            for r in order:
                ar = src_a[sl(r), :]
                pu = ar * pu + src_u[sl(r), :]
                pa = ar * pa
                src_a[sl(r), :] = pa
                src_u[sl(r), :] = pu
            nxt[0, dense, :] = pa
            nxt[1, dense, :] = pu

        sweep(src_a, src_u, size, radix, up)
        src_a, src_u = nxt.at[0], nxt.at[1]

    carry = h0
    top_c = lv[len(levels) - 1].at[2]
    for g in (range(top - 1, -1, -1) if rev else range(top)):
        top_c[g:g + 1, :] = carry
        carry = src_a[g:g + 1, :] * carry + src_u[g:g + 1, :]

    for l in range(len(levels) - 1, -1, -1):
        size, radix = levels[l]
        a_l, u_l = (a, u) if l == 0 else (lv[l - 1].at[0], lv[l - 1].at[1])
        c_l = lv[l].at[2]
        c_below = None if l == 0 else lv[l - 1].at[2]

        def down(sl, dense, first, order, a_l=a_l, u_l=u_l, c_l=c_l, c_below=c_below):
            cin = c_l[dense, :]
            prev = cin
            for r in [first] + order:
                h = u_l[sl(r), :] + a_l[sl(r), :] * cin
                if c_below is None:
                    u_l[sl(r), :] = h
                else:
                    c_below[sl(r), :] = prev
                prev = h

        sweep(a_l, u_l, size, radix, down)
    return carry


def _lru_kernel(*refs, has_init, seq):
    if has_init:
        (x_ref, y_ref, cw_ref, cb_ref, wr_ref, br_ref, wi_ref, bi_ref, lam_ref, h0_ref,
         o_ref, xp_s, a_s, u_s, *lv_s) = refs
        hfin_ref = None
    else:
        (x_ref, y_ref, cw_ref, cb_ref, wr_ref, br_ref, wi_ref, bi_ref, lam_ref,
         o_ref, hfin_ref, xp_s, a_s, u_s, *lv_s) = refs
    n = x_ref.shape[0]
    nseq = n // seq
    pad = SUBLANE
    xp_s[0:pad, :] = jnp.zeros((pad, LANE), F32)
    xp_s[pad + n:pad + n + pad, :] = jnp.zeros((pad, LANE), F32)
    xp_s[pad:pad + n, :] = x_ref[...]
    t = lax.broadcasted_iota(jnp.int32, (n, 1), 0)
    xc = cb_ref[...]
    for j in range(LRU_CONV):
        off = j - CONV_LEFT
        tap = xp_s[pl.ds(pad + off, n), :]
        edge = [s * seq + e for s in range(1, nseq) for e in range(0, -off)] if off < 0 else \
               [s * seq - 1 - e for s in range(1, nseq) for e in range(0, off)]
        if edge:
            tap = jnp.where(functools.reduce(jnp.logical_or, [t == e for e in edge]), 0.0, tap)
        xc = xc + tap * cw_ref[j:j + 1, :]
    xcb = xc.astype(BF16)
    for d in range(2):
        r = _sigmoid(jnp.dot(xcb, wr_ref[d], preferred_element_type=F32) + br_ref[d])
        i = _sigmoid(jnp.dot(xcb, wi_ref[d], preferred_element_type=F32) + bi_ref[d])
        log_a = -LRU_C * r * _softplus(-lam_ref[d])
        a = jnp.exp(log_a)
        a_s[d] = a
        v = -jnp.tanh(log_a) * (a * a + 1.0)
        u_s[d] = jnp.where(v > 0.0, v * lax.rsqrt(v), 0.0) * (i * xc)

    for s in range(nseq):
        rows = pl.ds(s * seq, seq)
        for d in range(2):
            h0 = h0_ref[s, d] if has_init else jnp.zeros((1, LANE), F32)
            h_last = _linear_scan(a_s.at[d, rows], u_s.at[d, rows], [lv.at[d] for lv in lv_s], seq, h0,
                                  rev=d == 1)
            if hfin_ref is not None:
                hfin_ref[s, d] = h_last
    o_ref[...] = ((u_s[0] + u_s[1]) * jax.nn.gelu(y_ref[...])).astype(o_ref.dtype)


def _lru(proj, conv_w, conv_b, w_r, b_r, w_i, b_i, lam, nb, n, init=None, layer=0):
    kb = conv_w.shape[1] // LRU_BLOCK
    bt = _seqs_per_step(nb, n)
    rows = bt * n
    vec = lambda r: pl.BlockSpec((r, LANE), lambda b, k: (0, k))
    vec2 = pl.BlockSpec((2, 1, LANE), lambda b, k: (0, 0, k))
    wspec = pl.BlockSpec((2, None, LRU_BLOCK, LRU_BLOCK), lambda b, k: (0, k, 0, 0))
    in_specs = [pl.BlockSpec((rows, LANE), lambda b, k: (b, k)),
                pl.BlockSpec((rows, LANE), lambda b, k: (b, kb + k)),
                vec(LRU_CONV), vec(1), wspec, vec2, wspec, vec2, vec2]
    args = [proj, proj, conv_w, conv_b, w_r, b_r, w_i, b_i, lam]
    o_shape = jax.ShapeDtypeStruct((nb * n, kb * LANE), BF16)
    o_spec = pl.BlockSpec((rows, LANE), lambda b, k: (b, k))
    if init is not None:
        in_specs.append(pl.BlockSpec((bt, None, 2, 1, LANE), lambda b, k: (b, layer, 0, 0, k)))
        args.append(init)
        out_shape, out_specs = o_shape, o_spec
    else:
        out_shape = (o_shape, jax.ShapeDtypeStruct((nb, 2, 1, kb * LANE), F32))
        out_specs = (o_spec, pl.BlockSpec((bt, 2, 1, LANE), lambda b, k: (b, 0, 0, k)))
    return pl.pallas_call(
        functools.partial(_lru_kernel, has_init=init is not None, seq=n),
        grid=(nb // bt, kb),
        in_specs=in_specs,
        out_specs=out_specs,
        out_shape=out_shape,
        scratch_shapes=[pltpu.VMEM((rows + 2 * SUBLANE, LANE), F32),
                        pltpu.VMEM((2, rows, LANE), F32), pltpu.VMEM((2, rows, LANE), F32)]
                       + [pltpu.VMEM((2, 3, size // radix, LANE), F32) for size, radix in _scan_levels(n)[0]],
        compiler_params=_params(("arbitrary", "arbitrary"), 2 * 2 * rows * LANE * 4 + 12 * rows * LANE * 4),
    )(*args)


def kernel(x_prompt, x_sample, cache_attn_k, cache_attn_v, state_gla, state_lru, c, c_ctx, w_ada, b_ada, norm1_g, w_in, attn_sink, gla_w_decay, gla_b_decay, gla_norm_g, lru_conv_w, lru_conv_b, lru_w_rgate, lru_b_rgate, lru_w_igate, lru_b_igate, lru_lambda, w_out, norm2_g, w_gu, w_down, final_norm_g):
    nb_c, s_len, d = x_prompt.shape
    nb_l, n_lat, _ = x_sample.shape
    depth = w_in.shape[0]
    lru_w = lru_conv_w.shape[2]
    qw, kw = ATT_HEADS * HEAD_DIM, ATT_KV_HEADS * HEAD_DIM
    gdec0 = qw + 2 * kw + 4 * GLA_HEADS * GLA_DK
    gdec1 = gdec0 + 2 * GLA_LOWRANK
    in_tile = 512
    tail_w = 2 * lru_w + 2 * GLA_LOWRANK
    tail_pad = -(-tail_w // LANE) * LANE
    assert gdec0 % in_tile == 0 and w_in.shape[2] == gdec0 + tail_w

    w_in_b = w_in.astype(BF16)
    w_tail_b = jnp.concatenate(
        [w_in_b[:, :, gdec1:], w_in_b[:, :, gdec0:gdec1],
         jnp.zeros((depth, d, tail_pad - tail_w), BF16)], axis=2)
    dec_col = 2 * lru_w // LANE
    w_out_b = w_out.astype(BF16)
    w_gu_b = w_gu.astype(BF16)
    w_down_b = w_down.astype(BF16)

    cvecs = jnp.concatenate([c_ctx[None], c, jnp.zeros((MOD_ROWS - 1 - nb_l, d), F32)], axis=0)
    modr = _ada(cvecs, w_ada, b_ada).reshape(depth, MOD_ROWS, 6, 1, d)
    row_ctx = _ModRows(0, nb_c * s_len)
    row_lat = _ModRows(1, n_lat)

    tables = _rope_tables(n_lat)
    cache_k = cache_attn_k.reshape(nb_l, depth, cache_attn_k.shape[2], kw)
    cache_v = cache_attn_v.reshape(nb_l, depth, cache_attn_v.shape[2], kw)
    lru_init = state_lru.reshape(nb_l, depth, 2, 1, lru_w)

    xc = x_prompt.reshape(nb_c * s_len, d)
    xl = x_sample.reshape(nb_l * n_lat, d)
    new_k, new_v, new_gla, new_lru = [], [], [], []
    for l in range(depth):
        wdec = gla_w_decay[l].reshape(2, GLA_LOWRANK, GLA_HEADS, GLA_DK).transpose(0, 2, 1, 3)
        zpad = jnp.zeros((GLA_HEADS, GLA_LOWRANK, GLA_DK), F32)
        zrest = jnp.zeros((GLA_HEADS, LANE - 2 * GLA_LOWRANK, GLA_DK), F32)
        wf = jnp.concatenate([wdec[0], zpad, zrest], axis=1).astype(BF16)
        wb = jnp.concatenate([zpad, wdec[1], zrest], axis=1).astype(BF16)
        b_dec = gla_b_decay[l].reshape(2, GLA_HEADS, 1, GLA_DK)
        ng = gla_norm_g[l].reshape(GLA_HEADS, 1, LANE)
        lru_args = (lru_conv_w[l], lru_conv_b[l].reshape(1, lru_w),
                    lru_w_rgate[l].astype(BF16), lru_b_rgate[l].reshape(2, 1, lru_w),
                    lru_w_igate[l].astype(BF16), lru_b_igate[l].reshape(2, 1, lru_w),
                    lru_lambda[l].reshape(2, 1, lru_w))
        sink = attn_sink[l]

        def ffn(x, mix, row_of):
            x = _out_proj(*mix, w_out_b, x, modr, l, row_of, 2)
            h2 = _norm_mod(x, norm2_g[l], modr, l, row_of, 4, 3)
            act = _gate_up(h2, w_gu_b, l)
            return _down_proj(act, w_down_b, x, modr, l, row_of, 5)

        h = _norm_mod(xc, norm1_g[l], modr, l, row_ctx, 1, 0)
        proj = _in_proj(h, w_in_b, l, gdec0)
        tail = _in_proj(h, w_tail_b, l, tail_pad)
        attn = _attn_ctx(proj, sink, nb_c, s_len)
        gla, gla_fin = _gla(proj, tail, dec_col, wf, wb, b_dec, ng, nb_c, s_len)
        lru, lru_fin = _lru(tail, *lru_args, nb_c, s_len)
        new_k.append(proj[:, qw:qw + kw].reshape(nb_c, s_len, ATT_KV_HEADS, HEAD_DIM))
        new_v.append(proj[:, qw + kw:qw + 2 * kw].reshape(nb_c, s_len, ATT_KV_HEADS, HEAD_DIM))
        new_gla.append(gla_fin)
        new_lru.append(lru_fin.reshape(nb_c, 2, lru_w))
        xc = ffn(xc, (attn, gla, lru), row_ctx)

        h = _norm_mod(xl, norm1_g[l], modr, l, row_lat, 1, 0)
        proj = _in_proj(h, w_in_b, l, gdec0)
        tail = _in_proj(h, w_tail_b, l, tail_pad)
        attn = _attn_lat(proj, sink, cache_k, cache_v, l, tables, nb_l, n_lat)
        gla = _gla(proj, tail, dec_col, wf, wb, b_dec, ng, nb_l, n_lat, init=state_gla, layer=l)
        lru = _lru(tail, *lru_args, nb_l, n_lat, init=lru_init, layer=l)
        xl = ffn(xl, (attn, gla, lru), row_lat)

    y_prompt = _final_norm(xc, final_norm_g).reshape(nb_c, s_len, d)
    y_sample = _final_norm(xl, final_norm_g).reshape(nb_l, n_lat, d)
    return (y_prompt, y_sample, jnp.stack(new_k, axis=1), jnp.stack(new_v, axis=1),
            jnp.stack(new_gla, axis=1), jnp.stack(new_lru, axis=1))
```
